```python
import jax, jax.numpy as jnp
from jax import lax
import numpy as np

D_MODEL = 1024
BATCH = 2
SEQ = 16384
DEPTH = 4

N_MIXERS = 2
N_A_LAYERS = (DEPTH + 1) // 2
N_B_LAYERS = DEPTH // 2
D_FF = ((8 * D_MODEL // 3 + 127) // 128) * 128
RMS_EPS = 1e-6

A_HEADS = 4
A_DQK = D_MODEL // 8
A_DV = D_MODEL // 4
A_CHUNK = 128
A_CONV = 5
A_QK = 2 * A_HEADS * A_DQK
A_V = A_HEADS * A_DV
A_IN = A_QK + 2 * A_V + 4 * A_HEADS

B_GROUPS = ((128, 1), (512, 4), (2048, 16))
B_HEADS = 16
B_DH = D_MODEL // B_HEADS
B_WIDTH = B_HEADS * B_DH
B_IN = len(B_GROUPS) * 3 * B_WIDTH

kernel_name = "hybrid_mlstm_dilated_macaron"


def rmsnorm(x, g):
    xf = x.astype(jnp.float32)
    y = xf * lax.rsqrt(jnp.mean(xf * xf, axis=-1, keepdims=True) + RMS_EPS)
    return (y * g.astype(jnp.float32)).astype(x.dtype)


def swiglu(x, w_gate, w_up, w_down):
    return (jax.nn.silu(x @ w_gate) * (x @ w_up)) @ w_down


def alibi_slopes(n_heads):
    return jnp.exp2(-8.0 * (jnp.arange(n_heads, dtype=jnp.float32) + 1.0) / n_heads)


def _mlstm_one_direction(q, k, v, log_i, log_f):
    bsz, nh, s, dqk = q.shape
    dv = v.shape[-1]
    L = A_CHUNK
    nc = s // L
    q = q.reshape(bsz, nh, nc, L, dqk)
    k = k.reshape(bsz, nh, nc, L, dqk)
    v = v.reshape(bsz, nh, nc, L, dv)
    log_i = log_i.reshape(bsz, nh, nc, L)
    b = jnp.cumsum(log_f.reshape(bsz, nh, nc, L), axis=-1)
    b_tot = b[..., -1]
    a = b_tot[..., None] - b + log_i
    m_loc = jnp.max(a, axis=-1)
    kw = k * jnp.exp(a - m_loc[..., None])[..., None]
    c_loc = jnp.einsum('bhcsk,bhcsv->bhckv', kw, v)
    n_loc = jnp.sum(kw, axis=3)

    def step(carry, inp):
        c, n, m = carry
        cl, nl, ml, bt = inp
        m_new = jnp.maximum(bt + m, ml)
        s_prev = jnp.exp(bt + m - m_new)
        s_loc = jnp.exp(ml - m_new)
        c_new = s_prev[..., None, None] * c + s_loc[..., None, None] * cl
        n_new = s_prev[..., None] * n + s_loc[..., None] * nl
        return (c_new, n_new, m_new), (c, n, m)

    init = (jnp.zeros((bsz, nh, dqk, dv), q.dtype),
            jnp.zeros((bsz, nh, dqk), q.dtype),
            jnp.zeros((bsz, nh), q.dtype))
    xs = (jnp.moveaxis(c_loc, 2, 0), jnp.moveaxis(n_loc, 2, 0),
          jnp.moveaxis(m_loc, 2, 0), jnp.moveaxis(b_tot, 2, 0))
    _, (c_prev, n_prev, m_prev) = lax.scan(step, init, xs)
    c_prev = jnp.moveaxis(c_prev, 0, 2)
    n_prev = jnp.moveaxis(n_prev, 0, 2)
    m_prev = jnp.moveaxis(m_prev, 0, 2)

    g = b + m_prev[..., None]
    d = b[..., :, None] - b[..., None, :] + log_i[..., None, :]
    tri = jnp.tril(jnp.ones((L, L), dtype=bool))
    d = jnp.where(tri, d, -jnp.inf)
    m_t = jnp.maximum(g, jnp.max(d, axis=-1))
    p = jnp.exp(d - m_t[..., None]) * jnp.einsum('bhctk,bhcsk->bhcts', q, k)
    s_inter = jnp.exp(g - m_t)
    num = (s_inter[..., None] * jnp.einsum('bhctk,bhckv->bhctv', q, c_prev)
           + jnp.einsum('bhcts,bhcsv->bhctv', p, v))
    den = s_inter * jnp.einsum('bhctk,bhck->bhct', q, n_prev) + jnp.sum(p, axis=-1)
    h = num / jnp.maximum(jnp.abs(den), jnp.exp(-m_t))[..., None]
    return h.reshape(bsz, nh, s, dv)


def mlstm_mixer(h, w_in, conv_w, conv_b, gate_b, head_g, w_out):
    bsz, s, _ = h.shape
    f32 = jnp.float32
    proj = h @ w_in
    qk = proj[..., :A_QK]
    v = proj[..., A_QK:A_QK + A_V]
    o = proj[..., A_QK + A_V:A_QK + 2 * A_V]
    gates = proj[..., A_QK + 2 * A_V:].astype(f32) + gate_b.astype(f32)
    qk = lax.conv_general_dilated(qk, conv_w[:, None, :], (1,), [(A_CONV // 2, A_CONV // 2)],
                                  dimension_numbers=('NWC', 'WIO', 'NWC'),
                                  feature_group_count=A_QK)
    qk = jax.nn.silu(qk + conv_b)
    q = qk[..., :A_QK // 2].reshape(bsz, s, A_HEADS, A_DQK).transpose(0, 2, 1, 3).astype(f32) * (A_DQK ** -0.5)
    k = qk[..., A_QK // 2:].reshape(bsz, s, A_HEADS, A_DQK).transpose(0, 2, 1, 3).astype(f32)
    v = v.reshape(bsz, s, A_HEADS, A_DV).transpose(0, 2, 1, 3).astype(f32)
    gates = gates.reshape(bsz, s, 4, A_HEADS).transpose(2, 0, 3, 1)
    i_f, f_f, i_b, f_b = gates[0], gates[1], gates[2], gates[3]
    h_f = _mlstm_one_direction(q, k, v, i_f, jax.nn.log_sigmoid(f_f))
    flip = lambda t: jnp.flip(t, axis=2)
    h_b = flip(_mlstm_one_direction(flip(q), flip(k), flip(v), flip(i_b),
                                    flip(jax.nn.log_sigmoid(f_b))))
    y = h_f + h_b
    y = y * lax.rsqrt(jnp.mean(y * y, axis=-1, keepdims=True) + RMS_EPS)
    y = y.transpose(0, 2, 1, 3).reshape(bsz, s, A_V) * head_g.astype(f32) * jax.nn.sigmoid(o.astype(f32))
    return y.astype(h.dtype) @ w_out


def _dilated_group(q, k, v, dil, half, slopes):
    bsz, s, nh, dh = q.shape
    f32 = jnp.float32
    n = s // dil
    nb = -(-n // half)
    npad = nb * half

    def to_blocks(t):
        t = t.astype(f32).reshape(bsz, n, dil, nh, dh).transpose(0, 2, 1, 3, 4)
        t = jnp.pad(t, ((0, 0), (0, 0), (0, npad - n), (0, 0), (0, 0)))
        return t.reshape(bsz, dil, nb, half, nh, dh)

    def neighbours(t):
        tp = jnp.pad(t, ((0, 0), (0, 0), (1, 1), (0, 0), (0, 0), (0, 0)))
        return jnp.concatenate([tp[:, :, :-2], tp[:, :, 1:-1], tp[:, :, 2:]], axis=3)

    qb = to_blocks(q)
    kn = neighbours(to_blocks(k))
    vn = neighbours(to_blocks(v))
    qi = jnp.arange(nb)[:, None] * half + jnp.arange(half)[None, :]
    ki = (jnp.arange(nb)[:, None] - 1) * half + jnp.arange(3 * half)[None, :]
    rel = jnp.abs(ki[:, None, :] - qi[:, :, None])
    valid = (rel <= half) & (ki[:, None, :] >= 0) & (ki[:, None, :] < n)
    penalty = slopes[None, :, None, None] * (rel * dil).astype(f32)[:, None]
    sc = jnp.einsum('brcqhd,brckhd->brchqk', qb, kn) * (dh ** -0.5) - penalty
    sc = jnp.where(valid[:, None], sc, -jnp.inf)
    mx = jnp.max(sc, axis=-1, keepdims=True)
    e = jnp.exp(sc - mx)
    den = jnp.sum(e, axis=-1, keepdims=True)
    out = jnp.einsum('brchqk,brckhd->brcqhd', e / den, vn)
    lse = (mx + jnp.log(den))[..., 0]
    out = out.reshape(bsz, dil, npad, nh, dh)[:, :, :n].transpose(0, 2, 1, 3, 4).reshape(bsz, s, nh, dh)
    lse = lse.transpose(0, 1, 2, 4, 3).reshape(bsz, dil, npad, nh)[:, :, :n]
    lse = lse.transpose(0, 2, 1, 3).reshape(bsz, s, nh)
    return out, lse


def dilated_mixer(h, w_in, w_out):
    bsz, s, _ = h.shape
    proj = (h @ w_in).reshape(bsz, s, len(B_GROUPS), 3, B_HEADS, B_DH)
    slopes = alibi_slopes(B_HEADS)
    outs, lses = [], []
    for gi, (window, dil) in enumerate(B_GROUPS):
        o, l = _dilated_group(proj[:, :, gi, 0], proj[:, :, gi, 1], proj[:, :, gi, 2],
                              dil, window // (2 * dil), slopes)
        outs.append(o)
        lses.append(l)
    alpha = jax.nn.softmax(jnp.stack(lses, axis=0), axis=0)
    y = jnp.sum(alpha[..., None] * jnp.stack(outs, axis=0), axis=0).reshape(bsz, s, B_WIDTH)
    return y.astype(h.dtype) @ w_out


def setup_inputs(seed: int = 0) -> dict:
    key = jax.random.key(seed)
    ks = iter(jax.random.split(key, 24))
    f32 = jnp.float32

    def nrm(shape, scale):
        return jax.random.normal(next(ks), shape, f32) * scale

    def gain(shape):
        return 1.0 + nrm(shape, 0.05)

    x = nrm((BATCH, SEQ, D_MODEL), 1.0)
    norm_ffn1 = gain((DEPTH, D_MODEL))
    ffn1_gate = nrm((DEPTH, D_MODEL, D_FF), D_MODEL ** -0.5)
    ffn1_up = nrm((DEPTH, D_MODEL, D_FF), D_MODEL ** -0.5)
    ffn1_down = nrm((DEPTH, D_FF, D_MODEL), D_FF ** -0.5)
    norm_mix = gain((DEPTH, D_MODEL))
    a_w_in = nrm((N_A_LAYERS, D_MODEL, A_IN), D_MODEL ** -0.5)
    a_conv_w = nrm((N_A_LAYERS, A_CONV, A_QK), A_CONV ** -0.5)
    a_conv_b = nrm((N_A_LAYERS, A_QK), 0.02)
    ig_b = nrm((N_A_LAYERS, 2, A_HEADS), 0.1)
    fg_b = jnp.linspace(3.0, 6.0, A_HEADS, dtype=f32) + nrm((N_A_LAYERS, 2, A_HEADS), 0.1)
    a_gate_b = jnp.stack([ig_b[:, 0], fg_b[:, 0], ig_b[:, 1], fg_b[:, 1]], axis=1).reshape(N_A_LAYERS, 4 * A_HEADS)
    a_head_g = gain((N_A_LAYERS, A_V))
    a_w_out = nrm((N_A_LAYERS, A_V, D_MODEL), A_V ** -0.5)
    b_w_in = nrm((N_B_LAYERS, D_MODEL, B_IN), D_MODEL ** -0.5)
    b_w_out = nrm((N_B_LAYERS, B_WIDTH, D_MODEL), B_WIDTH ** -0.5)
    norm_ffn2 = gain((DEPTH, D_MODEL))
    ffn2_gate = nrm((DEPTH, D_MODEL, D_FF), D_MODEL ** -0.5)
    ffn2_up = nrm((DEPTH, D_MODEL, D_FF), D_MODEL ** -0.5)
    ffn2_down = nrm((DEPTH, D_FF, D_MODEL), D_FF ** -0.5)
    norm_final = gain((D_MODEL,))
    return {"x": x, "norm_ffn1": norm_ffn1, "ffn1_gate": ffn1_gate, "ffn1_up": ffn1_up,
            "ffn1_down": ffn1_down, "norm_mix": norm_mix, "a_w_in": a_w_in,
            "a_conv_w": a_conv_w, "a_conv_b": a_conv_b, "a_gate_b": a_gate_b,
            "a_head_g": a_head_g, "a_w_out": a_w_out, "b_w_in": b_w_in, "b_w_out": b_w_out,
            "norm_ffn2": norm_ffn2, "ffn2_gate": ffn2_gate, "ffn2_up": ffn2_up,
            "ffn2_down": ffn2_down, "norm_final": norm_final}


def reference(x, norm_ffn1, ffn1_gate, ffn1_up, ffn1_down, norm_mix, a_w_in, a_conv_w,
              a_conv_b, a_gate_b, a_head_g, a_w_out, b_w_in, b_w_out, norm_ffn2,
              ffn2_gate, ffn2_up, ffn2_down, norm_final):
    for i in range(DEPTH):
        x = x + 0.5 * swiglu(rmsnorm(x, norm_ffn1[i]), ffn1_gate[i], ffn1_up[i], ffn1_down[i])
        h = rmsnorm(x, norm_mix[i])
        j = i // N_MIXERS
        if i % N_MIXERS == 0:
            x = x + mlstm_mixer(h, a_w_in[j], a_conv_w[j], a_conv_b[j], a_gate_b[j],
                                a_head_g[j], a_w_out[j])
        else:
            x = x + dilated_mixer(h, b_w_in[j], b_w_out[j])
        x = x + 0.5 * swiglu(rmsnorm(x, norm_ffn2[i]), ffn2_gate[i], ffn2_up[i], ffn2_down[i])
    return rmsnorm(x, norm_final)
```

```python
import functools

import jax
import jax.numpy as jnp
from jax import lax
from jax.experimental import pallas as pl
from jax.experimental.pallas import tpu as pltpu

F32 = jnp.float32
BF16 = jnp.bfloat16

D_MODEL = 1024
DEPTH = 4
D_FF = 2816
RMS_EPS = 1e-6

A_HEADS = 4
A_DQK = 128
A_DV = 256
A_CHUNK = 128
A_CONV = 5
A_QK = 2 * A_HEADS * A_DQK
A_V = A_HEADS * A_DV

B_GROUPS = ((128, 1), (512, 4), (2048, 16))
B_HEADS = 16
B_DH = 64
B_WIDTH = B_HEADS * B_DH
B_HALF = 64
B_TQ = 128
B_TK = B_TQ + 2 * B_HALF
LSE_LANES = 128 // B_HEADS
NEG_BIG = -1e30

ROW_TILE = 512
A_STEP_CHUNKS = 2
VMEM_LIMIT = 56 * 1024 * 1024


def _cparams(*sem):
    return pltpu.CompilerParams(dimension_semantics=sem, vmem_limit_bytes=VMEM_LIMIT)


def _resident(shape):
    nd = len(shape)
    return pl.BlockSpec(shape, lambda *_: (0,) * nd, pipeline_mode=pl.Buffered(1))


def _rms(x, g):
    ms = jnp.mean(x * x, axis=-1, keepdims=True)
    return x * lax.rsqrt(ms + RMS_EPS) * g


def _dot(a, b):
    return jnp.dot(a, b, preferred_element_type=F32)


def _dot_nt(a, b):
    return lax.dot_general(a, b, (((1,), (1,)), ((), ())), preferred_element_type=F32)


def _dot_tn(a, b):
    return lax.dot_general(a, b, (((0,), (0,)), ((), ())), preferred_element_type=F32)


def _split3(x):
    hi = x.astype(BF16)
    r1 = x - hi.astype(F32)
    mid = r1.astype(BF16)
    lo = (r1 - mid.astype(F32)).astype(BF16)
    return hi, mid, lo


def _ffn_kernel(x_ref, g_ref, wg_ref, wu_ref, wd_ref, gf_ref, o_ref, *, final_norm):
    x = x_ref[...]
    xn = _rms(x, g_ref[...]).astype(BF16)
    gate = _dot(xn, wg_ref[...])
    up = _dot(xn, wu_ref[...])
    act = (gate * jax.nn.sigmoid(gate) * up).astype(BF16)
    y = x + 0.5 * _dot(act, wd_ref[...])
    if final_norm:
        y = _rms(y, gf_ref[...])
    o_ref[...] = y


def _ffn(x, g, wg, wu, wd, g_final, final_norm):
    m, d = x.shape
    f = wg.shape[1]
    row = pl.BlockSpec((ROW_TILE, d), lambda i: (i, 0))
    return pl.pallas_call(
        functools.partial(_ffn_kernel, final_norm=final_norm),
        grid=(m // ROW_TILE,),
        in_specs=[row, _resident((1, d)), _resident((d, f)), _resident((d, f)),
                  _resident((f, d)), _resident((1, d))],
        out_specs=row,
        out_shape=jax.ShapeDtypeStruct((m, d), F32),
        compiler_params=_cparams("parallel"),
        name="ffn",
    )(x, g, wg, wu, wd, g_final)


def _a_proj_kernel(x_ref, g_ref, w_ref, wli_ref, wlf_ref, wliT_ref, wlfT_ref,
                   bli_ref, blf_ref, bliT_ref, blfT_ref,
                   qk_ref, v_ref, o_ref, li_ref, lf_ref, liT_ref, lfT_ref):
    xn = _rms(x_ref[...], g_ref[...]).astype(BF16)
    qk_ref[...] = _dot(xn, w_ref[:, 0:A_QK])
    v_ref[...] = _dot(xn, w_ref[:, A_QK:A_QK + A_V]).astype(BF16)
    o_ref[...] = _dot(xn, w_ref[:, A_QK + A_V:A_QK + 2 * A_V])
    li_ref[...] = _dot(xn, wli_ref[...]) + bli_ref[...]
    lf_ref[...] = jax.nn.log_sigmoid(_dot(xn, wlf_ref[...]) + blf_ref[...])
    liT_ref[...] = _dot_nt(wliT_ref[...], xn) + bliT_ref[...]
    lfT_ref[...] = jax.nn.log_sigmoid(_dot_nt(wlfT_ref[...], xn) + blfT_ref[...])


def _a_proj(x, g, w_main, w_li, w_lf, b_li, b_lf):
    m, d = x.shape
    ng = w_li.shape[1]
    row = lambda n: pl.BlockSpec((ROW_TILE, n), lambda i: (i, 0))
    col = pl.BlockSpec((ng, ROW_TILE), lambda i: (0, i))
    return pl.pallas_call(
        _a_proj_kernel,
        grid=(m // ROW_TILE,),
        in_specs=[row(d), _resident((1, d)), _resident(w_main.shape),
                  _resident((d, ng)), _resident((d, ng)), _resident((ng, d)), _resident((ng, d)),
                  _resident((1, ng)), _resident((1, ng)), _resident((ng, 1)), _resident((ng, 1))],
        out_specs=[row(A_QK), row(A_V), row(A_V), row(ng), row(ng), col, col],
        out_shape=[jax.ShapeDtypeStruct((m, A_QK), F32),
                   jax.ShapeDtypeStruct((m, A_V), BF16),
                   jax.ShapeDtypeStruct((m, A_V), F32),
                   jax.ShapeDtypeStruct((m, ng), F32),
                   jax.ShapeDtypeStruct((m, ng), F32),
                   jax.ShapeDtypeStruct((ng, m), F32),
                   jax.ShapeDtypeStruct((ng, m), F32)],
        compiler_params=_cparams("parallel"),
        name="mlstm_proj",
    )(x, g, w_main, w_li, w_lf, w_li.T, w_lf.T, b_li, b_lf, b_li.T, b_lf.T)


def _conv_kernel(x_ref, prev_ref, next_ref, w_ref, b_ref, o_ref, *, n_steps):
    i = pl.program_id(1)
    x = x_ref[...]
    rows = x.shape[0]
    prev = prev_ref[...] * (i > 0).astype(F32)
    nxt = next_ref[...] * (i < n_steps - 1).astype(F32)
    ridx = lax.broadcasted_iota(jnp.int32, x.shape, 0)
    half = A_CONV // 2
    acc = x * w_ref[half:half + 1, :] + b_ref[...]
    for s in range(1, half + 1):
        back = pltpu.roll(x, s, axis=0)
        fwd = pltpu.roll(x, rows - s, axis=0)
        for e in range(s):
            back = jnp.where(ridx == e, prev[8 - s + e:8 - s + e + 1, :], back)
            fwd = jnp.where(ridx == rows - s + e, nxt[e:e + 1, :], fwd)
        acc = acc + back * w_ref[half - s:half - s + 1, :] + fwd * w_ref[half + s:half + s + 1, :]
    y = acc * jax.nn.sigmoid(acc)
    lane = lax.broadcasted_iota(jnp.int32, x.shape, 1)
    y = jnp.where(lane < A_QK // 2, y * (A_DQK ** -0.5), y)
    o_ref[...] = y.astype(BF16)


def _conv(qk, w, b, bsz, seq):
    c = qk.shape[-1]
    qk = qk.reshape(bsz, seq, c)
    n_steps = seq // ROW_TILE
    r8 = ROW_TILE // 8
    return pl.pallas_call(
        functools.partial(_conv_kernel, n_steps=n_steps),
        grid=(bsz, n_steps),
        in_specs=[pl.BlockSpec((None, ROW_TILE, c), lambda b_, i: (b_, i, 0)),
                  pl.BlockSpec((None, 8, c), lambda b_, i: (b_, jnp.maximum(i * r8 - 1, 0), 0)),
                  pl.BlockSpec((None, 8, c), lambda b_, i: (b_, jnp.minimum((i + 1) * r8, seq // 8 - 1), 0)),
                  _resident((A_CONV, c)), _resident((1, c))],
        out_specs=pl.BlockSpec((None, ROW_TILE, c), lambda b_, i: (b_, i, 0)),
        out_shape=jax.ShapeDtypeStruct((bsz, seq, c), BF16),
        compiler_params=_cparams("parallel", "parallel"),
        name="mlstm_conv",
    )(qk, qk, qk, w, b)


def _mlstm_kernel(q_ref, k_ref, v_ref, li_ref, lf_ref, liT_ref, lfT_ref, h_ref,
                  c_ref, n_ref, m_ref, *, reverse):
    L = A_CHUNK
    lo = 4 if reverse else 0

    @pl.when(pl.program_id(1) == 0)
    def _():
        c_ref[...] = jnp.zeros_like(c_ref)
        n_ref[...] = jnp.zeros_like(n_ref)
        m_ref[...] = jnp.zeros_like(m_ref)

    t_idx = lax.broadcasted_iota(jnp.int32, (L, L), 0)
    s_idx = lax.broadcasted_iota(jnp.int32, (L, L), 1)
    seen = (s_idx >= t_idx) if reverse else (s_idx <= t_idx)
    seen_bf = seen.astype(F32).astype(BF16)
    last = 0 if reverse else L - 1

    order = range(A_STEP_CHUNKS - 1, -1, -1) if reverse else range(A_STEP_CHUNKS)
    for j in order:
        r0 = j * L
        li = li_ref[r0:r0 + L, :]
        lf = lf_ref[r0:r0 + L, :]
        liT = liT_ref[:, r0:r0 + L]
        lfT = lfT_ref[:, r0:r0 + L]
        b_col = sum(_dot(seen_bf, p) for p in _split3(lf))
        b_row = sum(_dot_nt(p, seen_bf) for p in _split3(lfT))
        b_tot = b_col[last:last + 1, :]
        a_col = b_tot - b_col + li
        m_loc = jnp.max(a_col, axis=0, keepdims=True)
        w_col = jnp.exp(a_col - m_loc)
        m_prev = m_ref[...]
        g_col = b_col + m_prev
        r_row = liT - b_row
        m_new = jnp.maximum(b_tot + m_prev, m_loc)
        s_prev = jnp.exp(b_tot + m_prev - m_new)
        s_loc = jnp.exp(m_loc - m_new)
        m_ref[...] = m_new

        for h in range(A_HEADS):
            gi = lo + h
            q = q_ref[r0:r0 + L, h * A_DQK:(h + 1) * A_DQK]
            k = k_ref[r0:r0 + L, h * A_DQK:(h + 1) * A_DQK]
            v = v_ref[r0:r0 + L, h * A_DV:(h + 1) * A_DV]
            c_prev = c_ref[h]
            n_prev = n_ref[h]

            d = b_col[:, gi:gi + 1] + r_row[gi:gi + 1, :]
            d = jnp.where(seen, d, -jnp.inf)
            g = g_col[:, gi:gi + 1]
            m_t = jnp.maximum(g, jnp.max(d, axis=-1, keepdims=True))
            p = jnp.exp(d - m_t) * _dot_nt(q, k)
            s_inter = jnp.exp(g - m_t)
            num = s_inter * _dot(q, c_prev.astype(BF16)) + _dot(p.astype(BF16), v)
            qn = jnp.sum(q.astype(F32) * n_prev, axis=-1, keepdims=True)
            den = s_inter * qn + jnp.sum(p, axis=-1, keepdims=True)
            out = num / jnp.maximum(jnp.abs(den), jnp.exp(-m_t))
            h_ref[r0:r0 + L, h * A_DV:(h + 1) * A_DV] = out

            kw = k.astype(F32) * w_col[:, gi:gi + 1]
            c_loc = _dot_tn(kw.astype(BF16), v)
            n_loc = jnp.sum(kw, axis=0, keepdims=True)
            sp = s_prev[:, gi:gi + 1]
            sl = s_loc[:, gi:gi + 1]
            c_ref[h] = sp * c_prev + sl * c_loc
            n_ref[h] = sp * n_prev + sl * n_loc


def _mlstm(qk, v, li, lf, liT, lfT, bsz, seq, reverse):
    tl = A_STEP_CHUNKS * A_CHUNK
    ns = seq // tl
    ng = li.shape[-1]
    blk = (lambda i: ns - 1 - i) if reverse else (lambda i: i)
    v = v.reshape(bsz, seq, A_V)
    li = li.reshape(bsz, seq, ng)
    lf = lf.reshape(bsz, seq, ng)
    gate = pl.BlockSpec((None, tl, ng), lambda b_, i: (b_, blk(i), 0))
    gate_t = pl.BlockSpec((ng, tl), lambda b_, i: (0, b_ * ns + blk(i)))
    return pl.pallas_call(
        functools.partial(_mlstm_kernel, reverse=reverse),
        grid=(bsz, ns),
        in_specs=[pl.BlockSpec((None, tl, A_QK // 2), lambda b_, i: (b_, blk(i), 0)),
                  pl.BlockSpec((None, tl, A_QK // 2), lambda b_, i: (b_, blk(i), 1)),
                  pl.BlockSpec((None, tl, A_V), lambda b_, i: (b_, blk(i), 0)),
                  gate, gate, gate_t, gate_t],
        out_specs=pl.BlockSpec((None, tl, A_V), lambda b_, i: (b_, blk(i), 0)),
        out_shape=jax.ShapeDtypeStruct((bsz, seq, A_V), F32),
        scratch_shapes=[pltpu.VMEM((A_HEADS, A_DQK, A_DV), F32),
                        pltpu.VMEM((A_HEADS, 1, A_DQK), F32),
                        pltpu.VMEM((1, ng), F32)],
        compiler_params=_cparams("parallel", "arbitrary"),
        name="mlstm_bwd" if reverse else "mlstm_fwd",
    )(qk, qk, v, li, lf, liT, lfT)


def _a_out_kernel(x_ref, hf_ref, hb_ref, o_ref, hg_ref, w_ref, y_ref):
    y = hf_ref[...] + hb_ref[...]
    parts = []
    for h in range(A_HEADS):
        yh = y[:, h * A_DV:(h + 1) * A_DV]
        ms = jnp.mean(yh * yh, axis=-1, keepdims=True)
        parts.append(yh * lax.rsqrt(ms + RMS_EPS))
    yn = jnp.concatenate(parts, axis=-1)
    z = (yn * hg_ref[...] * jax.nn.sigmoid(o_ref[...])).astype(BF16)
    y_ref[...] = x_ref[...] + _dot(z, w_ref[...])


def _a_out(x, hf, hb, o, hg, w):
    m, d = x.shape
    row = pl.BlockSpec((ROW_TILE, d), lambda i: (i, 0))
    return pl.pallas_call(
        _a_out_kernel,
        grid=(m // ROW_TILE,),
        in_specs=[row, row, row, row, _resident((1, d)), _resident((d, d))],
        out_specs=row,
        out_shape=jax.ShapeDtypeStruct((m, d), F32),
        compiler_params=_cparams("parallel"),
        name="mlstm_out",
    )(x, hf, hb, o, hg, w)


def _b_proj_kernel(x_ref, g_ref, w_ref, o_ref, *, n_slabs):
    xn = _rms(x_ref[...], g_ref[...]).astype(BF16)
    for j in range(n_slabs):
        y = _dot(xn, w_ref[:, j * B_WIDTH:(j + 1) * B_WIDTH])
        if j % 3 == 0:
            y = y * (B_DH ** -0.5)
        o_ref[j] = y.astype(BF16)


def _b_proj(x, g, w):
    m, d = x.shape
    n_slabs = w.shape[1] // B_WIDTH
    return pl.pallas_call(
        functools.partial(_b_proj_kernel, n_slabs=n_slabs),
        grid=(m // ROW_TILE,),
        in_specs=[pl.BlockSpec((ROW_TILE, d), lambda i: (i, 0)), _resident((1, d)),
                  _resident(w.shape)],
        out_specs=pl.BlockSpec((n_slabs, ROW_TILE, B_WIDTH), lambda i: (0, i, 0)),
        out_shape=jax.ShapeDtypeStruct((n_slabs, m, B_WIDTH), BF16),
        compiler_params=_cparams("parallel"),
        name="dilated_proj",
    )(x, g, w)


def _b_attn_kernel(q_ref, kp_ref, kc_ref, kn_ref, vp_ref, vc_ref, vn_ref, bias_ref,
                   o_ref, lse_ref):
    k_all = jnp.concatenate([kp_ref[...], kc_ref[...], kn_ref[...]], axis=0)
    v_all = jnp.concatenate([vp_ref[...], vc_ref[...], vn_ref[...]], axis=0)
    lane = lax.broadcasted_iota(jnp.int32, (B_TQ, 128), 1)
    low = lane < B_DH
    lse_row = jnp.zeros((B_TQ, 128), F32)
    for hp in range(B_HEADS // 2):
        cols = slice(hp * 128, (hp + 1) * 128)
        q2 = q_ref[:, cols]
        k2 = k_all[:, cols]
        v2 = v_all[:, cols]
        zero = jnp.zeros_like(q2)
        outs = []
        for par, qsel in ((0, jnp.where(low, q2, zero)), (1, jnp.where(low, zero, q2))):
            hd = 2 * hp + par
            s = _dot_nt(qsel, k2) + bias_ref[hd]
            mx = jnp.max(s, axis=-1, keepdims=True)
            e = jnp.exp(s - mx)
            den = jnp.sum(e, axis=-1, keepdims=True)
            outs.append(_dot(e.astype(BF16), v2) / den)
            lse = mx + jnp.log(den)
            lse_row = jnp.where(lane // LSE_LANES == hd, lse, lse_row)
        o_ref[:, cols] = jnp.where(low, outs[0], outs[1]).astype(BF16)
    lse_ref[...] = lse_row


def _b_attn(q, k, v, bias, bsz, seq, dil):
    n = seq // dil
    nq = n // B_TQ
    w = B_WIDTH
    q = q.reshape(bsz, n, dil * w)
    k = k.reshape(bsz, n, dil * w)
    v = v.reshape(bsz, n, dil * w)
    hb = B_TQ // B_HALF
    last_half = n // B_HALF - 1
    cur = pl.BlockSpec((None, B_TQ, w), lambda b_, r, i: (b_, i, r))
    prev = pl.BlockSpec((None, B_HALF, w), lambda b_, r, i: (b_, jnp.maximum(i * hb - 1, 0), r))
    nxt = pl.BlockSpec((None, B_HALF, w), lambda b_, r, i: (b_, jnp.minimum((i + 1) * hb, last_half), r))
    variant = lambda i: jnp.where(i == 0, 1, 0) + jnp.where(i == nq - 1, 2, 0)
    bias_spec = pl.BlockSpec((None, B_HEADS, B_TQ, B_TK), lambda b_, r, i: (variant(i), 0, 0, 0))
    out, lse = pl.pallas_call(
        _b_attn_kernel,
        grid=(bsz, dil, nq),
        in_specs=[cur, prev, cur, nxt, prev, cur, nxt, bias_spec],
        out_specs=[cur, pl.BlockSpec((None, B_TQ, 128), lambda b_, r, i: (b_, i, r))],
        out_shape=[jax.ShapeDtypeStruct((bsz, n, dil * w), BF16),
                   jax.ShapeDtypeStruct((bsz, n, dil * 128), F32)],
        compiler_params=_cparams("parallel", "parallel", "arbitrary"),
        name=f"dilated_attn_d{dil}",
    )(q, k, k, k, v, v, v, bias)
    return out.reshape(bsz * seq, w), lse.reshape(bsz * seq, 128)


def _attn_bias(dil):
    slopes = jnp.exp2(-8.0 * (jnp.arange(B_HEADS, dtype=F32) + 1.0) / B_HEADS)
    tq = jnp.arange(B_TQ)[:, None]
    col = jnp.arange(B_TK)[None, :]
    rel = jnp.abs(col - B_HALF - tq)
    base = -slopes[:, None, None] * (rel * dil).astype(F32)[None]
    band = (rel <= B_HALF)[None]
    first = (col >= B_HALF)[None]
    lastv = (col < B_TQ + B_HALF)[None]
    variants = []
    for need_first, need_last in ((False, False), (True, False), (False, True), (True, True)):
        ok = band
        if need_first:
            ok = ok & first
        if need_last:
            ok = ok & lastv
        variants.append(jnp.where(ok, base, NEG_BIG))
    return jnp.stack(variants, axis=0)


def _b_out_kernel(x_ref, o0_ref, o1_ref, o2_ref, l0_ref, l1_ref, l2_ref, e_ref, w_ref, y_ref):
    ls = [l0_ref[...], l1_ref[...], l2_ref[...]]
    mx = jnp.maximum(jnp.maximum(ls[0], ls[1]), ls[2])
    es = [jnp.exp(l - mx) for l in ls]
    den = es[0] + es[1] + es[2]
    acc = None
    for e, o_ref in zip(es, (o0_ref, o1_ref, o2_ref)):
        alpha = e / den
        wide = sum(_dot(p, e_ref[...]) for p in _split3(alpha))
        term = wide * o_ref[...].astype(F32)
        acc = term if acc is None else acc + term
    y_ref[...] = x_ref[...] + _dot(acc.astype(BF16), w_ref[...])


def _b_out(x, outs, lses, w):
    m, d = x.shape
    head_of_col = jnp.arange(B_WIDTH) // B_DH
    expand = (jnp.arange(128)[:, None] == head_of_col[None, :] * LSE_LANES).astype(BF16)
    row = pl.BlockSpec((ROW_TILE, d), lambda i: (i, 0))
    lrow = pl.BlockSpec((ROW_TILE, 128), lambda i: (i, 0))
    return pl.pallas_call(
        _b_out_kernel,
        grid=(m // ROW_TILE,),
        in_specs=[row, row, row, row, lrow, lrow, lrow, _resident((128, B_WIDTH)),
                  _resident((B_WIDTH, d))],
        out_specs=row,
        out_shape=jax.ShapeDtypeStruct((m, d), F32),
        compiler_params=_cparams("parallel"),
        name="dilated_out",
    )(x, *outs, *lses, expand, w)


def _mlstm_mixer(x, g, w_in, conv_w, conv_b, gate_b, head_g, w_out, bsz, seq):
    w_main = w_in[:, :A_QK + 2 * A_V].astype(BF16)
    wg = w_in[:, A_QK + 2 * A_V:]
    H = A_HEADS
    pick = lambda a, s: jnp.concatenate([a[..., s * H:(s + 1) * H], a[..., (s + 2) * H:(s + 3) * H]], axis=-1)
    w_li, w_lf = pick(wg, 0).astype(BF16), pick(wg, 1).astype(BF16)
    gb = gate_b.astype(F32)[None, :]
    b_li, b_lf = pick(gb, 0), pick(gb, 1)
    qk, v, o, li, lf, liT, lfT = _a_proj(x, g, w_main, w_li, w_lf, b_li, b_lf)
    qk = _conv(qk, conv_w.astype(F32), conv_b.astype(F32)[None, :], bsz, seq)
    hf = _mlstm(qk, v, li, lf, liT, lfT, bsz, seq, reverse=False)
    hb = _mlstm(qk, v, li, lf, liT, lfT, bsz, seq, reverse=True)
    m = bsz * seq
    return _a_out(x, hf.reshape(m, A_V), hb.reshape(m, A_V), o,
                  head_g.astype(F32)[None, :], w_out.astype(BF16))


def _dilated_mixer(x, g, w_in, w_out, bsz, seq):
    slabs = _b_proj(x, g, w_in.astype(BF16))
    outs, lses = [], []
    for gi, (_, dil) in enumerate(B_GROUPS):
        o, l = _b_attn(slabs[3 * gi], slabs[3 * gi + 1], slabs[3 * gi + 2], _attn_bias(dil),
                       bsz, seq, dil)
        outs.append(o)
        lses.append(l)
    return _b_out(x, outs, lses, w_out.astype(BF16))


def kernel(x, norm_ffn1, ffn1_gate, ffn1_up, ffn1_down, norm_mix, a_w_in, a_conv_w, a_conv_b,
           a_gate_b, a_head_g, a_w_out, b_w_in, b_w_out, norm_ffn2, ffn2_gate, ffn2_up,
           ffn2_down, norm_final):
    bsz, seq, d = x.shape
    h = x.reshape(bsz * seq, d)
    row = lambda a: a.astype(F32)[None, :]
    g_final = row(norm_final)
    for i in range(DEPTH):
        h = _ffn(h, row(norm_ffn1[i]), ffn1_gate[i].astype(BF16), ffn1_up[i].astype(BF16),
                 ffn1_down[i].astype(BF16), g_final, False)
        j = i // 2
        if i % 2 == 0:
            h = _mlstm_mixer(h, row(norm_mix[i]), a_w_in[j], a_conv_w[j], a_conv_b[j],
                             a_gate_b[j], a_head_g[j], a_w_out[j], bsz, seq)
        else:
            h = _dilated_mixer(h, row(norm_mix[i]), b_w_in[j], b_w_out[j], bsz, seq)
        h = _ffn(h, row(norm_ffn2[i]), ffn2_gate[i].astype(BF16), ffn2_up[i].astype(BF16),
                 ffn2_down[i].astype(BF16), g_final, i == DEPTH - 1)
    return h.reshape(bsz, seq, d)
```

```python
import functools

import jax
import jax.numpy as jnp
from jax import lax
from jax.experimental import pallas as pl
from jax.experimental.pallas import tpu as pltpu

F32 = jnp.float32
BF16 = jnp.bfloat16

D_MODEL = 1024
DEPTH = 4
D_FF = 2816
RMS_EPS = 1e-6

A_HEADS = 4
A_DQK = 128
A_DV = 256
A_CHUNK = 128
A_CONV = 5
A_QK = 2 * A_HEADS * A_DQK
A_V = A_HEADS * A_DV

B_GROUPS = ((128, 1), (512, 4), (2048, 16))
B_HEADS = 16
B_DH = 64
B_WIDTH = B_HEADS * B_DH
B_HALF = 64
B_TQ = 128
B_TK = B_TQ + 2 * B_HALF
LSE_LANES = 128 // B_HEADS
NEG_BIG = -1e30

ROW_TILE = 512
A_STEP_CHUNKS = 2
VMEM_LIMIT = 56 * 1024 * 1024


def _cparams(*sem):
    return pltpu.CompilerParams(dimension_semantics=sem, vmem_limit_bytes=VMEM_LIMIT)


def _resident(shape):
    nd = len(shape)
    return pl.BlockSpec(shape, lambda *_: (0,) * nd, pipeline_mode=pl.Buffered(1))


def _rms(x, g):
    ms = jnp.mean(x * x, axis=-1, keepdims=True)
    return x * lax.rsqrt(ms + RMS_EPS) * g


def _dot(a, b):
    return jnp.dot(a, b, preferred_element_type=F32)


def _dot_nt(a, b):
    return lax.dot_general(a, b, (((1,), (1,)), ((), ())), preferred_element_type=F32)


def _dot_tn(a, b):
    return lax.dot_general(a, b, (((0,), (0,)), ((), ())), preferred_element_type=F32)


def _split3(x):
    hi = x.astype(BF16)
    r1 = x - hi.astype(F32)
    mid = r1.astype(BF16)
    lo = (r1 - mid.astype(F32)).astype(BF16)
    return hi, mid, lo


def _ffn_kernel(x_ref, g_ref, wg_ref, wu_ref, wd_ref, gf_ref, o_ref, *, final_norm):
    x = x_ref[...]
    xn = _rms(x, g_ref[...]).astype(BF16)
    gate = _dot(xn, wg_ref[...])
    up = _dot(xn, wu_ref[...])
    act = (gate * jax.nn.sigmoid(gate) * up).astype(BF16)
    y = x + 0.5 * _dot(act, wd_ref[...])
    if final_norm:
        y = _rms(y, gf_ref[...])
    o_ref[...] = y


def _ffn(x, g, wg, wu, wd, g_final, final_norm):
    m, d = x.shape
    f = wg.shape[1]
    row = pl.BlockSpec((ROW_TILE, d), lambda i: (i, 0))
    return pl.pallas_call(
        functools.partial(_ffn_kernel, final_norm=final_norm),
        grid=(m // ROW_TILE,),
        in_specs=[row, _resident((1, d)), _resident((d, f)), _resident((d, f)),
                  _resident((f, d)), _resident((1, d))],
        out_specs=row,
        out_shape=jax.ShapeDtypeStruct((m, d), F32),
        compiler_params=_cparams("parallel"),
        name="ffn",
    )(x, g, wg, wu, wd, g_final)


def _a_proj_kernel(x_ref, g_ref, w_ref, wli_ref, wlf_ref, wliT_ref, wlfT_ref,
                   bli_ref, blf_ref, bliT_ref, blfT_ref,
                   qk_ref, v_ref, o_ref, li_ref, lf_ref, liT_ref, lfT_ref):
    xn = _rms(x_ref[...], g_ref[...]).astype(BF16)
    qk_ref[...] = _dot(xn, w_ref[:, 0:A_QK])
    v_ref[...] = _dot(xn, w_ref[:, A_QK:A_QK + A_V]).astype(BF16)
    o_ref[...] = _dot(xn, w_ref[:, A_QK + A_V:A_QK + 2 * A_V])
    li_ref[...] = _dot(xn, wli_ref[...]) + bli_ref[...]
    lf_ref[...] = jax.nn.log_sigmoid(_dot(xn, wlf_ref[...]) + blf_ref[...])
    liT_ref[...] = _dot_nt(wliT_ref[...], xn) + bliT_ref[...]
    lfT_ref[...] = jax.nn.log_sigmoid(_dot_nt(wlfT_ref[...], xn) + blfT_ref[...])


def _a_proj(x, g, w_main, w_li, w_lf, b_li, b_lf):
    m, d = x.shape
    ng = w_li.shape[1]
    row = lambda n: pl.BlockSpec((ROW_TILE, n), lambda i: (i, 0))
    col = pl.BlockSpec((ng, ROW_TILE), lambda i: (0, i))
    return pl.pallas_call(
        _a_proj_kernel,
        grid=(m // ROW_TILE,),
        in_specs=[row(d), _resident((1, d)), _resident(w_main.shape),
                  _resident((d, ng)), _resident((d, ng)), _resident((ng, d)), _resident((ng, d)),
                  _resident((1, ng)), _resident((1, ng)), _resident((ng, 1)), _resident((ng, 1))],
        out_specs=[row(A_QK), row(A_V), row(A_V), row(ng), row(ng), col, col],
        out_shape=[jax.ShapeDtypeStruct((m, A_QK), F32),
                   jax.ShapeDtypeStruct((m, A_V), BF16),
                   jax.ShapeDtypeStruct((m, A_V), F32),
                   jax.ShapeDtypeStruct((m, ng), F32),
                   jax.ShapeDtypeStruct((m, ng), F32),
                   jax.ShapeDtypeStruct((ng, m), F32),
                   jax.ShapeDtypeStruct((ng, m), F32)],
        compiler_params=_cparams("parallel"),
        name="mlstm_proj",
    )(x, g, w_main, w_li, w_lf, w_li.T, w_lf.T, b_li, b_lf, b_li.T, b_lf.T)


def _conv_kernel(x_ref, prev_ref, next_ref, w_ref, b_ref, o_ref, *, n_steps):
    i = pl.program_id(1)
    x = x_ref[...]
    rows = x.shape[0]
    prev = prev_ref[...] * (i > 0).astype(F32)
    nxt = next_ref[...] * (i < n_steps - 1).astype(F32)
    ridx = lax.broadcasted_iota(jnp.int32, x.shape, 0)
    half = A_CONV // 2
    acc = x * w_ref[half:half + 1, :] + b_ref[...]
    for s in range(1, half + 1):
        back = pltpu.roll(x, s, axis=0)
        fwd = pltpu.roll(x, rows - s, axis=0)
        for e in range(s):
            back = jnp.where(ridx == e, prev[8 - s + e:8 - s + e + 1, :], back)
            fwd = jnp.where(ridx == rows - s + e, nxt[e:e + 1, :], fwd)
        acc = acc + back * w_ref[half - s:half - s + 1, :] + fwd * w_ref[half + s:half + s + 1, :]
    y = acc * jax.nn.sigmoid(acc)
    lane = lax.broadcasted_iota(jnp.int32, x.shape, 1)
    y = jnp.where(lane < A_QK // 2, y * (A_DQK ** -0.5), y)
    o_ref[...] = y.astype(BF16)


def _conv(qk, w, b, bsz, seq):
    c = qk.shape[-1]
    qk = qk.reshape(bsz, seq, c)
    n_steps = seq // ROW_TILE
    r8 = ROW_TILE // 8
    return pl.pallas_call(
        functools.partial(_conv_kernel, n_steps=n_steps),
        grid=(bsz, n_steps),
        in_specs=[pl.BlockSpec((None, ROW_TILE, c), lambda b_, i: (b_, i, 0)),
                  pl.BlockSpec((None, 8, c), lambda b_, i: (b_, jnp.maximum(i * r8 - 1, 0), 0)),
                  pl.BlockSpec((None, 8, c), lambda b_, i: (b_, jnp.minimum((i + 1) * r8, seq // 8 - 1), 0)),
                  _resident((A_CONV, c)), _resident((1, c))],
        out_specs=pl.BlockSpec((None, ROW_TILE, c), lambda b_, i: (b_, i, 0)),
        out_shape=jax.ShapeDtypeStruct((bsz, seq, c), BF16),
        compiler_params=_cparams("parallel", "parallel"),
        name="mlstm_conv",
    )(qk, qk, qk, w, b)


def _mlstm_kernel(q_ref, k_ref, v_ref, li_ref, lf_ref, liT_ref, lfT_ref, h_ref,
                  c_ref, n_ref, m_ref, *, reverse):
    L = A_CHUNK
    lo = 4 if reverse else 0

    @pl.when(pl.program_id(1) == 0)
    def _():
        c_ref[...] = jnp.zeros_like(c_ref)
        n_ref[...] = jnp.zeros_like(n_ref)
        m_ref[...] = jnp.zeros_like(m_ref)

    t_idx = lax.broadcasted_iota(jnp.int32, (L, L), 0)
    s_idx = lax.broadcasted_iota(jnp.int32, (L, L), 1)
    seen = (s_idx >= t_idx) if reverse else (s_idx <= t_idx)
    seen_bf = seen.astype(F32).astype(BF16)
    last = 0 if reverse else L - 1

    order = range(A_STEP_CHUNKS - 1, -1, -1) if reverse else range(A_STEP_CHUNKS)
    for j in order:
        r0 = j * L
        li = li_ref[r0:r0 + L, :]
        lf = lf_ref[r0:r0 + L, :]
        liT = liT_ref[:, r0:r0 + L]
        lfT = lfT_ref[:, r0:r0 + L]
        b_col = sum(_dot(seen_bf, p) for p in _split3(lf))
        b_row = sum(_dot_nt(p, seen_bf) for p in _split3(lfT))
        b_tot = b_col[last:last + 1, :]
        a_col = b_tot - b_col + li
        m_loc = jnp.max(a_col, axis=0, keepdims=True)
        w_col = jnp.exp(a_col - m_loc)
        m_prev = m_ref[...]
        g_col = b_col + m_prev
        r_row = liT - b_row
        m_new = jnp.maximum(b_tot + m_prev, m_loc)
        s_prev = jnp.exp(b_tot + m_prev - m_new)
        s_loc = jnp.exp(m_loc - m_new)
        m_ref[...] = m_new

        for h in range(A_HEADS):
            gi = lo + h
            q = q_ref[r0:r0 + L, h * A_DQK:(h + 1) * A_DQK]
            k = k_ref[r0:r0 + L, h * A_DQK:(h + 1) * A_DQK]
            v = v_ref[r0:r0 + L, h * A_DV:(h + 1) * A_DV]
            c_prev = c_ref[h]
            n_prev = n_ref[h]

            d = b_col[:, gi:gi + 1] + r_row[gi:gi + 1, :]
            d = jnp.where(seen, d, -jnp.inf)
            g = g_col[:, gi:gi + 1]
            m_t = jnp.maximum(g, jnp.max(d, axis=-1, keepdims=True))
            p = jnp.exp(d - m_t) * _dot_nt(q, k)
            s_inter = jnp.exp(g - m_t)
            num = s_inter * _dot(q, c_prev.astype(BF16)) + _dot(p.astype(BF16), v)
            qn = jnp.sum(q.astype(F32) * n_prev, axis=-1, keepdims=True)
            den = s_inter * qn + jnp.sum(p, axis=-1, keepdims=True)
            out = num / jnp.maximum(jnp.abs(den), jnp.exp(-m_t))
            h_ref[r0:r0 + L, h * A_DV:(h + 1) * A_DV] = out

            kw = k.astype(F32) * w_col[:, gi:gi + 1]
            c_loc = _dot_tn(kw.astype(BF16), v)
            n_loc = jnp.sum(kw, axis=0, keepdims=True)
            sp = s_prev[:, gi:gi + 1]
            sl = s_loc[:, gi:gi + 1]
            c_ref[h] = sp * c_prev + sl * c_loc
            n_ref[h] = sp * n_prev + sl * n_loc


def _mlstm(qk, v, li, lf, liT, lfT, bsz, seq, reverse):
    tl = A_STEP_CHUNKS * A_CHUNK
    ns = seq // tl
    ng = li.shape[-1]
    blk = (lambda i: ns - 1 - i) if reverse else (lambda i: i)
    v = v.reshape(bsz, seq, A_V)
    li = li.reshape(bsz, seq, ng)
    lf = lf.reshape(bsz, seq, ng)
    gate = pl.BlockSpec((None, tl, ng), lambda b_, i: (b_, blk(i), 0))
    gate_t = pl.BlockSpec((ng, tl), lambda b_, i: (0, b_ * ns + blk(i)))
    return pl.pallas_call(
        functools.partial(_mlstm_kernel, reverse=reverse),
        grid=(bsz, ns),
        in_specs=[pl.BlockSpec((None, tl, A_QK // 2), lambda b_, i: (b_, blk(i), 0)),
                  pl.BlockSpec((None, tl, A_QK // 2), lambda b_, i: (b_, blk(i), 1)),
                  pl.BlockSpec((None, tl, A_V), lambda b_, i: (b_, blk(i), 0)),
                  gate, gate, gate_t, gate_t],
        out_specs=pl.BlockSpec((None, tl, A_V), lambda b_, i: (b_, blk(i), 0)),
        out_shape=jax.ShapeDtypeStruct((bsz, seq, A_V), F32),
        scratch_shapes=[pltpu.VMEM((A_HEADS, A_DQK, A_DV), F32),
                        pltpu.VMEM((A_HEADS, 1, A_DQK), F32),
                        pltpu.VMEM((1, ng), F32)],
        compiler_params=_cparams("parallel", "arbitrary"),
        name="mlstm_bwd" if reverse else "mlstm_fwd",
    )(qk, qk, v, li, lf, liT, lfT)


def _a_out_kernel(x_ref, hf_ref, hb_ref, o_ref, hg_ref, w_ref, y_ref):
    y = hf_ref[...] + hb_ref[...]
    parts = []
    for h in range(A_HEADS):
        yh = y[:, h * A_DV:(h + 1) * A_DV]
        ms = jnp.mean(yh * yh, axis=-1, keepdims=True)
        parts.append(yh * lax.rsqrt(ms + RMS_EPS))
    yn = jnp.concatenate(parts, axis=-1)
    z = (yn * hg_ref[...] * jax.nn.sigmoid(o_ref[...])).astype(BF16)
    y_ref[...] = x_ref[...] + _dot(z, w_ref[...])


def _a_out(x, hf, hb, o, hg, w):
    m, d = x.shape
    row = pl.BlockSpec((ROW_TILE, d), lambda i: (i, 0))
    return pl.pallas_call(
        _a_out_kernel,
        grid=(m // ROW_TILE,),
        in_specs=[row, row, row, row, _resident((1, d)), _resident((d, d))],
        out_specs=row,
        out_shape=jax.ShapeDtypeStruct((m, d), F32),
        compiler_params=_cparams("parallel"),
        name="mlstm_out",
    )(x, hf, hb, o, hg, w)


def _b_proj_kernel(x_ref, g_ref, w_ref, *rest):
    o_refs, ys_ref = rest[:-1], rest[-1]
    xn = _rms(x_ref[...], g_ref[...]).astype(BF16)
    for j, o_ref in enumerate(o_refs):
        dil = B_GROUPS[j // 3][1]
        y = _dot(xn, w_ref[:, j * B_WIDTH:(j + 1) * B_WIDTH])
        if j % 3 == 0:
            y = y * (B_DH ** -0.5)
        if dil == 1:
            o_ref[...] = y.astype(BF16)
        else:
            for c in range(B_WIDTH // 128):
                ys_ref[c] = y[:, c * 128:(c + 1) * 128]
            for r in range(dil):
                rows = [ys_ref[c, pl.ds(r, ROW_TILE // dil, stride=dil), :]
                        for c in range(B_WIDTH // 128)]
                o_ref[:, r * B_WIDTH:(r + 1) * B_WIDTH] = jnp.concatenate(rows, axis=-1).astype(BF16)


def _b_proj(x, g, w):
    m, d = x.shape
    dils = [dil for _, dil in B_GROUPS for _ in range(3)]
    return pl.pallas_call(
        _b_proj_kernel,
        grid=(m // ROW_TILE,),
        in_specs=[pl.BlockSpec((ROW_TILE, d), lambda i: (i, 0)), _resident((1, d)),
                  _resident(w.shape)],
        out_specs=[pl.BlockSpec((ROW_TILE // dil, dil * B_WIDTH), lambda i: (i, 0)) for dil in dils],
        out_shape=[jax.ShapeDtypeStruct((m // dil, dil * B_WIDTH), BF16) for dil in dils],
        scratch_shapes=[pltpu.VMEM((B_WIDTH // 128, ROW_TILE, 128), F32)],
        compiler_params=_cparams("parallel"),
        name="dilated_proj",
    )(x, g, w)


def _b_attn_kernel(q_ref, kp_ref, kc_ref, kn_ref, vp_ref, vc_ref, vn_ref, bias_ref,
                   o_ref, lse_ref):
    k_all = jnp.concatenate([kp_ref[...], kc_ref[...], kn_ref[...]], axis=0)
    v_all = jnp.concatenate([vp_ref[...], vc_ref[...], vn_ref[...]], axis=0)
    lane = lax.broadcasted_iota(jnp.int32, (B_TQ, 128), 1)
    low = lane < B_DH
    lse_row = jnp.zeros((B_TQ, 128), F32)
    for hp in range(B_HEADS // 2):
        cols = slice(hp * 128, (hp + 1) * 128)
        q2 = q_ref[:, cols]
        k2 = k_all[:, cols]
        v2 = v_all[:, cols]
        zero = jnp.zeros_like(q2)
        outs = []
        for par, qsel in ((0, jnp.where(low, q2, zero)), (1, jnp.where(low, zero, q2))):
            hd = 2 * hp + par
            s = _dot_nt(qsel, k2) + bias_ref[hd]
            mx = jnp.max(s, axis=-1, keepdims=True)
            e = jnp.exp(s - mx)
            den = jnp.sum(e, axis=-1, keepdims=True)
            outs.append(_dot(e.astype(BF16), v2) / den)
            lse = mx + jnp.log(den)
            lse_row = jnp.where(lane // LSE_LANES == hd, lse, lse_row)
        o_ref[:, cols] = jnp.where(low, outs[0], outs[1]).astype(BF16)
    lse_ref[...] = lse_row


def _b_attn(q, k, v, bias, bsz, seq, dil):
    n = seq // dil
    nq = n // B_TQ
    w = B_WIDTH
    q = q.reshape(bsz, n, dil * w)
    k = k.reshape(bsz, n, dil * w)
    v = v.reshape(bsz, n, dil * w)
    hb = B_TQ // B_HALF
    last_half = n // B_HALF - 1
    cur = pl.BlockSpec((None, B_TQ, w), lambda b_, r, i: (b_, i, r))
    prev = pl.BlockSpec((None, B_HALF, w), lambda b_, r, i: (b_, jnp.maximum(i * hb - 1, 0), r))
    nxt = pl.BlockSpec((None, B_HALF, w), lambda b_, r, i: (b_, jnp.minimum((i + 1) * hb, last_half), r))
    variant = lambda i: jnp.where(i == 0, 1, 0) + jnp.where(i == nq - 1, 2, 0)
    bias_spec = pl.BlockSpec((None, B_HEADS, B_TQ, B_TK), lambda b_, r, i: (variant(i), 0, 0, 0))
    out, lse = pl.pallas_call(
        _b_attn_kernel,
        grid=(bsz, dil, nq),
        in_specs=[cur, prev, cur, nxt, prev, cur, nxt, bias_spec],
        out_specs=[cur, pl.BlockSpec((None, B_TQ, 128), lambda b_, r, i: (b_, i, r))],
        out_shape=[jax.ShapeDtypeStruct((bsz, n, dil * w), BF16),
                   jax.ShapeDtypeStruct((bsz, n, dil * 128), F32)],
        compiler_params=_cparams("parallel", "parallel", "arbitrary"),
        name=f"dilated_attn_d{dil}",
    )(q, k, k, k, v, v, v, bias)
    return out.reshape(bsz * n, dil * w), lse.reshape(bsz * n, dil * 128)


def _attn_bias(dil):
    slopes = jnp.exp2(-8.0 * (jnp.arange(B_HEADS, dtype=F32) + 1.0) / B_HEADS)
    tq = jnp.arange(B_TQ)[:, None]
    col = jnp.arange(B_TK)[None, :]
    rel = jnp.abs(col - B_HALF - tq)
    base = -slopes[:, None, None] * (rel * dil).astype(F32)[None]
    band = (rel <= B_HALF)[None]
    first = (col >= B_HALF)[None]
    lastv = (col < B_TQ + B_HALF)[None]
    variants = []
    for need_first, need_last in ((False, False), (True, False), (False, True), (True, True)):
        ok = band
        if need_first:
            ok = ok & first
        if need_last:
            ok = ok & lastv
        variants.append(jnp.where(ok, base, NEG_BIG))
    return jnp.stack(variants, axis=0)


def _b_out_kernel(x_ref, o0_ref, o1_ref, o2_ref, l0_ref, l1_ref, l2_ref, e_ref, w_ref, y_ref,
                  os_ref, ls_ref):
    for gi, (l_ref, (_, dil)) in enumerate(zip((l0_ref, l1_ref, l2_ref), B_GROUPS)):
        for r in range(dil):
            ls_ref[gi, pl.ds(r, ROW_TILE // dil, stride=dil), :] = l_ref[:, r * 128:(r + 1) * 128]
    ls = [ls_ref[0], ls_ref[1], ls_ref[2]]
    mx = jnp.maximum(jnp.maximum(ls[0], ls[1]), ls[2])
    es = [jnp.exp(l - mx) for l in ls]
    den = es[0] + es[1] + es[2]
    acc = None
    for gi, (e, o_ref, (_, dil)) in enumerate(zip(es, (o0_ref, o1_ref, o2_ref), B_GROUPS)):
        alpha = e / den
        wide = sum(_dot(p, e_ref[...]) for p in _split3(alpha))
        nblk = B_WIDTH // 128
        for r in range(dil):
            for c in range(nblk):
                lo = r * B_WIDTH + c * 128
                os_ref[gi, c, pl.ds(r, ROW_TILE // dil, stride=dil), :] = (
                    o_ref[:, lo:lo + 128].astype(F32))
        term = wide * jnp.concatenate([os_ref[gi, c] for c in range(nblk)], axis=-1)
        acc = term if acc is None else acc + term
    y_ref[...] = x_ref[...] + _dot(acc.astype(BF16), w_ref[...])


def _b_out(x, outs, lses, w):
    m, d = x.shape
    head_of_col = jnp.arange(B_WIDTH) // B_DH
    expand = (jnp.arange(128)[:, None] == head_of_col[None, :] * LSE_LANES).astype(BF16)
    row = pl.BlockSpec((ROW_TILE, d), lambda i: (i, 0))
    grp = lambda lanes: [pl.BlockSpec((ROW_TILE // dil, dil * lanes), lambda i: (i, 0))
                         for _, dil in B_GROUPS]
    ng = len(B_GROUPS)
    return pl.pallas_call(
        _b_out_kernel,
        grid=(m // ROW_TILE,),
        in_specs=[row, *grp(B_WIDTH), *grp(128), _resident((128, B_WIDTH)),
                  _resident((B_WIDTH, d))],
        out_specs=row,
        out_shape=jax.ShapeDtypeStruct((m, d), F32),
        scratch_shapes=[pltpu.VMEM((ng, B_WIDTH // 128, ROW_TILE, 128), F32),
                        pltpu.VMEM((ng, ROW_TILE, 128), F32)],
        compiler_params=_cparams("parallel"),
        name="dilated_out",
    )(x, *outs, *lses, expand, w)


def _mlstm_mixer(x, g, w_in, conv_w, conv_b, gate_b, head_g, w_out, bsz, seq):
    w_main = w_in[:, :A_QK + 2 * A_V].astype(BF16)
    wg = w_in[:, A_QK + 2 * A_V:]
    H = A_HEADS
    pick = lambda a, s: jnp.concatenate([a[..., s * H:(s + 1) * H], a[..., (s + 2) * H:(s + 3) * H]], axis=-1)
    w_li, w_lf = pick(wg, 0).astype(BF16), pick(wg, 1).astype(BF16)
    gb = gate_b.astype(F32)[None, :]
    b_li, b_lf = pick(gb, 0), pick(gb, 1)
    qk, v, o, li, lf, liT, lfT = _a_proj(x, g, w_main, w_li, w_lf, b_li, b_lf)
    qk = _conv(qk, conv_w.astype(F32), conv_b.astype(F32)[None, :], bsz, seq)
    hf = _mlstm(qk, v, li, lf, liT, lfT, bsz, seq, reverse=False)
    hb = _mlstm(qk, v, li, lf, liT, lfT, bsz, seq, reverse=True)
    m = bsz * seq
    return _a_out(x, hf.reshape(m, A_V), hb.reshape(m, A_V), o,
                  head_g.astype(F32)[None, :], w_out.astype(BF16))


def _dilated_mixer(x, g, w_in, w_out, bsz, seq):
    slabs = _b_proj(x, g, w_in.astype(BF16))
    outs, lses = [], []
    for gi, (_, dil) in enumerate(B_GROUPS):
        o, l = _b_attn(slabs[3 * gi], slabs[3 * gi + 1], slabs[3 * gi + 2], _attn_bias(dil),
                       bsz, seq, dil)
        outs.append(o)
        lses.append(l)
    return _b_out(x, outs, lses, w_out.astype(BF16))


def kernel(x, norm_ffn1, ffn1_gate, ffn1_up, ffn1_down, norm_mix, a_w_in, a_conv_w, a_conv_b,
           a_gate_b, a_head_g, a_w_out, b_w_in, b_w_out, norm_ffn2, ffn2_gate, ffn2_up,
           ffn2_down, norm_final):
    bsz, seq, d = x.shape
    h = x.reshape(bsz * seq, d)
    row = lambda a: a.astype(F32)[None, :]
    g_final = row(norm_final)
    for i in range(DEPTH):
        h = _ffn(h, row(norm_ffn1[i]), ffn1_gate[i].astype(BF16), ffn1_up[i].astype(BF16),
                 ffn1_down[i].astype(BF16), g_final, False)
        j = i // 2
        if i % 2 == 0:
            h = _mlstm_mixer(h, row(norm_mix[i]), a_w_in[j], a_conv_w[j], a_conv_b[j],
                             a_gate_b[j], a_head_g[j], a_w_out[j], bsz, seq)
        else:
            h = _dilated_mixer(h, row(norm_mix[i]), b_w_in[j], b_w_out[j], bsz, seq)
        h = _ffn(h, row(norm_ffn2[i]), ffn2_gate[i].astype(BF16), ffn2_up[i].astype(BF16),
                 ffn2_down[i].astype(BF16), g_final, i == DEPTH - 1)
    return h.reshape(bsz, seq, d)
```

```python
import functools

import jax
import jax.numpy as jnp
from jax import lax
from jax.experimental import pallas as pl
from jax.experimental.pallas import tpu as pltpu

F32 = jnp.float32
BF16 = jnp.bfloat16

D_MODEL = 1024
DEPTH = 4
D_FF = 2816
RMS_EPS = 1e-6

A_HEADS = 4
A_DQK = 128
A_DV = 256
A_CHUNK = 128
A_CONV = 5
A_QK = 2 * A_HEADS * A_DQK
A_V = A_HEADS * A_DV

B_GROUPS = ((128, 1), (512, 4), (2048, 16))
B_HEADS = 16
B_DH = 64
B_WIDTH = B_HEADS * B_DH
B_HALF = 64
B_TQ = 128
B_TK = B_TQ + 2 * B_HALF
B_SUB = 4
LOG2E = 1.4426950408889634
STAT_LANES = 128 // B_HEADS
NEG_BIG = -1e30

ROW_TILE = 512
A_STEP_CHUNKS = 4
VMEM_LIMIT = 56 * 1024 * 1024


def _cparams(*sem):
    return pltpu.CompilerParams(dimension_semantics=sem, vmem_limit_bytes=VMEM_LIMIT)


def _resident(shape):
    nd = len(shape)
    return pl.BlockSpec(shape, lambda *_: (0,) * nd, pipeline_mode=pl.Buffered(1))


def _rms(x, g):
    ms = jnp.mean(x * x, axis=-1, keepdims=True)
    return x * lax.rsqrt(ms + RMS_EPS) * g


def _dot(a, b):
    return jnp.dot(a, b, preferred_element_type=F32)


def _dot_nt(a, b):
    return lax.dot_general(a, b, (((1,), (1,)), ((), ())), preferred_element_type=F32)


def _dot_tn(a, b):
    return lax.dot_general(a, b, (((0,), (0,)), ((), ())), preferred_element_type=F32)


def _split3(x):
    hi = x.astype(BF16)
    r1 = x - hi.astype(F32)
    mid = r1.astype(BF16)
    lo = (r1 - mid.astype(F32)).astype(BF16)
    return hi, mid, lo


def _ffn_kernel(x_ref, g_ref, wg_ref, wu_ref, wd_ref, gf_ref, o_ref, *, final_norm):
    x = x_ref[...]
    xn = _rms(x, g_ref[...]).astype(BF16)
    gate = _dot(xn, wg_ref[...])
    up = _dot(xn, wu_ref[...])
    act = (gate * jax.nn.sigmoid(gate) * up).astype(BF16)
    y = x + 0.5 * _dot(act, wd_ref[...])
    if final_norm:
        y = _rms(y, gf_ref[...])
    o_ref[...] = y


def _ffn(x, g, wg, wu, wd, g_final, final_norm):
    m, d = x.shape
    f = wg.shape[1]
    row = pl.BlockSpec((ROW_TILE, d), lambda i: (i, 0))
    return pl.pallas_call(
        functools.partial(_ffn_kernel, final_norm=final_norm),
        grid=(m // ROW_TILE,),
        in_specs=[row, _resident((1, d)), _resident((d, f)), _resident((d, f)),
                  _resident((f, d)), _resident((1, d))],
        out_specs=row,
        out_shape=jax.ShapeDtypeStruct((m, d), F32),
        compiler_params=_cparams("parallel"),
        name="ffn",
    )(x, g, wg, wu, wd, g_final)


def _a_proj_kernel(x_ref, g_ref, w_ref, wli_ref, wlf_ref, wliT_ref, wlfT_ref,
                   bli_ref, blf_ref, bliT_ref, blfT_ref,
                   qk_ref, v_ref, o_ref, li_ref, lf_ref, liT_ref, lfT_ref):
    xn = _rms(x_ref[...], g_ref[...]).astype(BF16)
    qk_ref[...] = _dot(xn, w_ref[:, 0:A_QK])
    v_ref[...] = _dot(xn, w_ref[:, A_QK:A_QK + A_V]).astype(BF16)
    o_ref[...] = _dot(xn, w_ref[:, A_QK + A_V:A_QK + 2 * A_V])
    li_ref[...] = _dot(xn, wli_ref[...]) + bli_ref[...]
    lf_ref[...] = jax.nn.log_sigmoid(_dot(xn, wlf_ref[...]) + blf_ref[...])
    liT_ref[...] = _dot_nt(wliT_ref[...], xn) + bliT_ref[...]
    lfT_ref[...] = jax.nn.log_sigmoid(_dot_nt(wlfT_ref[...], xn) + blfT_ref[...])


def _a_proj(x, g, w_main, w_li, w_lf, b_li, b_lf):
    m, d = x.shape
    ng = w_li.shape[1]
    row = lambda n: pl.BlockSpec((ROW_TILE, n), lambda i: (i, 0))
    col = pl.BlockSpec((ng, ROW_TILE), lambda i: (0, i))
    return pl.pallas_call(
        _a_proj_kernel,
        grid=(m // ROW_TILE,),
        in_specs=[row(d), _resident((1, d)), _resident(w_main.shape),
                  _resident((d, ng)), _resident((d, ng)), _resident((ng, d)), _resident((ng, d)),
                  _resident((1, ng)), _resident((1, ng)), _resident((ng, 1)), _resident((ng, 1))],
        out_specs=[row(A_QK), row(A_V), row(A_V), row(ng), row(ng), col, col],
        out_shape=[jax.ShapeDtypeStruct((m, A_QK), F32),
                   jax.ShapeDtypeStruct((m, A_V), BF16),
                   jax.ShapeDtypeStruct((m, A_V), F32),
                   jax.ShapeDtypeStruct((m, ng), F32),
                   jax.ShapeDtypeStruct((m, ng), F32),
                   jax.ShapeDtypeStruct((ng, m), F32),
                   jax.ShapeDtypeStruct((ng, m), F32)],
        compiler_params=_cparams("parallel"),
        name="mlstm_proj",
    )(x, g, w_main, w_li, w_lf, w_li.T, w_lf.T, b_li, b_lf, b_li.T, b_lf.T)


def _conv_kernel(x_ref, prev_ref, next_ref, w_ref, b_ref, o_ref, *, n_steps):
    i = pl.program_id(1)
    x = x_ref[...]
    rows = x.shape[0]
    prev = prev_ref[...] * (i > 0).astype(F32)
    nxt = next_ref[...] * (i < n_steps - 1).astype(F32)
    ridx = lax.broadcasted_iota(jnp.int32, x.shape, 0)
    half = A_CONV // 2
    acc = x * w_ref[half:half + 1, :] + b_ref[...]
    for s in range(1, half + 1):
        back = pltpu.roll(x, s, axis=0)
        fwd = pltpu.roll(x, rows - s, axis=0)
        for e in range(s):
            back = jnp.where(ridx == e, prev[8 - s + e:8 - s + e + 1, :], back)
            fwd = jnp.where(ridx == rows - s + e, nxt[e:e + 1, :], fwd)
        acc = acc + back * w_ref[half - s:half - s + 1, :] + fwd * w_ref[half + s:half + s + 1, :]
    y = acc * jax.nn.sigmoid(acc)
    lane = lax.broadcasted_iota(jnp.int32, x.shape, 1)
    y = jnp.where(lane < A_QK // 2, y * (A_DQK ** -0.5), y)
    o_ref[...] = y.astype(BF16)


def _conv(qk, w, b, bsz, seq):
    c = qk.shape[-1]
    qk = qk.reshape(bsz, seq, c)
    n_steps = seq // ROW_TILE
    r8 = ROW_TILE // 8
    return pl.pallas_call(
        functools.partial(_conv_kernel, n_steps=n_steps),
        grid=(bsz, n_steps),
        in_specs=[pl.BlockSpec((None, ROW_TILE, c), lambda b_, i: (b_, i, 0)),
                  pl.BlockSpec((None, 8, c), lambda b_, i: (b_, jnp.maximum(i * r8 - 1, 0), 0)),
                  pl.BlockSpec((None, 8, c), lambda b_, i: (b_, jnp.minimum((i + 1) * r8, seq // 8 - 1), 0)),
                  _resident((A_CONV, c)), _resident((1, c))],
        out_specs=pl.BlockSpec((None, ROW_TILE, c), lambda b_, i: (b_, i, 0)),
        out_shape=jax.ShapeDtypeStruct((bsz, seq, c), BF16),
        compiler_params=_cparams("parallel", "parallel"),
        name="mlstm_conv",
    )(qk, qk, qk, w, b)


def _mlstm_kernel(q_ref, k_ref, v_ref, li_ref, lf_ref, liT_ref, lfT_ref, h_ref,
                  c_ref, n_ref, m_ref, *, reverse):
    L = A_CHUNK
    lo = 4 if reverse else 0

    @pl.when(pl.program_id(1) == 0)
    def _():
        c_ref[...] = jnp.zeros_like(c_ref)
        n_ref[...] = jnp.zeros_like(n_ref)
        m_ref[...] = jnp.zeros_like(m_ref)

    t_idx = lax.broadcasted_iota(jnp.int32, (L, L), 0)
    s_idx = lax.broadcasted_iota(jnp.int32, (L, L), 1)
    seen = (s_idx >= t_idx) if reverse else (s_idx <= t_idx)
    seen_bf = seen.astype(F32).astype(BF16)
    last = 0 if reverse else L - 1

    order = range(A_STEP_CHUNKS - 1, -1, -1) if reverse else range(A_STEP_CHUNKS)
    for j in order:
        r0 = j * L
        li = li_ref[r0:r0 + L, :]
        lf = lf_ref[r0:r0 + L, :]
        liT = liT_ref[:, r0:r0 + L]
        lfT = lfT_ref[:, r0:r0 + L]
        b_col = sum(_dot(seen_bf, p) for p in _split3(lf))
        b_row = sum(_dot_nt(p, seen_bf) for p in _split3(lfT))
        b_tot = b_col[last:last + 1, :]
        a_col = b_tot - b_col + li
        m_loc = jnp.max(a_col, axis=0, keepdims=True)
        w_col = jnp.exp(a_col - m_loc)
        m_prev = m_ref[...]
        g_col = b_col + m_prev
        r_row = liT - b_row
        m_new = jnp.maximum(b_tot + m_prev, m_loc)
        s_prev = jnp.exp(b_tot + m_prev - m_new)
        s_loc = jnp.exp(m_loc - m_new)
        m_ref[...] = m_new

        for h in range(A_HEADS):
            gi = lo + h
            q = q_ref[r0:r0 + L, h * A_DQK:(h + 1) * A_DQK]
            k = k_ref[r0:r0 + L, h * A_DQK:(h + 1) * A_DQK]
            v = v_ref[r0:r0 + L, h * A_DV:(h + 1) * A_DV]
            c_prev = c_ref[h]
            n_prev = n_ref[h]

            d = b_col[:, gi:gi + 1] + r_row[gi:gi + 1, :]
            d = jnp.where(seen, d, -jnp.inf)
            g = g_col[:, gi:gi + 1]
            m_t = jnp.maximum(g, jnp.max(d, axis=-1, keepdims=True))
            p = jnp.exp(d - m_t) * _dot_nt(q, k)
            s_inter = jnp.exp(g - m_t)
            num = s_inter * _dot(q, c_prev.astype(BF16)) + _dot(p.astype(BF16), v)
            qn = jnp.sum(q.astype(F32) * n_prev, axis=-1, keepdims=True)
            den = s_inter * qn + jnp.sum(p, axis=-1, keepdims=True)
            out = num / jnp.maximum(jnp.abs(den), jnp.exp(-m_t))
            h_ref[r0:r0 + L, h * A_DV:(h + 1) * A_DV] = out

            kw = k.astype(F32) * w_col[:, gi:gi + 1]
            c_loc = _dot_tn(kw.astype(BF16), v)
            n_loc = jnp.sum(kw, axis=0, keepdims=True)
            sp = s_prev[:, gi:gi + 1]
            sl = s_loc[:, gi:gi + 1]
            c_ref[h] = sp * c_prev + sl * c_loc
            n_ref[h] = sp * n_prev + sl * n_loc


def _mlstm(qk, v, li, lf, liT, lfT, bsz, seq, reverse):
    tl = A_STEP_CHUNKS * A_CHUNK
    ns = seq // tl
    ng = li.shape[-1]
    blk = (lambda i: ns - 1 - i) if reverse else (lambda i: i)
    v = v.reshape(bsz, seq, A_V)
    li = li.reshape(bsz, seq, ng)
    lf = lf.reshape(bsz, seq, ng)
    gate = pl.BlockSpec((None, tl, ng), lambda b_, i: (b_, blk(i), 0))
    gate_t = pl.BlockSpec((ng, tl), lambda b_, i: (0, b_ * ns + blk(i)))
    return pl.pallas_call(
        functools.partial(_mlstm_kernel, reverse=reverse),
        grid=(bsz, ns),
        in_specs=[pl.BlockSpec((None, tl, A_QK // 2), lambda b_, i: (b_, blk(i), 0)),
                  pl.BlockSpec((None, tl, A_QK // 2), lambda b_, i: (b_, blk(i), 1)),
                  pl.BlockSpec((None, tl, A_V), lambda b_, i: (b_, blk(i), 0)),
                  gate, gate, gate_t, gate_t],
        out_specs=pl.BlockSpec((None, tl, A_V), lambda b_, i: (b_, blk(i), 0)),
        out_shape=jax.ShapeDtypeStruct((bsz, seq, A_V), F32),
        scratch_shapes=[pltpu.VMEM((A_HEADS, A_DQK, A_DV), F32),
                        pltpu.VMEM((A_HEADS, 1, A_DQK), F32),
                        pltpu.VMEM((1, ng), F32)],
        compiler_params=_cparams("parallel", "arbitrary"),
        name="mlstm_bwd" if reverse else "mlstm_fwd",
    )(qk, qk, v, li, lf, liT, lfT)


def _a_out_kernel(x_ref, hf_ref, hb_ref, o_ref, hg_ref, w_ref, y_ref):
    y = hf_ref[...] + hb_ref[...]
    parts = []
    for h in range(A_HEADS):
        yh = y[:, h * A_DV:(h + 1) * A_DV]
        ms = jnp.mean(yh * yh, axis=-1, keepdims=True)
        parts.append(yh * lax.rsqrt(ms + RMS_EPS))
    yn = jnp.concatenate(parts, axis=-1)
    z = (yn * hg_ref[...] * jax.nn.sigmoid(o_ref[...])).astype(BF16)
    y_ref[...] = x_ref[...] + _dot(z, w_ref[...])


def _a_out(x, hf, hb, o, hg, w):
    m, d = x.shape
    row = pl.BlockSpec((ROW_TILE, d), lambda i: (i, 0))
    return pl.pallas_call(
        _a_out_kernel,
        grid=(m // ROW_TILE,),
        in_specs=[row, row, row, row, _resident((1, d)), _resident((d, d))],
        out_specs=row,
        out_shape=jax.ShapeDtypeStruct((m, d), F32),
        compiler_params=_cparams("parallel"),
        name="mlstm_out",
    )(x, hf, hb, o, hg, w)


def _b_proj_kernel(x_ref, g_ref, w_ref, *rest):
    o_refs, ys_ref = rest[:-1], rest[-1]
    xn = _rms(x_ref[...], g_ref[...]).astype(BF16)
    for j, o_ref in enumerate(o_refs):
        dil = B_GROUPS[j // 3][1]
        y = _dot(xn, w_ref[:, j * B_WIDTH:(j + 1) * B_WIDTH])
        if j % 3 == 0:
            y = y * (B_DH ** -0.5 * LOG2E)
        if dil == 1:
            o_ref[...] = y.astype(BF16)
        else:
            for c in range(B_WIDTH // 128):
                ys_ref[c] = y[:, c * 128:(c + 1) * 128]
            for r in range(dil):
                rows = [ys_ref[c, pl.ds(r, ROW_TILE // dil, stride=dil), :]
                        for c in range(B_WIDTH // 128)]
                o_ref[:, r * B_WIDTH:(r + 1) * B_WIDTH] = jnp.concatenate(rows, axis=-1).astype(BF16)


def _b_proj(x, g, w):
    m, d = x.shape
    dils = [dil for _, dil in B_GROUPS for _ in range(3)]
    return pl.pallas_call(
        _b_proj_kernel,
        grid=(m // ROW_TILE,),
        in_specs=[pl.BlockSpec((ROW_TILE, d), lambda i: (i, 0)), _resident((1, d)),
                  _resident(w.shape)],
        out_specs=[pl.BlockSpec((ROW_TILE // dil, dil * B_WIDTH), lambda i: (i, 0)) for dil in dils],
        out_shape=[jax.ShapeDtypeStruct((m // dil, dil * B_WIDTH), BF16) for dil in dils],
        scratch_shapes=[pltpu.VMEM((B_WIDTH // 128, ROW_TILE, 128), F32)],
        compiler_params=_cparams("parallel"),
        name="dilated_proj",
    )(x, g, w)


def _b_attn_kernel(q_ref, kp_ref, kc_ref, kn_ref, vp_ref, vc_ref, vn_ref,
                   bias_first_ref, bias_mid_ref, bias_last_ref, o_ref, stat_ref):
    k_all = jnp.concatenate([kp_ref[...], kc_ref[...], kn_ref[...]], axis=0)
    v_all = jnp.concatenate([vp_ref[...], vc_ref[...], vn_ref[...]], axis=0)
    lane = lax.broadcasted_iota(jnp.int32, (B_TQ, 128), 1)
    low = lane < B_DH
    stat_slot = lane // (STAT_LANES // 2)
    for sb in range(B_SUB):
        bias_ref = bias_first_ref if sb == 0 else (bias_last_ref if sb == B_SUB - 1 else bias_mid_ref)
        q0 = sb * B_TQ
        stat_row = jnp.zeros((B_TQ, 128), F32)
        for hp in range(B_HEADS // 2):
            cols = slice(hp * 128, (hp + 1) * 128)
            q2 = q_ref[q0:q0 + B_TQ, cols]
            k2 = k_all[q0:q0 + B_TK, cols]
            v2 = v_all[q0:q0 + B_TK, cols]
            zero = jnp.zeros_like(q2)
            outs = []
            for par, qsel in ((0, jnp.where(low, q2, zero)), (1, jnp.where(low, zero, q2))):
                hd = 2 * hp + par
                s = _dot_nt(qsel, k2) + bias_ref[hd]
                mx = jnp.max(s, axis=-1, keepdims=True)
                e = jnp.exp2(s - mx)
                den = jnp.sum(e, axis=-1, keepdims=True)
                outs.append(_dot(e.astype(BF16), v2))
                stat_row = jnp.where(stat_slot == 2 * hd, mx,
                                     jnp.where(stat_slot == 2 * hd + 1, den, stat_row))
            o_ref[q0:q0 + B_TQ, cols] = jnp.where(low, outs[0], outs[1]).astype(BF16)
        stat_ref[q0:q0 + B_TQ, :] = stat_row


def _b_attn(q, k, v, bias, bsz, seq, dil):
    n = seq // dil
    rows = B_SUB * B_TQ
    nq = n // rows
    w = B_WIDTH
    q = q.reshape(bsz, n, dil * w)
    k = k.reshape(bsz, n, dil * w)
    v = v.reshape(bsz, n, dil * w)
    hb = rows // B_HALF
    last_half = n // B_HALF - 1
    cur = pl.BlockSpec((None, rows, w), lambda b_, r, i: (b_, i, r))
    prev = pl.BlockSpec((None, B_HALF, w), lambda b_, r, i: (b_, jnp.maximum(i * hb - 1, 0), r))
    nxt = pl.BlockSpec((None, B_HALF, w), lambda b_, r, i: (b_, jnp.minimum((i + 1) * hb, last_half), r))
    at_start = lambda i: jnp.where(i == 0, 1, 0)
    at_end = lambda i: jnp.where(i == nq - 1, 2, 0)
    first_variant = (lambda i: at_start(i) + at_end(i)) if B_SUB == 1 else at_start
    bias_blk = (None, B_HEADS, B_TQ, B_TK)
    bias_first = pl.BlockSpec(bias_blk, lambda b_, r, i: (first_variant(i), 0, 0, 0))
    bias_mid = pl.BlockSpec(bias_blk, lambda b_, r, i: (0, 0, 0, 0), pipeline_mode=pl.Buffered(1))
    bias_last = pl.BlockSpec(bias_blk, lambda b_, r, i: (at_end(i), 0, 0, 0))
    out, stats = pl.pallas_call(
        _b_attn_kernel,
        grid=(bsz, dil, nq),
        in_specs=[cur, prev, cur, nxt, prev, cur, nxt, bias_first, bias_mid, bias_last],
        out_specs=[cur, pl.BlockSpec((None, rows, 128), lambda b_, r, i: (b_, i, r))],
        out_shape=[jax.ShapeDtypeStruct((bsz, n, dil * w), BF16),
                   jax.ShapeDtypeStruct((bsz, n, dil * 128), F32)],
        compiler_params=_cparams("parallel", "parallel", "arbitrary"),
        name=f"dilated_attn_d{dil}",
    )(q, k, k, k, v, v, v, bias, bias, bias)
    return out.reshape(bsz * n, dil * w), stats.reshape(bsz * n, dil * 128)


def _attn_bias(dil):
    slopes = jnp.exp2(-8.0 * (jnp.arange(B_HEADS, dtype=F32) + 1.0) / B_HEADS)
    tq = jnp.arange(B_TQ)[:, None]
    col = jnp.arange(B_TK)[None, :]
    rel = jnp.abs(col - B_HALF - tq)
    base = -slopes[:, None, None] * (rel * dil).astype(F32)[None] * LOG2E
    band = (rel <= B_HALF)[None]
    first = (col >= B_HALF)[None]
    lastv = (col < B_TQ + B_HALF)[None]
    variants = []
    for need_first, need_last in ((False, False), (True, False), (False, True), (True, True)):
        ok = band
        if need_first:
            ok = ok & first
        if need_last:
            ok = ok & lastv
        variants.append(jnp.where(ok, base, NEG_BIG))
    return jnp.stack(variants, axis=0)


def _b_out_kernel(x_ref, o0_ref, o1_ref, o2_ref, s0_ref, s1_ref, s2_ref, e_ref, w_ref, y_ref,
                  os_ref, ss_ref):
    for gi, (s_ref, (_, dil)) in enumerate(zip((s0_ref, s1_ref, s2_ref), B_GROUPS)):
        for r in range(dil):
            ss_ref[gi, pl.ds(r, ROW_TILE // dil, stride=dil), :] = s_ref[:, r * 128:(r + 1) * 128]
    half = STAT_LANES // 2
    mxs = [ss_ref[gi] for gi in range(len(B_GROUPS))]
    dens = [pltpu.roll(m, 128 - half, axis=1) for m in mxs]
    top = jnp.maximum(jnp.maximum(mxs[0], mxs[1]), mxs[2])
    ts = [jnp.exp2(m - top) for m in mxs]
    total = dens[0] * ts[0] + dens[1] * ts[1] + dens[2] * ts[2]
    lane = lax.broadcasted_iota(jnp.int32, total.shape, 1)
    picked = lane % STAT_LANES == 0
    acc = None
    for gi, (t, o_ref, (_, dil)) in enumerate(zip(ts, (o0_ref, o1_ref, o2_ref), B_GROUPS)):
        weight = jnp.where(picked, t / total, 0.0)
        wide = sum(_dot(p, e_ref[...]) for p in _split3(weight))
        nblk = B_WIDTH // 128
        for r in range(dil):
            for c in range(nblk):
                lo = r * B_WIDTH + c * 128
                os_ref[gi, c, pl.ds(r, ROW_TILE // dil, stride=dil), :] = (
                    o_ref[:, lo:lo + 128].astype(F32))
        term = wide * jnp.concatenate([os_ref[gi, c] for c in range(nblk)], axis=-1)
        acc = term if acc is None else acc + term
    y_ref[...] = x_ref[...] + _dot(acc.astype(BF16), w_ref[...])


def _b_out(x, outs, stats, w):
    m, d = x.shape
    head_of_col = jnp.arange(B_WIDTH) // B_DH
    expand = (jnp.arange(128)[:, None] == head_of_col[None, :] * STAT_LANES).astype(BF16)
    row = pl.BlockSpec((ROW_TILE, d), lambda i: (i, 0))
    grp = lambda lanes: [pl.BlockSpec((ROW_TILE // dil, dil * lanes), lambda i: (i, 0))
                         for _, dil in B_GROUPS]
    ng = len(B_GROUPS)
    return pl.pallas_call(
        _b_out_kernel,
        grid=(m // ROW_TILE,),
        in_specs=[row, *grp(B_WIDTH), *grp(128), _resident((128, B_WIDTH)),
                  _resident((B_WIDTH, d))],
        out_specs=row,
        out_shape=jax.ShapeDtypeStruct((m, d), F32),
        scratch_shapes=[pltpu.VMEM((ng, B_WIDTH // 128, ROW_TILE, 128), F32),
                        pltpu.VMEM((ng, ROW_TILE, 128), F32)],
        compiler_params=_cparams("parallel"),
        name="dilated_out",
    )(x, *outs, *stats, expand, w)


def _mlstm_mixer(x, g, w_in, conv_w, conv_b, gate_b, head_g, w_out, bsz, seq):
    w_main = w_in[:, :A_QK + 2 * A_V].astype(BF16)
    wg = w_in[:, A_QK + 2 * A_V:]
    H = A_HEADS
    pick = lambda a, s: jnp.concatenate([a[..., s * H:(s + 1) * H], a[..., (s + 2) * H:(s + 3) * H]], axis=-1)
    w_li, w_lf = pick(wg, 0).astype(BF16), pick(wg, 1).astype(BF16)
    gb = gate_b.astype(F32)[None, :]
    b_li, b_lf = pick(gb, 0), pick(gb, 1)
    qk, v, o, li, lf, liT, lfT = _a_proj(x, g, w_main, w_li, w_lf, b_li, b_lf)
    qk = _conv(qk, conv_w.astype(F32), conv_b.astype(F32)[None, :], bsz, seq)
    hf = _mlstm(qk, v, li, lf, liT, lfT, bsz, seq, reverse=False)
    hb = _mlstm(qk, v, li, lf, liT, lfT, bsz, seq, reverse=True)
    m = bsz * seq
    return _a_out(x, hf.reshape(m, A_V), hb.reshape(m, A_V), o,
                  head_g.astype(F32)[None, :], w_out.astype(BF16))


def _dilated_mixer(x, g, w_in, w_out, bsz, seq):
    slabs = _b_proj(x, g, w_in.astype(BF16))
    outs, stats = [], []
    for gi, (_, dil) in enumerate(B_GROUPS):
        o, l = _b_attn(slabs[3 * gi], slabs[3 * gi + 1], slabs[3 * gi + 2], _attn_bias(dil),
                       bsz, seq, dil)
        outs.append(o)
        stats.append(l)
    return _b_out(x, outs, stats, w_out.astype(BF16))


def kernel(x, norm_ffn1, ffn1_gate, ffn1_up, ffn1_down, norm_mix, a_w_in, a_conv_w, a_conv_b,
           a_gate_b, a_head_g, a_w_out, b_w_in, b_w_out, norm_ffn2, ffn2_gate, ffn2_up,
           ffn2_down, norm_final):
    bsz, seq, d = x.shape
    h = x.reshape(bsz * seq, d)
    row = lambda a: a.astype(F32)[None, :]
    g_final = row(norm_final)
    for i in range(DEPTH):
        h = _ffn(h, row(norm_ffn1[i]), ffn1_gate[i].astype(BF16), ffn1_up[i].astype(BF16),
                 ffn1_down[i].astype(BF16), g_final, False)
        j = i // 2
        if i % 2 == 0:
            h = _mlstm_mixer(h, row(norm_mix[i]), a_w_in[j], a_conv_w[j], a_conv_b[j],
                             a_gate_b[j], a_head_g[j], a_w_out[j], bsz, seq)
        else:
            h = _dilated_mixer(h, row(norm_mix[i]), b_w_in[j], b_w_out[j], bsz, seq)
        h = _ffn(h, row(norm_ffn2[i]), ffn2_gate[i].astype(BF16), ffn2_up[i].astype(BF16),
                 ffn2_down[i].astype(BF16), g_final, i == DEPTH - 1)
    return h.reshape(bsz, seq, d)
```

```python
import functools

import jax
import jax.numpy as jnp
from jax import lax
from jax.experimental import pallas as pl
from jax.experimental.pallas import tpu as pltpu

F32 = jnp.float32
BF16 = jnp.bfloat16

D_MODEL = 1024
DEPTH = 4
D_FF = 2816
RMS_EPS = 1e-6

A_HEADS = 4
A_DQK = 128
A_DV = 256
A_CHUNK = 128
A_CONV = 5
A_QK = 2 * A_HEADS * A_DQK
A_V = A_HEADS * A_DV
A_PAD = 16

B_GROUPS = ((128, 1), (512, 4), (2048, 16))
B_HEADS = 16
B_DH = 64
B_WIDTH = B_HEADS * B_DH
B_HALF = 64
B_TQ = 128
B_TK = B_TQ + 2 * B_HALF
B_SUB = 4
LOG2E = 1.4426950408889634
STAT_LANES = 128 // B_HEADS
NEG_BIG = -1e30

ROW_TILE = 512
A_STEP_CHUNKS = 4
VMEM_LIMIT = 56 * 1024 * 1024


def _cparams(*sem):
    return pltpu.CompilerParams(dimension_semantics=sem, vmem_limit_bytes=VMEM_LIMIT)


def _resident(shape):
    nd = len(shape)
    return pl.BlockSpec(shape, lambda *_: (0,) * nd, pipeline_mode=pl.Buffered(1))


def _rms(x, g):
    ms = jnp.mean(x * x, axis=-1, keepdims=True)
    return x * lax.rsqrt(ms + RMS_EPS) * g


def _dot(a, b):
    return jnp.dot(a, b, preferred_element_type=F32)


def _dot_nt(a, b):
    return lax.dot_general(a, b, (((1,), (1,)), ((), ())), preferred_element_type=F32)


def _dot_tn(a, b):
    return lax.dot_general(a, b, (((0,), (0,)), ((), ())), preferred_element_type=F32)


def _bdot(a, b):
    return lax.dot_general(a, b, (((2,), (1,)), ((0,), (0,))), preferred_element_type=F32)


def _split3(x):
    hi = x.astype(BF16)
    r1 = x - hi.astype(F32)
    mid = r1.astype(BF16)
    lo = (r1 - mid.astype(F32)).astype(BF16)
    return hi, mid, lo


def _ffn_kernel(x_ref, g_ref, wg_ref, wu_ref, wd_ref, gf_ref, o_ref, *, final_norm):
    x = x_ref[...]
    xn = _rms(x, g_ref[...]).astype(BF16)
    gate = _dot(xn, wg_ref[...])
    up = _dot(xn, wu_ref[...])
    act = (gate * jax.nn.sigmoid(gate) * up).astype(BF16)
    y = x + 0.5 * _dot(act, wd_ref[...])
    if final_norm:
        y = _rms(y, gf_ref[...])
    o_ref[...] = y


def _ffn(x, g, wg, wu, wd, g_final, final_norm):
    m, d = x.shape
    f = wg.shape[1]
    row = pl.BlockSpec((ROW_TILE, d), lambda i: (i, 0))
    return pl.pallas_call(
        functools.partial(_ffn_kernel, final_norm=final_norm),
        grid=(m // ROW_TILE,),
        in_specs=[row, _resident((1, d)), _resident((d, f)), _resident((d, f)),
                  _resident((f, d)), _resident((1, d))],
        out_specs=row,
        out_shape=jax.ShapeDtypeStruct((m, d), F32),
        compiler_params=_cparams("parallel"),
        name="ffn",
    )(x, g, wg, wu, wd, g_final)


def _a_proj_kernel(x_ref, g_ref, wqk_ref, wvoT_ref, wli_ref, wlf_ref, wliT_ref, wlfT_ref,
                   bli_ref, blf_ref, bliT_ref, blfT_ref,
                   qk_ref, vT_ref, oT_ref, li_ref, lf_ref, liT_ref, lfT_ref):
    xn = _rms(x_ref[...], g_ref[...]).astype(BF16)
    qk_ref[...] = _dot(xn, wqk_ref[...])
    vT_ref[...] = _dot_nt(wvoT_ref[0:A_V, :], xn).astype(BF16)
    oT_ref[...] = _dot_nt(wvoT_ref[A_V:2 * A_V, :], xn)
    li_ref[...] = _dot(xn, wli_ref[...]) + bli_ref[...]
    lf_ref[...] = jax.nn.log_sigmoid(_dot(xn, wlf_ref[...]) + blf_ref[...])
    liT_ref[...] = _dot_nt(wliT_ref[...], xn) + bliT_ref[...]
    lfT_ref[...] = jax.nn.log_sigmoid(_dot_nt(wlfT_ref[...], xn) + blfT_ref[...])


def _a_proj(x, g, w_qk, w_voT, w_li, w_lf, b_li, b_lf):
    m, d = x.shape
    ng = w_li.shape[1]
    row = lambda n: pl.BlockSpec((ROW_TILE, n), lambda i: (i, 0))
    col = lambda n: pl.BlockSpec((n, ROW_TILE), lambda i: (0, i))
    return pl.pallas_call(
        _a_proj_kernel,
        grid=(m // ROW_TILE,),
        in_specs=[row(d), _resident((1, d)), _resident(w_qk.shape), _resident(w_voT.shape),
                  _resident((d, ng)), _resident((d, ng)), _resident((ng, d)), _resident((ng, d)),
                  _resident((1, ng)), _resident((1, ng)), _resident((ng, 1)), _resident((ng, 1))],
        out_specs=[row(A_QK), col(A_V), col(A_V), row(ng), row(ng), col(ng), col(ng)],
        out_shape=[jax.ShapeDtypeStruct((m, A_QK), F32),
                   jax.ShapeDtypeStruct((A_V, m), BF16),
                   jax.ShapeDtypeStruct((A_V, m), F32),
                   jax.ShapeDtypeStruct((m, ng), F32),
                   jax.ShapeDtypeStruct((m, ng), F32),
                   jax.ShapeDtypeStruct((ng, m), F32),
                   jax.ShapeDtypeStruct((ng, m), F32)],
        compiler_params=_cparams("parallel"),
        name="mlstm_proj",
    )(x, g, w_qk, w_voT, w_li, w_lf, w_li.T, w_lf.T, b_li, b_lf, b_li.T, b_lf.T)


def _conv_kernel(x_ref, prev_ref, next_ref, w_ref, b_ref, k_ref, qT_ref, *, n_steps):
    i = pl.program_id(1)
    x = x_ref[...]
    rows = x.shape[0]
    prev = prev_ref[...] * (i > 0).astype(F32)
    nxt = next_ref[...] * (i < n_steps - 1).astype(F32)
    ridx = lax.broadcasted_iota(jnp.int32, x.shape, 0)
    half = A_CONV // 2
    acc = x * w_ref[half:half + 1, :] + b_ref[...]
    for s in range(1, half + 1):
        back = pltpu.roll(x, s, axis=0)
        fwd = pltpu.roll(x, rows - s, axis=0)
        for e in range(s):
            back = jnp.where(ridx == e, prev[8 - s + e:8 - s + e + 1, :], back)
            fwd = jnp.where(ridx == rows - s + e, nxt[e:e + 1, :], fwd)
        acc = acc + back * w_ref[half - s:half - s + 1, :] + fwd * w_ref[half + s:half + s + 1, :]
    y = acc * jax.nn.sigmoid(acc)
    hq = A_QK // 2
    k_ref[...] = y[:, hq:].astype(BF16)
    qT_ref[...] = (y[:, :hq] * (A_DQK ** -0.5)).T.astype(BF16)


def _conv(qk, w, b, bsz, seq):
    c = qk.shape[-1]
    qk = qk.reshape(bsz, seq, c)
    n_steps = seq // ROW_TILE
    r8 = ROW_TILE // 8
    return pl.pallas_call(
        functools.partial(_conv_kernel, n_steps=n_steps),
        grid=(bsz, n_steps),
        in_specs=[pl.BlockSpec((None, ROW_TILE, c), lambda b_, i: (b_, i, 0)),
                  pl.BlockSpec((None, 8, c), lambda b_, i: (b_, jnp.maximum(i * r8 - 1, 0), 0)),
                  pl.BlockSpec((None, 8, c), lambda b_, i: (b_, jnp.minimum((i + 1) * r8, seq // 8 - 1), 0)),
                  _resident((A_CONV, c)), _resident((1, c))],
        out_specs=[pl.BlockSpec((None, ROW_TILE, c // 2), lambda b_, i: (b_, i, 0)),
                   pl.BlockSpec((None, c // 2, ROW_TILE), lambda b_, i: (b_, 0, i))],
        out_shape=[jax.ShapeDtypeStruct((bsz, seq, c // 2), BF16),
                   jax.ShapeDtypeStruct((bsz, c // 2, seq), BF16)],
        compiler_params=_cparams("parallel", "parallel"),
        name="mlstm_conv",
    )(qk, qk, qk, w, b)


def _mlstm_kernel(qT_ref, k_ref, vT_ref, li_ref, lf_ref, liT_ref, lfT_ref, hT_ref,
                  cT_ref, n_ref, m_ref, *, reverse):
    L = A_CHUNK
    lo = 4 if reverse else 0

    @pl.when(pl.program_id(1) == 0)
    def _():
        cT_ref[...] = jnp.zeros_like(cT_ref)
        n_ref[...] = jnp.zeros_like(n_ref)
        m_ref[...] = jnp.zeros_like(m_ref)

    row_i = lax.broadcasted_iota(jnp.int32, (L, L), 0)
    col_i = lax.broadcasted_iota(jnp.int32, (L, L), 1)
    seen = (col_i >= row_i) if reverse else (col_i <= row_i)
    seen_t = (row_i >= col_i) if reverse else (row_i <= col_i)
    seen_bf = seen.astype(F32).astype(BF16)
    last = 0 if reverse else L - 1

    H = A_HEADS
    c_state = cT_ref[...]
    n_state = n_ref[...]
    m_state = m_ref[...]

    order = range(A_STEP_CHUNKS - 1, -1, -1) if reverse else range(A_STEP_CHUNKS)
    for j in order:
        r0 = j * L
        li = li_ref[r0:r0 + L, :]
        lf = lf_ref[r0:r0 + L, :]
        liT = liT_ref[:, r0:r0 + L]
        lfT = lfT_ref[:, r0:r0 + L]
        b_col = sum(_dot(seen_bf, p) for p in _split3(lf))
        b_row = sum(_dot_nt(p, seen_bf) for p in _split3(lfT))
        r_col = li - b_col
        b_tot = b_row[:, last:last + 1]
        a_row = b_tot - b_row + liT
        m_loc = jnp.max(a_row, axis=1, keepdims=True)
        w_row = jnp.exp(a_row - m_loc)
        g_rows = b_row + m_state
        m_new = jnp.maximum(b_tot + m_state, m_loc)
        s_prev = jnp.exp(b_tot + m_state - m_new)
        s_loc = jnp.exp(m_loc - m_new)
        m_state = m_new

        heads = lambda rows: rows[lo:lo + H][:, None, :]
        qT = qT_ref[:, r0:r0 + L].reshape(H, A_DQK, L)
        vT = vT_ref[:, r0:r0 + L].reshape(H, A_DV, L)
        k = jnp.stack([k_ref[r0:r0 + L, h * A_DQK:(h + 1) * A_DQK] for h in range(H)])
        k_aug = jnp.concatenate(
            [k, jnp.broadcast_to(n_state, (H, A_PAD, A_DQK)).astype(BF16)], axis=1)
        sq = _bdot(k_aug, qT)
        qn = sq[:, L:L + 1, :]
        d = jnp.stack([r_col[:, lo + h:lo + h + 1] + b_row[lo + h:lo + h + 1, :]
                       for h in range(H)])
        d = jnp.where(seen_t[None], d, -jnp.inf)
        g = heads(g_rows)
        m_t = jnp.maximum(g, jnp.max(d, axis=1, keepdims=True))
        p = jnp.exp(d - m_t) * sq[:, :L, :]
        s_inter = jnp.exp(g - m_t)
        den = s_inter * qn + jnp.sum(p, axis=1, keepdims=True)
        inv = 1.0 / jnp.maximum(jnp.abs(den), jnp.exp(-m_t))
        rhs = jnp.concatenate([(qT.astype(F32) * (s_inter * inv)).astype(BF16),
                               (p * inv).astype(BF16)], axis=1)
        lhs = jnp.concatenate([c_state.astype(BF16), vT], axis=2)
        hT_ref[:, r0:r0 + L] = _bdot(lhs, rhs).reshape(A_V, L)

        w = heads(w_row)
        vw_aug = jnp.concatenate([(vT.astype(F32) * w).astype(BF16),
                                  jnp.broadcast_to(w, (H, A_PAD, L)).astype(BF16)], axis=1)
        upd = _bdot(vw_aug, k)
        sp = heads(s_prev)
        sl = heads(s_loc)
        c_state = sp * c_state + sl * upd[:, :A_DV, :]
        n_state = sp * n_state + sl * upd[:, A_DV:A_DV + 1, :]

    cT_ref[...] = c_state
    n_ref[...] = n_state
    m_ref[...] = m_state


def _mlstm(k, qT, vT, li, lf, liT, lfT, bsz, seq, reverse):
    tl = A_STEP_CHUNKS * A_CHUNK
    ns = seq // tl
    ng = li.shape[-1]
    half = A_QK // 2
    blk = (lambda i: ns - 1 - i) if reverse else (lambda i: i)
    li = li.reshape(bsz, seq, ng)
    lf = lf.reshape(bsz, seq, ng)
    gate = pl.BlockSpec((None, tl, ng), lambda b_, i: (b_, blk(i), 0))
    feat = lambda n: pl.BlockSpec((n, tl), lambda b_, i: (0, b_ * ns + blk(i)))
    return pl.pallas_call(
        functools.partial(_mlstm_kernel, reverse=reverse),
        grid=(bsz, ns),
        in_specs=[pl.BlockSpec((None, half, tl), lambda b_, i: (b_, 0, blk(i))),
                  pl.BlockSpec((None, tl, half), lambda b_, i: (b_, blk(i), 0)),
                  feat(A_V), gate, gate, feat(ng), feat(ng)],
        out_specs=feat(A_V),
        out_shape=jax.ShapeDtypeStruct((A_V, bsz * seq), F32),
        scratch_shapes=[pltpu.VMEM((A_HEADS, A_DV, A_DQK), F32),
                        pltpu.VMEM((A_HEADS, 1, A_DQK), F32),
                        pltpu.VMEM((ng, A_CHUNK), F32)],
        compiler_params=_cparams("parallel", "arbitrary"),
        name="mlstm_bwd" if reverse else "mlstm_fwd",
    )(qT, k, vT, li, lf, liT, lfT)


def _a_out_kernel(x_ref, hfT_ref, hbT_ref, oT_ref, hg_ref, w_ref, y_ref):
    yT = hfT_ref[...] + hbT_ref[...]
    parts = []
    for h in range(A_HEADS):
        yh = yT[h * A_DV:(h + 1) * A_DV, :]
        ms = jnp.mean(yh * yh, axis=0, keepdims=True)
        parts.append(yh * lax.rsqrt(ms + RMS_EPS))
    ynT = jnp.concatenate(parts, axis=0)
    zT = (ynT * hg_ref[...] * jax.nn.sigmoid(oT_ref[...])).astype(BF16)
    y_ref[...] = x_ref[...] + _dot_tn(zT, w_ref[...])


def _a_out(x, hfT, hbT, oT, hg, w):
    m, d = x.shape
    row = pl.BlockSpec((ROW_TILE, d), lambda i: (i, 0))
    col = pl.BlockSpec((A_V, ROW_TILE), lambda i: (0, i))
    return pl.pallas_call(
        _a_out_kernel,
        grid=(m // ROW_TILE,),
        in_specs=[row, col, col, col, _resident((A_V, ROW_TILE)), _resident((A_V, d))],
        out_specs=row,
        out_shape=jax.ShapeDtypeStruct((m, d), F32),
        compiler_params=_cparams("parallel"),
        name="mlstm_out",
    )(x, hfT, hbT, oT, hg, w)


def _b_proj_kernel(x_ref, g_ref, w_ref, *rest):
    o_refs, xs_ref = rest[:-1], rest[-1]
    xn = _rms(x_ref[...], g_ref[...])
    nblk = D_MODEL // 128
    lhs = {1: xn.astype(BF16)}
    for c in range(nblk):
        xs_ref[c] = xn[:, c * 128:(c + 1) * 128]
    for dil in sorted({d for _, d in B_GROUPS if d > 1}):
        n = ROW_TILE // dil
        pieces = [jnp.concatenate([xs_ref[c, pl.ds(r, n, stride=dil), :] for c in range(nblk)],
                                  axis=-1).astype(BF16) for r in range(dil)]
        lhs[dil] = jnp.concatenate(pieces, axis=0)
    for j, o_ref in enumerate(o_refs):
        dil = B_GROUPS[j // 3][1]
        y = _dot(lhs[dil], w_ref[:, j * B_WIDTH:(j + 1) * B_WIDTH])
        if j % 3 == 0:
            y = y * (B_DH ** -0.5 * LOG2E)
        y = y.astype(BF16)
        n = ROW_TILE // dil
        for r in range(dil):
            o_ref[:, r * B_WIDTH:(r + 1) * B_WIDTH] = y[r * n:(r + 1) * n, :]


def _b_proj(x, g, w):
    m, d = x.shape
    dils = [dil for _, dil in B_GROUPS for _ in range(3)]
    return pl.pallas_call(
        _b_proj_kernel,
        grid=(m // ROW_TILE,),
        in_specs=[pl.BlockSpec((ROW_TILE, d), lambda i: (i, 0)), _resident((1, d)),
                  _resident(w.shape)],
        out_specs=[pl.BlockSpec((ROW_TILE // dil, dil * B_WIDTH), lambda i: (i, 0)) for dil in dils],
        out_shape=[jax.ShapeDtypeStruct((m // dil, dil * B_WIDTH), BF16) for dil in dils],
        scratch_shapes=[pltpu.VMEM((B_WIDTH // 128, ROW_TILE, 128), F32)],
        compiler_params=_cparams("parallel"),
        name="dilated_proj",
    )(x, g, w)


def _b_attn_kernel(q_ref, kp_ref, kc_ref, kn_ref, vp_ref, vc_ref, vn_ref,
                   bias_first_ref, bias_mid_ref, bias_last_ref, o_ref, stat_ref):
    k_all = jnp.concatenate([kp_ref[...], kc_ref[...], kn_ref[...]], axis=0)
    v_all = jnp.concatenate([vp_ref[...], vc_ref[...], vn_ref[...]], axis=0)
    lane = lax.broadcasted_iota(jnp.int32, (B_TQ, 128), 1)
    low = lane < B_DH
    stat_slot = lane // (STAT_LANES // 2)
    for sb in range(B_SUB):
        bias_ref = bias_first_ref if sb == 0 else (bias_last_ref if sb == B_SUB - 1 else bias_mid_ref)
        q0 = sb * B_TQ
        stat_row = jnp.zeros((B_TQ, 128), F32)
        for hp in range(B_HEADS // 2):
            cols = slice(hp * 128, (hp + 1) * 128)
            q2 = q_ref[q0:q0 + B_TQ, cols]
            k2 = k_all[q0:q0 + B_TK, cols]
            v2 = v_all[q0:q0 + B_TK, cols]
            zero = jnp.zeros_like(q2)
            outs = []
            for par, qsel in ((0, jnp.where(low, q2, zero)), (1, jnp.where(low, zero, q2))):
                hd = 2 * hp + par
                s = _dot_nt(qsel, k2) + bias_ref[hd]
                mx = jnp.max(s, axis=-1, keepdims=True)
                e = jnp.exp2(s - mx)
                den = jnp.sum(e, axis=-1, keepdims=True)
                outs.append(_dot(e.astype(BF16), v2))
                stat_row = jnp.where(stat_slot == 2 * hd, mx,
                                     jnp.where(stat_slot == 2 * hd + 1, den, stat_row))
            o_ref[q0:q0 + B_TQ, cols] = jnp.where(low, outs[0], outs[1]).astype(BF16)
        stat_ref[q0:q0 + B_TQ, :] = stat_row


def _b_attn(q, k, v, bias, bsz, seq, dil):
    n = seq // dil
    rows = B_SUB * B_TQ
    nq = n // rows
    w = B_WIDTH
    q = q.reshape(bsz, n, dil * w)
    k = k.reshape(bsz, n, dil * w)
    v = v.reshape(bsz, n, dil * w)
    hb = rows // B_HALF
    last_half = n // B_HALF - 1
    cur = pl.BlockSpec((None, rows, w), lambda b_, r, i: (b_, i, r))
    prev = pl.BlockSpec((None, B_HALF, w), lambda b_, r, i: (b_, jnp.maximum(i * hb - 1, 0), r))
    nxt = pl.BlockSpec((None, B_HALF, w), lambda b_, r, i: (b_, jnp.minimum((i + 1) * hb, last_half), r))
    at_start = lambda i: jnp.where(i == 0, 1, 0)
    at_end = lambda i: jnp.where(i == nq - 1, 2, 0)
    first_variant = (lambda i: at_start(i) + at_end(i)) if B_SUB == 1 else at_start
    bias_blk = (None, B_HEADS, B_TQ, B_TK)
    bias_first = pl.BlockSpec(bias_blk, lambda b_, r, i: (first_variant(i), 0, 0, 0))
    bias_mid = pl.BlockSpec(bias_blk, lambda b_, r, i: (0, 0, 0, 0), pipeline_mode=pl.Buffered(1))
    bias_last = pl.BlockSpec(bias_blk, lambda b_, r, i: (at_end(i), 0, 0, 0))
    out, stats = pl.pallas_call(
        _b_attn_kernel,
        grid=(bsz, dil, nq),
        in_specs=[cur, prev, cur, nxt, prev, cur, nxt, bias_first, bias_mid, bias_last],
        out_specs=[cur, pl.BlockSpec((None, rows, 128), lambda b_, r, i: (b_, i, r))],
        out_shape=[jax.ShapeDtypeStruct((bsz, n, dil * w), BF16),
                   jax.ShapeDtypeStruct((bsz, n, dil * 128), F32)],
        compiler_params=_cparams("parallel", "parallel", "arbitrary"),
        name=f"dilated_attn_d{dil}",
    )(q, k, k, k, v, v, v, bias, bias, bias)
    return out.reshape(bsz * n, dil * w), stats.reshape(bsz * n, dil * 128)


def _attn_bias(dil):
    slopes = jnp.exp2(-8.0 * (jnp.arange(B_HEADS, dtype=F32) + 1.0) / B_HEADS)
    tq = jnp.arange(B_TQ)[:, None]
    col = jnp.arange(B_TK)[None, :]
    rel = jnp.abs(col - B_HALF - tq)
    base = -slopes[:, None, None] * (rel * dil).astype(F32)[None] * LOG2E
    band = (rel <= B_HALF)[None]
    first = (col >= B_HALF)[None]
    lastv = (col < B_TQ + B_HALF)[None]
    variants = []
    for need_first, need_last in ((False, False), (True, False), (False, True), (True, True)):
        ok = band
        if need_first:
            ok = ok & first
        if need_last:
            ok = ok & lastv
        variants.append(jnp.where(ok, base, NEG_BIG))
    return jnp.stack(variants, axis=0)


def _b_out_kernel(x_ref, o0_ref, o1_ref, o2_ref, s0_ref, s1_ref, s2_ref, e_ref, w_ref, y_ref,
                  os_ref, ss_ref):
    for gi, (s_ref, (_, dil)) in enumerate(zip((s0_ref, s1_ref, s2_ref), B_GROUPS)):
        for r in range(dil):
            ss_ref[gi, pl.ds(r, ROW_TILE // dil, stride=dil), :] = s_ref[:, r * 128:(r + 1) * 128]
    half = STAT_LANES // 2
    mxs = [ss_ref[gi] for gi in range(len(B_GROUPS))]
    dens = [pltpu.roll(m, 128 - half, axis=1) for m in mxs]
    top = jnp.maximum(jnp.maximum(mxs[0], mxs[1]), mxs[2])
    ts = [jnp.exp2(m - top) for m in mxs]
    total = dens[0] * ts[0] + dens[1] * ts[1] + dens[2] * ts[2]
    lane = lax.broadcasted_iota(jnp.int32, total.shape, 1)
    picked = lane % STAT_LANES == 0
    acc = None
    for gi, (t, o_ref, (_, dil)) in enumerate(zip(ts, (o0_ref, o1_ref, o2_ref), B_GROUPS)):
        weight = jnp.where(picked, t / total, 0.0)
        wide = sum(_dot(p, e_ref[...]) for p in _split3(weight))
        nblk = B_WIDTH // 128
        for r in range(dil):
            for c in range(nblk):
                lo = r * B_WIDTH + c * 128
                os_ref[gi, c, pl.ds(r, ROW_TILE // dil, stride=dil), :] = (
                    o_ref[:, lo:lo + 128].astype(F32))
        term = wide * jnp.concatenate([os_ref[gi, c] for c in range(nblk)], axis=-1)
        acc = term if acc is None else acc + term
    y_ref[...] = x_ref[...] + _dot(acc.astype(BF16), w_ref[...])


def _b_out(x, outs, stats, w):
    m, d = x.shape
    head_of_col = jnp.arange(B_WIDTH) // B_DH
    expand = (jnp.arange(128)[:, None] == head_of_col[None, :] * STAT_LANES).astype(BF16)
    row = pl.BlockSpec((ROW_TILE, d), lambda i: (i, 0))
    grp = lambda lanes: [pl.BlockSpec((ROW_TILE // dil, dil * lanes), lambda i: (i, 0))
                         for _, dil in B_GROUPS]
    ng = len(B_GROUPS)
    return pl.pallas_call(
        _b_out_kernel,
        grid=(m // ROW_TILE,),
        in_specs=[row, *grp(B_WIDTH), *grp(128), _resident((128, B_WIDTH)),
                  _resident((B_WIDTH, d))],
        out_specs=row,
        out_shape=jax.ShapeDtypeStruct((m, d), F32),
        scratch_shapes=[pltpu.VMEM((ng, B_WIDTH // 128, ROW_TILE, 128), F32),
                        pltpu.VMEM((ng, ROW_TILE, 128), F32)],
        compiler_params=_cparams("parallel"),
        name="dilated_out",
    )(x, *outs, *stats, expand, w)


def _mlstm_mixer(x, g, w_in, conv_w, conv_b, gate_b, head_g, w_out, bsz, seq):
    w_qk = w_in[:, :A_QK].astype(BF16)
    w_voT = w_in[:, A_QK:A_QK + 2 * A_V].T.astype(BF16)
    wg = w_in[:, A_QK + 2 * A_V:]
    H = A_HEADS
    pick = lambda a, s: jnp.concatenate([a[..., s * H:(s + 1) * H], a[..., (s + 2) * H:(s + 3) * H]], axis=-1)
    w_li, w_lf = pick(wg, 0).astype(BF16), pick(wg, 1).astype(BF16)
    gb = gate_b.astype(F32)[None, :]
    b_li, b_lf = pick(gb, 0), pick(gb, 1)
    qk, vT, oT, li, lf, liT, lfT = _a_proj(x, g, w_qk, w_voT, w_li, w_lf, b_li, b_lf)
    k, qT = _conv(qk, conv_w.astype(F32), conv_b.astype(F32)[None, :], bsz, seq)
    hfT = _mlstm(k, qT, vT, li, lf, liT, lfT, bsz, seq, reverse=False)
    hbT = _mlstm(k, qT, vT, li, lf, liT, lfT, bsz, seq, reverse=True)
    hg = jnp.broadcast_to(head_g.astype(F32)[:, None], (A_V, ROW_TILE))
    return _a_out(x, hfT, hbT, oT, hg, w_out.astype(BF16))


def _dilated_mixer(x, g, w_in, w_out, bsz, seq):
    slabs = _b_proj(x, g, w_in.astype(BF16))
    outs, stats = [], []
    for gi, (_, dil) in enumerate(B_GROUPS):
        o, l = _b_attn(slabs[3 * gi], slabs[3 * gi + 1], slabs[3 * gi + 2], _attn_bias(dil),
                       bsz, seq, dil)
        outs.append(o)
        stats.append(l)
    return _b_out(x, outs, stats, w_out.astype(BF16))


def kernel(x, norm_ffn1, ffn1_gate, ffn1_up, ffn1_down, norm_mix, a_w_in, a_conv_w, a_conv_b,
           a_gate_b, a_head_g, a_w_out, b_w_in, b_w_out, norm_ffn2, ffn2_gate, ffn2_up,
           ffn2_down, norm_final):
    bsz, seq, d = x.shape
    h = x.reshape(bsz * seq, d)
    row = lambda a: a.astype(F32)[None, :]
    g_final = row(norm_final)
    for i in range(DEPTH):
        h = _ffn(h, row(norm_ffn1[i]), ffn1_gate[i].astype(BF16), ffn1_up[i].astype(BF16),
                 ffn1_down[i].astype(BF16), g_final, False)
        j = i // 2
        if i % 2 == 0:
            h = _mlstm_mixer(h, row(norm_mix[i]), a_w_in[j], a_conv_w[j], a_conv_b[j],
                             a_gate_b[j], a_head_g[j], a_w_out[j], bsz, seq)
        else:
            h = _dilated_mixer(h, row(norm_mix[i]), b_w_in[j], b_w_out[j], bsz, seq)
        h = _ffn(h, row(norm_ffn2[i]), ffn2_gate[i].astype(BF16), ffn2_up[i].astype(BF16),
                 ffn2_down[i].astype(BF16), g_final, i == DEPTH - 1)
    return h.reshape(bsz, seq, d)
```

```python
import functools

import jax
import jax.numpy as jnp
from jax import lax
from jax.experimental import pallas as pl
from jax.experimental.pallas import tpu as pltpu

F32 = jnp.float32
BF16 = jnp.bfloat16

D_MODEL = 1024
DEPTH = 4
D_FF = 2816
RMS_EPS = 1e-6

A_HEADS = 4
A_DQK = 128
A_DV = 256
A_CHUNK = 128
A_CONV = 5
A_QK = 2 * A_HEADS * A_DQK
A_V = A_HEADS * A_DV
A_PAD = 16

B_GROUPS = ((128, 1), (512, 4), (2048, 16))
B_HEADS = 16
B_DH = 64
B_WIDTH = B_HEADS * B_DH
B_HALF = 64
B_TQ = 128
B_TK = B_TQ + 2 * B_HALF
B_SUB = 8
LOG2E = 1.4426950408889634
STAT_LANES = 128 // B_HEADS
NEG_BIG = -1e30

ROW_TILE = 512
W_STAGE_ROWS = 128
A_STEP_CHUNKS = 4
VMEM_LIMIT = 56 * 1024 * 1024


def _cparams(*sem):
    return pltpu.CompilerParams(dimension_semantics=sem, vmem_limit_bytes=VMEM_LIMIT)


def _resident(shape):
    nd = len(shape)
    return pl.BlockSpec(shape, lambda *_: (0,) * nd, pipeline_mode=pl.Buffered(1))


def _rms(x, g):
    ms = jnp.mean(x * x, axis=-1, keepdims=True)
    return x * lax.rsqrt(ms + RMS_EPS) * g


def _dot(a, b):
    return jnp.dot(a, b, preferred_element_type=F32)


def _dot_nt(a, b):
    return lax.dot_general(a, b, (((1,), (1,)), ((), ())), preferred_element_type=F32)


def _dot_tn(a, b):
    return lax.dot_general(a, b, (((0,), (0,)), ((), ())), preferred_element_type=F32)


def _bdot(a, b):
    return lax.dot_general(a, b, (((2,), (1,)), ((0,), (0,))), preferred_element_type=F32)


def _split3(x):
    hi = x.astype(BF16)
    r1 = x - hi.astype(F32)
    mid = r1.astype(BF16)
    lo = (r1 - mid.astype(F32)).astype(BF16)
    return hi, mid, lo


def _stage_copy(src_hbm, stage_ref, sem, chunk, c):
    slot = c % 2
    return pltpu.make_async_copy(src_hbm.at[pl.ds(c * chunk, chunk), :], stage_ref.at[slot],
                                 sem.at[slot])


def _load_as_bf16(src_hbm, dst_ref, stage_ref, sem):
    chunk = stage_ref.shape[1]
    n = src_hbm.shape[0] // chunk
    _stage_copy(src_hbm, stage_ref, sem, chunk, 0).start()
    for c in range(n):
        if c + 1 < n:
            _stage_copy(src_hbm, stage_ref, sem, chunk, c + 1).start()
        _stage_copy(src_hbm, stage_ref, sem, chunk, c).wait()
        dst_ref[c * chunk:(c + 1) * chunk, :] = stage_ref[c % 2].astype(BF16)


def _ffn_kernel(x_ref, g_ref, wg_hbm, wu_hbm, wd_hbm, gf_ref, o_ref,
                wg_ref, wu_ref, wd_ref, stage_in_ref, stage_out_ref, sem, *, final_norm):
    @pl.when(pl.program_id(0) == 0)
    def _():
        _load_as_bf16(wg_hbm, wg_ref, stage_in_ref, sem)
        _load_as_bf16(wu_hbm, wu_ref, stage_in_ref, sem)
        _load_as_bf16(wd_hbm, wd_ref, stage_out_ref, sem)

    x = x_ref[...]
    xn = _rms(x, g_ref[...]).astype(BF16)
    gate = _dot(xn, wg_ref[...])
    up = _dot(xn, wu_ref[...])
    act = (gate * jax.nn.sigmoid(gate) * up).astype(BF16)
    y = x + 0.5 * _dot(act, wd_ref[...])
    if final_norm:
        y = _rms(y, gf_ref[...])
    o_ref[...] = y


def _ffn(x, g, wg, wu, wd, g_final, final_norm):
    m, d = x.shape
    f = wg.shape[1]
    row = pl.BlockSpec((ROW_TILE, d), lambda i: (i, 0))
    hbm = pl.BlockSpec(memory_space=pltpu.MemorySpace.HBM)
    return pl.pallas_call(
        functools.partial(_ffn_kernel, final_norm=final_norm),
        grid=(m // ROW_TILE,),
        in_specs=[row, _resident((1, d)), hbm, hbm, hbm, _resident((1, d))],
        out_specs=row,
        out_shape=jax.ShapeDtypeStruct((m, d), F32),
        scratch_shapes=[pltpu.VMEM((d, f), BF16), pltpu.VMEM((d, f), BF16), pltpu.VMEM((f, d), BF16),
                        pltpu.VMEM((2, W_STAGE_ROWS, f), F32), pltpu.VMEM((2, W_STAGE_ROWS, d), F32),
                        pltpu.SemaphoreType.DMA((2,))],
        compiler_params=_cparams("arbitrary"),
        name="ffn",
    )(x, g, wg, wu, wd, g_final)


def _a_proj_kernel(x_ref, g_ref, wqk_ref, wvoT_ref, wli_ref, wlf_ref, wliT_ref, wlfT_ref,
                   bli_ref, blf_ref, bliT_ref, blfT_ref,
                   qk_ref, vT_ref, oT_ref, li_ref, lf_ref, liT_ref, lfT_ref):
    xn = _rms(x_ref[...], g_ref[...]).astype(BF16)
    qk_ref[...] = _dot(xn, wqk_ref[...])
    vT_ref[...] = _dot_nt(wvoT_ref[0:A_V, :], xn).astype(BF16)
    oT_ref[...] = _dot_nt(wvoT_ref[A_V:2 * A_V, :], xn)
    li_ref[...] = _dot(xn, wli_ref[...]) + bli_ref[...]
    lf_ref[...] = jax.nn.log_sigmoid(_dot(xn, wlf_ref[...]) + blf_ref[...])
    liT_ref[...] = _dot_nt(wliT_ref[...], xn) + bliT_ref[...]
    lfT_ref[...] = jax.nn.log_sigmoid(_dot_nt(wlfT_ref[...], xn) + blfT_ref[...])


def _a_proj(x, g, w_qk, w_voT, w_li, w_lf, b_li, b_lf):
    m, d = x.shape
    ng = w_li.shape[1]
    row = lambda n: pl.BlockSpec((ROW_TILE, n), lambda i: (i, 0))
    col = lambda n: pl.BlockSpec((n, ROW_TILE), lambda i: (0, i))
    return pl.pallas_call(
        _a_proj_kernel,
        grid=(m // ROW_TILE,),
        in_specs=[row(d), _resident((1, d)), _resident(w_qk.shape), _resident(w_voT.shape),
                  _resident((d, ng)), _resident((d, ng)), _resident((ng, d)), _resident((ng, d)),
                  _resident((1, ng)), _resident((1, ng)), _resident((ng, 1)), _resident((ng, 1))],
        out_specs=[row(A_QK), col(A_V), col(A_V), row(ng), row(ng), col(ng), col(ng)],
        out_shape=[jax.ShapeDtypeStruct((m, A_QK), F32),
                   jax.ShapeDtypeStruct((A_V, m), BF16),
                   jax.ShapeDtypeStruct((A_V, m), F32),
                   jax.ShapeDtypeStruct((m, ng), F32),
                   jax.ShapeDtypeStruct((m, ng), F32),
                   jax.ShapeDtypeStruct((ng, m), F32),
                   jax.ShapeDtypeStruct((ng, m), F32)],
        compiler_params=_cparams("parallel"),
        name="mlstm_proj",
    )(x, g, w_qk, w_voT, w_li, w_lf, w_li.T, w_lf.T, b_li, b_lf, b_li.T, b_lf.T)


def _conv_kernel(x_ref, prev_ref, next_ref, w_ref, b_ref, k_ref, qT_ref, *, n_steps):
    i = pl.program_id(1)
    x = x_ref[...]
    rows = x.shape[0]
    prev = prev_ref[...] * (i > 0).astype(F32)
    nxt = next_ref[...] * (i < n_steps - 1).astype(F32)
    ridx = lax.broadcasted_iota(jnp.int32, x.shape, 0)
    half = A_CONV // 2
    acc = x * w_ref[half:half + 1, :] + b_ref[...]
    for s in range(1, half + 1):
        back = pltpu.roll(x, s, axis=0)
        fwd = pltpu.roll(x, rows - s, axis=0)
        for e in range(s):
            back = jnp.where(ridx == e, prev[8 - s + e:8 - s + e + 1, :], back)
            fwd = jnp.where(ridx == rows - s + e, nxt[e:e + 1, :], fwd)
        acc = acc + back * w_ref[half - s:half - s + 1, :] + fwd * w_ref[half + s:half + s + 1, :]
    y = acc * jax.nn.sigmoid(acc)
    hq = A_QK // 2
    k_ref[...] = y[:, hq:].astype(BF16)
    qT_ref[...] = (y[:, :hq] * (A_DQK ** -0.5)).T.astype(BF16)


def _conv(qk, w, b, bsz, seq):
    c = qk.shape[-1]
    qk = qk.reshape(bsz, seq, c)
    n_steps = seq // ROW_TILE
    r8 = ROW_TILE // 8
    return pl.pallas_call(
        functools.partial(_conv_kernel, n_steps=n_steps),
        grid=(bsz, n_steps),
        in_specs=[pl.BlockSpec((None, ROW_TILE, c), lambda b_, i: (b_, i, 0)),
                  pl.BlockSpec((None, 8, c), lambda b_, i: (b_, jnp.maximum(i * r8 - 1, 0), 0)),
                  pl.BlockSpec((None, 8, c), lambda b_, i: (b_, jnp.minimum((i + 1) * r8, seq // 8 - 1), 0)),
                  _resident((A_CONV, c)), _resident((1, c))],
        out_specs=[pl.BlockSpec((None, ROW_TILE, c // 2), lambda b_, i: (b_, i, 0)),
                   pl.BlockSpec((None, c // 2, ROW_TILE), lambda b_, i: (b_, 0, i))],
        out_shape=[jax.ShapeDtypeStruct((bsz, seq, c // 2), BF16),
                   jax.ShapeDtypeStruct((bsz, c // 2, seq), BF16)],
        compiler_params=_cparams("parallel", "parallel"),
        name="mlstm_conv",
    )(qk, qk, qk, w, b)


def _mlstm_kernel(qT_ref, k_ref, vT_ref, li_ref, lf_ref, liT_ref, lfT_ref, hT_ref,
                  cT_ref, n_ref, m_ref, *, reverse):
    L = A_CHUNK
    lo = 4 if reverse else 0

    @pl.when(pl.program_id(1) == 0)
    def _():
        cT_ref[...] = jnp.zeros_like(cT_ref)
        n_ref[...] = jnp.zeros_like(n_ref)
        m_ref[...] = jnp.zeros_like(m_ref)

    row_i = lax.broadcasted_iota(jnp.int32, (L, L), 0)
    col_i = lax.broadcasted_iota(jnp.int32, (L, L), 1)
    seen = (col_i >= row_i) if reverse else (col_i <= row_i)
    seen_t = (row_i >= col_i) if reverse else (row_i <= col_i)
    seen_bf = seen.astype(F32).astype(BF16)
    last = 0 if reverse else L - 1

    H = A_HEADS
    c_state = cT_ref[...]
    n_state = n_ref[...]
    m_state = m_ref[...]

    order = range(A_STEP_CHUNKS - 1, -1, -1) if reverse else range(A_STEP_CHUNKS)
    for j in order:
        r0 = j * L
        li = li_ref[r0:r0 + L, :]
        lf = lf_ref[r0:r0 + L, :]
        liT = liT_ref[:, r0:r0 + L]
        lfT = lfT_ref[:, r0:r0 + L]
        b_col = sum(_dot(seen_bf, p) for p in _split3(lf))
        b_row = sum(_dot_nt(p, seen_bf) for p in _split3(lfT))
        r_col = li - b_col
        b_tot = b_row[:, last:last + 1]
        a_row = b_tot - b_row + liT
        m_loc = jnp.max(a_row, axis=1, keepdims=True)
        w_row = jnp.exp(a_row - m_loc)
        g_rows = b_row + m_state
        m_new = jnp.maximum(b_tot + m_state, m_loc)
        s_prev = jnp.exp(b_tot + m_state - m_new)
        s_loc = jnp.exp(m_loc - m_new)
        m_state = m_new

        heads = lambda rows: rows[lo:lo + H][:, None, :]
        qT = qT_ref[:, r0:r0 + L].reshape(H, A_DQK, L)
        vT = vT_ref[:, r0:r0 + L].reshape(H, A_DV, L)
        k = jnp.stack([k_ref[r0:r0 + L, h * A_DQK:(h + 1) * A_DQK] for h in range(H)])
        k_aug = jnp.concatenate(
            [k, jnp.broadcast_to(n_state, (H, A_PAD, A_DQK)).astype(BF16)], axis=1)
        sq = _bdot(k_aug, qT)
        qn = sq[:, L:L + 1, :]
        d = jnp.stack([r_col[:, lo + h:lo + h + 1] + b_row[lo + h:lo + h + 1, :]
                       for h in range(H)])
        d = jnp.where(seen_t[None], d, -jnp.inf)
        g = heads(g_rows)
        m_t = jnp.maximum(g, jnp.max(d, axis=1, keepdims=True))
        p = jnp.exp(d - m_t) * sq[:, :L, :]
        s_inter = jnp.exp(g - m_t)
        den = s_inter * qn + jnp.sum(p, axis=1, keepdims=True)
        inv = 1.0 / jnp.maximum(jnp.abs(den), jnp.exp(-m_t))
        rhs = jnp.concatenate([(qT.astype(F32) * (s_inter * inv)).astype(BF16),
                               (p * inv).astype(BF16)], axis=1)
        lhs = jnp.concatenate([c_state.astype(BF16), vT], axis=2)
        hT_ref[:, r0:r0 + L] = _bdot(lhs, rhs).reshape(A_V, L)

        w = heads(w_row)
        vw_aug = jnp.concatenate([(vT.astype(F32) * w).astype(BF16),
                                  jnp.broadcast_to(w, (H, A_PAD, L)).astype(BF16)], axis=1)
        upd = _bdot(vw_aug, k)
        sp = heads(s_prev)
        sl = heads(s_loc)
        c_state = sp * c_state + sl * upd[:, :A_DV, :]
        n_state = sp * n_state + sl * upd[:, A_DV:A_DV + 1, :]

    cT_ref[...] = c_state
    n_ref[...] = n_state
    m_ref[...] = m_state


def _mlstm(k, qT, vT, li, lf, liT, lfT, bsz, seq, reverse):
    tl = A_STEP_CHUNKS * A_CHUNK
    ns = seq // tl
    ng = li.shape[-1]
    half = A_QK // 2
    blk = (lambda i: ns - 1 - i) if reverse else (lambda i: i)
    li = li.reshape(bsz, seq, ng)
    lf = lf.reshape(bsz, seq, ng)
    gate = pl.BlockSpec((None, tl, ng), lambda b_, i: (b_, blk(i), 0))
    feat = lambda n: pl.BlockSpec((n, tl), lambda b_, i: (0, b_ * ns + blk(i)))
    return pl.pallas_call(
        functools.partial(_mlstm_kernel, reverse=reverse),
        grid=(bsz, ns),
        in_specs=[pl.BlockSpec((None, half, tl), lambda b_, i: (b_, 0, blk(i))),
                  pl.BlockSpec((None, tl, half), lambda b_, i: (b_, blk(i), 0)),
                  feat(A_V), gate, gate, feat(ng), feat(ng)],
        out_specs=feat(A_V),
        out_shape=jax.ShapeDtypeStruct((A_V, bsz * seq), F32),
        scratch_shapes=[pltpu.VMEM((A_HEADS, A_DV, A_DQK), F32),
                        pltpu.VMEM((A_HEADS, 1, A_DQK), F32),
                        pltpu.VMEM((ng, A_CHUNK), F32)],
        compiler_params=_cparams("parallel", "arbitrary"),
        name="mlstm_bwd" if reverse else "mlstm_fwd",
    )(qT, k, vT, li, lf, liT, lfT)


def _a_out_kernel(x_ref, hfT_ref, hbT_ref, oT_ref, hg_ref, w_ref, y_ref):
    yT = hfT_ref[...] + hbT_ref[...]
    parts = []
    for h in range(A_HEADS):
        yh = yT[h * A_DV:(h + 1) * A_DV, :]
        ms = jnp.mean(yh * yh, axis=0, keepdims=True)
        parts.append(yh * lax.rsqrt(ms + RMS_EPS))
    ynT = jnp.concatenate(parts, axis=0)
    zT = (ynT * hg_ref[...] * jax.nn.sigmoid(oT_ref[...])).astype(BF16)
    y_ref[...] = x_ref[...] + _dot_tn(zT, w_ref[...])


def _a_out(x, hfT, hbT, oT, hg, w):
    m, d = x.shape
    row = pl.BlockSpec((ROW_TILE, d), lambda i: (i, 0))
    col = pl.BlockSpec((A_V, ROW_TILE), lambda i: (0, i))
    return pl.pallas_call(
        _a_out_kernel,
        grid=(m // ROW_TILE,),
        in_specs=[row, col, col, col, _resident((A_V, ROW_TILE)), _resident((A_V, d))],
        out_specs=row,
        out_shape=jax.ShapeDtypeStruct((m, d), F32),
        compiler_params=_cparams("parallel"),
        name="mlstm_out",
    )(x, hfT, hbT, oT, hg, w)


def _b_proj_kernel(x_ref, g_ref, w_ref, *rest):
    o_refs, xs_ref = rest[:-1], rest[-1]
    xn = _rms(x_ref[...], g_ref[...])
    nblk = D_MODEL // 128
    lhs = {1: xn.astype(BF16)}
    for c in range(nblk):
        xs_ref[c] = xn[:, c * 128:(c + 1) * 128]
    for dil in sorted({d for _, d in B_GROUPS if d > 1}):
        n = ROW_TILE // dil
        pieces = [jnp.concatenate([xs_ref[c, pl.ds(r, n, stride=dil), :] for c in range(nblk)],
                                  axis=-1).astype(BF16) for r in range(dil)]
        lhs[dil] = jnp.concatenate(pieces, axis=0)
    for j, o_ref in enumerate(o_refs):
        dil = B_GROUPS[j // 3][1]
        y = _dot(lhs[dil], w_ref[:, j * B_WIDTH:(j + 1) * B_WIDTH])
        if j % 3 == 0:
            y = y * (B_DH ** -0.5 * LOG2E)
        y = y.astype(BF16)
        n = ROW_TILE // dil
        for r in range(dil):
            o_ref[:, r * B_WIDTH:(r + 1) * B_WIDTH] = y[r * n:(r + 1) * n, :]


def _b_proj(x, g, w):
    m, d = x.shape
    dils = [dil for _, dil in B_GROUPS for _ in range(3)]
    return pl.pallas_call(
        _b_proj_kernel,
        grid=(m // ROW_TILE,),
        in_specs=[pl.BlockSpec((ROW_TILE, d), lambda i: (i, 0)), _resident((1, d)),
                  _resident(w.shape)],
        out_specs=[pl.BlockSpec((ROW_TILE // dil, dil * B_WIDTH), lambda i: (i, 0)) for dil in dils],
        out_shape=[jax.ShapeDtypeStruct((m // dil, dil * B_WIDTH), BF16) for dil in dils],
        scratch_shapes=[pltpu.VMEM((B_WIDTH // 128, ROW_TILE, 128), F32)],
        compiler_params=_cparams("parallel"),
        name="dilated_proj",
    )(x, g, w)


def _b_attn_kernel(q_ref, kp_ref, kc_ref, kn_ref, vp_ref, vc_ref, vn_ref,
                   bias_first_ref, bias_mid_ref, bias_last_ref, o_ref, stat_ref):
    k_all = jnp.concatenate([kp_ref[...], kc_ref[...], kn_ref[...]], axis=0)
    v_all = jnp.concatenate([vp_ref[...], vc_ref[...], vn_ref[...]], axis=0)
    lane = lax.broadcasted_iota(jnp.int32, (B_TQ, 128), 1)
    low = lane < B_DH
    stat_slot = lane // (STAT_LANES // 2)
    for sb in range(B_SUB):
        bias_ref = bias_first_ref if sb == 0 else (bias_last_ref if sb == B_SUB - 1 else bias_mid_ref)
        q0 = sb * B_TQ
        stat_row = jnp.zeros((B_TQ, 128), F32)
        for hp in range(B_HEADS // 2):
            cols = slice(hp * 128, (hp + 1) * 128)
            q2 = q_ref[q0:q0 + B_TQ, cols]
            k2 = k_all[q0:q0 + B_TK, cols]
            v2 = v_all[q0:q0 + B_TK, cols]
            zero = jnp.zeros_like(q2)
            outs = []
            for par, qsel in ((0, jnp.where(low, q2, zero)), (1, jnp.where(low, zero, q2))):
                hd = 2 * hp + par
                s = _dot_nt(qsel, k2) + bias_ref[hd]
                mx = jnp.max(s, axis=-1, keepdims=True)
                e = jnp.exp2(s - mx)
                den = jnp.sum(e, axis=-1, keepdims=True)
                outs.append(_dot(e.astype(BF16), v2))
                stat_row = jnp.where(stat_slot == 2 * hd, mx,
                                     jnp.where(stat_slot == 2 * hd + 1, den, stat_row))
            o_ref[q0:q0 + B_TQ, cols] = jnp.where(low, outs[0], outs[1]).astype(BF16)
        stat_ref[q0:q0 + B_TQ, :] = stat_row


def _b_attn(q, k, v, bias, bsz, seq, dil):
    n = seq // dil
    rows = B_SUB * B_TQ
    nq = n // rows
    w = B_WIDTH
    q = q.reshape(bsz, n, dil * w)
    k = k.reshape(bsz, n, dil * w)
    v = v.reshape(bsz, n, dil * w)
    hb = rows // B_HALF
    last_half = n // B_HALF - 1
    cur = pl.BlockSpec((None, rows, w), lambda b_, r, i: (b_, i, r))
    prev = pl.BlockSpec((None, B_HALF, w), lambda b_, r, i: (b_, jnp.maximum(i * hb - 1, 0), r))
    nxt = pl.BlockSpec((None, B_HALF, w), lambda b_, r, i: (b_, jnp.minimum((i + 1) * hb, last_half), r))
    at_start = lambda i: jnp.where(i == 0, 1, 0)
    at_end = lambda i: jnp.where(i == nq - 1, 2, 0)
    first_variant = (lambda i: at_start(i) + at_end(i)) if B_SUB == 1 else at_start
    bias_blk = (None, B_HEADS, B_TQ, B_TK)
    bias_first = pl.BlockSpec(bias_blk, lambda b_, r, i: (first_variant(i), 0, 0, 0))
    bias_mid = pl.BlockSpec(bias_blk, lambda b_, r, i: (0, 0, 0, 0), pipeline_mode=pl.Buffered(1))
    bias_last = pl.BlockSpec(bias_blk, lambda b_, r, i: (at_end(i), 0, 0, 0))
    out, stats = pl.pallas_call(
        _b_attn_kernel,
        grid=(bsz, dil, nq),
        in_specs=[cur, prev, cur, nxt, prev, cur, nxt, bias_first, bias_mid, bias_last],
        out_specs=[cur, pl.BlockSpec((None, rows, 128), lambda b_, r, i: (b_, i, r))],
        out_shape=[jax.ShapeDtypeStruct((bsz, n, dil * w), BF16),
                   jax.ShapeDtypeStruct((bsz, n, dil * 128), F32)],
        compiler_params=_cparams("parallel", "parallel", "arbitrary"),
        name=f"dilated_attn_d{dil}",
    )(q, k, k, k, v, v, v, bias, bias, bias)
    return out.reshape(bsz * n, dil * w), stats.reshape(bsz * n, dil * 128)


def _attn_bias(dil):
    slopes = jnp.exp2(-8.0 * (jnp.arange(B_HEADS, dtype=F32) + 1.0) / B_HEADS)
    tq = jnp.arange(B_TQ)[:, None]
    col = jnp.arange(B_TK)[None, :]
    rel = jnp.abs(col - B_HALF - tq)
    base = -slopes[:, None, None] * (rel * dil).astype(F32)[None] * LOG2E
    band = (rel <= B_HALF)[None]
    first = (col >= B_HALF)[None]
    lastv = (col < B_TQ + B_HALF)[None]
    variants = []
    for need_first, need_last in ((False, False), (True, False), (False, True), (True, True)):
        ok = band
        if need_first:
            ok = ok & first
        if need_last:
            ok = ok & lastv
        variants.append(jnp.where(ok, base, NEG_BIG))
    return jnp.stack(variants, axis=0)


def _b_out_kernel(x_ref, o0_ref, o1_ref, o2_ref, s0_ref, s1_ref, s2_ref, e_ref, w_ref, y_ref,
                  os_ref, ss_ref):
    for gi, (s_ref, (_, dil)) in enumerate(zip((s0_ref, s1_ref, s2_ref), B_GROUPS)):
        for r in range(dil):
            ss_ref[gi, pl.ds(r, ROW_TILE // dil, stride=dil), :] = s_ref[:, r * 128:(r + 1) * 128]
    half = STAT_LANES // 2
    mxs = [ss_ref[gi] for gi in range(len(B_GROUPS))]
    dens = [pltpu.roll(m, 128 - half, axis=1) for m in mxs]
    top = jnp.maximum(jnp.maximum(mxs[0], mxs[1]), mxs[2])
    ts = [jnp.exp2(m - top) for m in mxs]
    total = dens[0] * ts[0] + dens[1] * ts[1] + dens[2] * ts[2]
    lane = lax.broadcasted_iota(jnp.int32, total.shape, 1)
    picked = lane % STAT_LANES == 0
    acc = None
    for gi, (t, o_ref, (_, dil)) in enumerate(zip(ts, (o0_ref, o1_ref, o2_ref), B_GROUPS)):
        weight = jnp.where(picked, t / total, 0.0)
        hi = weight.astype(BF16)
        lo = (weight - hi.astype(F32)).astype(BF16)
        wide = _dot(jnp.concatenate([hi, lo], axis=1), e_ref[...])
        nblk = B_WIDTH // 128
        for r in range(dil):
            for c in range(nblk):
                lo = r * B_WIDTH + c * 128
                os_ref[gi, c, pl.ds(r, ROW_TILE // dil, stride=dil), :] = (
                    o_ref[:, lo:lo + 128].astype(F32))
        term = wide * jnp.concatenate([os_ref[gi, c] for c in range(nblk)], axis=-1)
        acc = term if acc is None else acc + term
    y_ref[...] = x_ref[...] + _dot(acc.astype(BF16), w_ref[...])


def _b_out(x, outs, stats, w):
    m, d = x.shape
    head_of_col = jnp.arange(B_WIDTH) // B_DH
    expand = (jnp.arange(128)[:, None] == head_of_col[None, :] * STAT_LANES).astype(BF16)
    expand = jnp.concatenate([expand, expand], axis=0)
    row = pl.BlockSpec((ROW_TILE, d), lambda i: (i, 0))
    grp = lambda lanes: [pl.BlockSpec((ROW_TILE // dil, dil * lanes), lambda i: (i, 0))
                         for _, dil in B_GROUPS]
    ng = len(B_GROUPS)
    return pl.pallas_call(
        _b_out_kernel,
        grid=(m // ROW_TILE,),
        in_specs=[row, *grp(B_WIDTH), *grp(128), _resident((256, B_WIDTH)),
                  _resident((B_WIDTH, d))],
        out_specs=row,
        out_shape=jax.ShapeDtypeStruct((m, d), F32),
        scratch_shapes=[pltpu.VMEM((ng, B_WIDTH // 128, ROW_TILE, 128), F32),
                        pltpu.VMEM((ng, ROW_TILE, 128), F32)],
        compiler_params=_cparams("parallel"),
        name="dilated_out",
    )(x, *outs, *stats, expand, w)


def _mlstm_mixer(x, g, w_in, conv_w, conv_b, gate_b, head_g, w_out, bsz, seq):
    w_qk = w_in[:, :A_QK].astype(BF16)
    w_voT = w_in[:, A_QK:A_QK + 2 * A_V].T.astype(BF16)
    wg = w_in[:, A_QK + 2 * A_V:]
    H = A_HEADS
    pick = lambda a, s: jnp.concatenate([a[..., s * H:(s + 1) * H], a[..., (s + 2) * H:(s + 3) * H]], axis=-1)
    w_li, w_lf = pick(wg, 0).astype(BF16), pick(wg, 1).astype(BF16)
    gb = gate_b.astype(F32)[None, :]
    b_li, b_lf = pick(gb, 0), pick(gb, 1)
    qk, vT, oT, li, lf, liT, lfT = _a_proj(x, g, w_qk, w_voT, w_li, w_lf, b_li, b_lf)
    k, qT = _conv(qk, conv_w.astype(F32), conv_b.astype(F32)[None, :], bsz, seq)
    hfT = _mlstm(k, qT, vT, li, lf, liT, lfT, bsz, seq, reverse=False)
    hbT = _mlstm(k, qT, vT, li, lf, liT, lfT, bsz, seq, reverse=True)
    hg = jnp.broadcast_to(head_g.astype(F32)[:, None], (A_V, ROW_TILE))
    return _a_out(x, hfT, hbT, oT, hg, w_out.astype(BF16))


def _dilated_mixer(x, g, w_in, w_out, bsz, seq):
    slabs = _b_proj(x, g, w_in.astype(BF16))
    outs, stats = [], []
    for gi, (_, dil) in enumerate(B_GROUPS):
        o, l = _b_attn(slabs[3 * gi], slabs[3 * gi + 1], slabs[3 * gi + 2], _attn_bias(dil),
                       bsz, seq, dil)
        outs.append(o)
        stats.append(l)
    return _b_out(x, outs, stats, w_out.astype(BF16))


def kernel(x, norm_ffn1, ffn1_gate, ffn1_up, ffn1_down, norm_mix, a_w_in, a_conv_w, a_conv_b,
           a_gate_b, a_head_g, a_w_out, b_w_in, b_w_out, norm_ffn2, ffn2_gate, ffn2_up,
           ffn2_down, norm_final):
    bsz, seq, d = x.shape
    h = x.reshape(bsz * seq, d)
    row = lambda a: a.astype(F32)[None, :]
    g_final = row(norm_final)
    for i in range(DEPTH):
        h = _ffn(h, row(norm_ffn1[i]), ffn1_gate[i].astype(F32), ffn1_up[i].astype(F32),
                 ffn1_down[i].astype(F32), g_final, False)
        j = i // 2
        if i % 2 == 0:
            h = _mlstm_mixer(h, row(norm_mix[i]), a_w_in[j], a_conv_w[j], a_conv_b[j],
                             a_gate_b[j], a_head_g[j], a_w_out[j], bsz, seq)
        else:
            h = _dilated_mixer(h, row(norm_mix[i]), b_w_in[j], b_w_out[j], bsz, seq)
        h = _ffn(h, row(norm_ffn2[i]), ffn2_gate[i].astype(F32), ffn2_up[i].astype(F32),
                 ffn2_down[i].astype(F32), g_final, i == DEPTH - 1)
    return h.reshape(bsz, seq, d)
```

```python
import functools

import jax
import jax.numpy as jnp
from jax import lax
from jax.experimental import pallas as pl
from jax.experimental.pallas import tpu as pltpu

F32 = jnp.float32
BF16 = jnp.bfloat16

D_MODEL = 1024
DEPTH = 4
D_FF = 2816
RMS_EPS = 1e-6

A_HEADS = 4
A_DQK = 128
A_DV = 256
A_CHUNK = 128
A_CONV = 5
A_QK = 2 * A_HEADS * A_DQK
A_V = A_HEADS * A_DV
A_PAD = 16

B_GROUPS = ((128, 1), (512, 4), (2048, 16))
B_HEADS = 16
B_DH = 64
B_WIDTH = B_HEADS * B_DH
B_HALF = 64
B_TQ = 128
B_TK = B_TQ + 2 * B_HALF
B_SUB = 8
LOG2E = 1.4426950408889634
STAT_LANES = 128 // B_HEADS
NEG_BIG = -1e30

ROW_TILE = 512
W_STAGE_ROWS = 128
A_STEP_CHUNKS = 4
VMEM_LIMIT = 56 * 1024 * 1024


def _cparams(*sem):
    return pltpu.CompilerParams(dimension_semantics=sem, vmem_limit_bytes=VMEM_LIMIT)


def _resident(shape):
    nd = len(shape)
    return pl.BlockSpec(shape, lambda *_: (0,) * nd, pipeline_mode=pl.Buffered(1))


def _rms(x, g):
    ms = jnp.mean(x * x, axis=-1, keepdims=True)
    return x * lax.rsqrt(ms + RMS_EPS) * g


def _dot(a, b):
    return jnp.dot(a, b, preferred_element_type=F32)


def _dot_nt(a, b):
    return lax.dot_general(a, b, (((1,), (1,)), ((), ())), preferred_element_type=F32)


def _dot_tn(a, b):
    return lax.dot_general(a, b, (((0,), (0,)), ((), ())), preferred_element_type=F32)


def _bdot(a, b):
    return lax.dot_general(a, b, (((2,), (1,)), ((0,), (0,))), preferred_element_type=F32)


def _split3(x):
    hi = x.astype(BF16)
    r1 = x - hi.astype(F32)
    mid = r1.astype(BF16)
    lo = (r1 - mid.astype(F32)).astype(BF16)
    return hi, mid, lo


def _stage_copy(src_hbm, layer, stage_ref, sem, chunk, c):
    slot = c % 2
    return pltpu.make_async_copy(src_hbm.at[layer, pl.ds(c * chunk, chunk), :],
                                 stage_ref.at[slot], sem.at[slot])


def _load_as_bf16(src_hbm, layer, dst_ref, stage_ref, sem):
    chunk = stage_ref.shape[1]
    n = src_hbm.shape[1] // chunk
    _stage_copy(src_hbm, layer, stage_ref, sem, chunk, 0).start()
    for c in range(n):
        if c + 1 < n:
            _stage_copy(src_hbm, layer, stage_ref, sem, chunk, c + 1).start()
        _stage_copy(src_hbm, layer, stage_ref, sem, chunk, c).wait()
        dst_ref[c * chunk:(c + 1) * chunk, :] = stage_ref[c % 2].astype(BF16)


def _ffn_kernel(x_ref, g_ref, wg_hbm, wu_hbm, wd_hbm, gf_ref, o_ref,
                wg_ref, wu_ref, wd_ref, stage_in_ref, stage_out_ref, sem, *, layer, final_norm):
    @pl.when(pl.program_id(0) == 0)
    def _():
        _load_as_bf16(wg_hbm, layer, wg_ref, stage_in_ref, sem)
        _load_as_bf16(wu_hbm, layer, wu_ref, stage_in_ref, sem)
        _load_as_bf16(wd_hbm, layer, wd_ref, stage_out_ref, sem)

    x = x_ref[...]
    xn = _rms(x, g_ref[...]).astype(BF16)
    gate = _dot(xn, wg_ref[...])
    up = _dot(xn, wu_ref[...])
    act = (gate * jax.nn.sigmoid(gate) * up).astype(BF16)
    y = x + 0.5 * _dot(act, wd_ref[...])
    if final_norm:
        y = _rms(y, gf_ref[...])
    o_ref[...] = y


def _ffn(x, g, wg, wu, wd, layer, g_final, final_norm):
    m, d = x.shape
    f = wg.shape[2]
    row = pl.BlockSpec((ROW_TILE, d), lambda i: (i, 0))
    hbm = pl.BlockSpec(memory_space=pltpu.MemorySpace.HBM)
    return pl.pallas_call(
        functools.partial(_ffn_kernel, layer=layer, final_norm=final_norm),
        grid=(m // ROW_TILE,),
        in_specs=[row, _resident((1, d)), hbm, hbm, hbm, _resident((1, d))],
        out_specs=row,
        out_shape=jax.ShapeDtypeStruct((m, d), F32),
        scratch_shapes=[pltpu.VMEM((d, f), BF16), pltpu.VMEM((d, f), BF16), pltpu.VMEM((f, d), BF16),
                        pltpu.VMEM((2, W_STAGE_ROWS, f), F32), pltpu.VMEM((2, W_STAGE_ROWS, d), F32),
                        pltpu.SemaphoreType.DMA((2,))],
        compiler_params=_cparams("arbitrary"),
        name="ffn",
    )(x, g, wg, wu, wd, g_final)


A_HALO = 8


def _a_proj_kernel(x_ref, xp_ref, xn_ref, g_ref, wqk_ref, wvoT_ref, wli_ref, wlf_ref, wliT_ref,
                   wlfT_ref, bli_ref, blf_ref, bliT_ref, blfT_ref, cw_ref, cb_ref,
                   k_ref, qT_ref, vT_ref, oT_ref, li_ref, lf_ref, liT_ref, lfT_ref, *, n_steps):
    i = pl.program_id(1)
    g = g_ref[...]
    xn = _rms(x_ref[...], g)
    before = _rms(xp_ref[...], g) * (i > 0).astype(F32)
    after = _rms(xn_ref[...], g) * (i < n_steps - 1).astype(F32)
    ext = jnp.concatenate([before, xn, after], axis=0).astype(BF16)
    xb = xn.astype(BF16)
    rows = xn.shape[0]
    total = rows + 2 * A_HALO

    qk = _dot(ext, wqk_ref[...])
    half = A_CONV // 2
    acc = qk[A_HALO:A_HALO + rows, :] * cw_ref[half:half + 1, :] + cb_ref[...]
    for s in range(1, half + 1):
        back = pltpu.roll(qk, s, axis=0)[A_HALO:A_HALO + rows, :]
        fwd = pltpu.roll(qk, total - s, axis=0)[A_HALO:A_HALO + rows, :]
        acc = acc + back * cw_ref[half - s:half - s + 1, :] + fwd * cw_ref[half + s:half + s + 1, :]
    y = acc * jax.nn.sigmoid(acc)
    hq = A_QK // 2
    k_ref[...] = y[:, hq:].astype(BF16)
    qT_ref[...] = (y[:, :hq] * (A_DQK ** -0.5)).T.astype(BF16)

    vT_ref[...] = _dot_nt(wvoT_ref[0:A_V, :], xb).astype(BF16)
    oT_ref[...] = _dot_nt(wvoT_ref[A_V:2 * A_V, :], xb)
    li_ref[...] = _dot(xb, wli_ref[...]) + bli_ref[...]
    lf_ref[...] = jax.nn.log_sigmoid(_dot(xb, wlf_ref[...]) + blf_ref[...])
    liT_ref[...] = _dot_nt(wliT_ref[...], xb) + bliT_ref[...]
    lfT_ref[...] = jax.nn.log_sigmoid(_dot_nt(wlfT_ref[...], xb) + blfT_ref[...])


def _a_proj(x, g, w_qk, w_voT, w_li, w_lf, b_li, b_lf, conv_w, conv_b, bsz, seq):
    m, d = x.shape
    ng = w_li.shape[1]
    hq = A_QK // 2
    n_steps = seq // ROW_TILE
    rh = ROW_TILE // A_HALO
    x3 = x.reshape(bsz, seq, d)
    flat = lambda b_, i: b_ * n_steps + i
    tok = lambda n: pl.BlockSpec((None, ROW_TILE, n), lambda b_, i: (b_, i, 0))
    feat = lambda n: pl.BlockSpec((n, ROW_TILE), lambda b_, i: (0, flat(b_, i)))
    return pl.pallas_call(
        functools.partial(_a_proj_kernel, n_steps=n_steps),
        grid=(bsz, n_steps),
        in_specs=[tok(d),
                  pl.BlockSpec((None, A_HALO, d), lambda b_, i: (b_, jnp.maximum(i * rh - 1, 0), 0)),
                  pl.BlockSpec((None, A_HALO, d),
                               lambda b_, i: (b_, jnp.minimum((i + 1) * rh, seq // A_HALO - 1), 0)),
                  _resident((1, d)), _resident(w_qk.shape), _resident(w_voT.shape),
                  _resident((d, ng)), _resident((d, ng)), _resident((ng, d)), _resident((ng, d)),
                  _resident((1, ng)), _resident((1, ng)), _resident((ng, 1)), _resident((ng, 1)),
                  _resident((A_CONV, A_QK)), _resident((1, A_QK))],
        out_specs=[tok(hq), pl.BlockSpec((None, hq, ROW_TILE), lambda b_, i: (b_, 0, i)),
                   feat(A_V), feat(A_V), tok(ng), tok(ng), feat(ng), feat(ng)],
        out_shape=[jax.ShapeDtypeStruct((bsz, seq, hq), BF16),
                   jax.ShapeDtypeStruct((bsz, hq, seq), BF16),
                   jax.ShapeDtypeStruct((A_V, m), BF16),
                   jax.ShapeDtypeStruct((A_V, m), F32),
                   jax.ShapeDtypeStruct((bsz, seq, ng), F32),
                   jax.ShapeDtypeStruct((bsz, seq, ng), F32),
                   jax.ShapeDtypeStruct((ng, m), F32),
                   jax.ShapeDtypeStruct((ng, m), F32)],
        compiler_params=_cparams("parallel", "parallel"),
        name="mlstm_proj",
    )(x3, x3, x3, g, w_qk, w_voT, w_li, w_lf, w_li.T, w_lf.T, b_li, b_lf, b_li.T, b_lf.T,
      conv_w, conv_b)


def _mlstm_kernel(qT_ref, k_ref, vT_ref, li_ref, lf_ref, liT_ref, lfT_ref, hT_ref,
                  cT_ref, n_ref, m_ref, *, reverse):
    L = A_CHUNK
    lo = 4 if reverse else 0

    @pl.when(pl.program_id(1) == 0)
    def _():
        cT_ref[...] = jnp.zeros_like(cT_ref)
        n_ref[...] = jnp.zeros_like(n_ref)
        m_ref[...] = jnp.zeros_like(m_ref)

    row_i = lax.broadcasted_iota(jnp.int32, (L, L), 0)
    col_i = lax.broadcasted_iota(jnp.int32, (L, L), 1)
    seen = (col_i >= row_i) if reverse else (col_i <= row_i)
    seen_t = (row_i >= col_i) if reverse else (row_i <= col_i)
    seen_bf = seen.astype(F32).astype(BF16)
    last = 0 if reverse else L - 1

    H = A_HEADS
    c_state = cT_ref[...]
    n_state = n_ref[...]
    m_state = m_ref[...]

    order = range(A_STEP_CHUNKS - 1, -1, -1) if reverse else range(A_STEP_CHUNKS)
    for j in order:
        r0 = j * L
        li = li_ref[r0:r0 + L, :]
        lf = lf_ref[r0:r0 + L, :]
        liT = liT_ref[:, r0:r0 + L]
        lfT = lfT_ref[:, r0:r0 + L]
        b_col = sum(_dot(seen_bf, p) for p in _split3(lf))
        b_row = sum(_dot_nt(p, seen_bf) for p in _split3(lfT))
        r_col = li - b_col
        b_tot = b_row[:, last:last + 1]
        a_row = b_tot - b_row + liT
        m_loc = jnp.max(a_row, axis=1, keepdims=True)
        w_row = jnp.exp(a_row - m_loc)
        g_rows = b_row + m_state
        m_new = jnp.maximum(b_tot + m_state, m_loc)
        s_prev = jnp.exp(b_tot + m_state - m_new)
        s_loc = jnp.exp(m_loc - m_new)
        m_state = m_new

        heads = lambda rows: rows[lo:lo + H][:, None, :]
        qT = qT_ref[:, r0:r0 + L].reshape(H, A_DQK, L)
        vT = vT_ref[:, r0:r0 + L].reshape(H, A_DV, L)
        k = jnp.stack([k_ref[r0:r0 + L, h * A_DQK:(h + 1) * A_DQK] for h in range(H)])
        k_aug = jnp.concatenate(
            [k, jnp.broadcast_to(n_state, (H, A_PAD, A_DQK)).astype(BF16)], axis=1)
        sq = _bdot(k_aug, qT)
        qn = sq[:, L:L + 1, :]
        d = jnp.stack([r_col[:, lo + h:lo + h + 1] + b_row[lo + h:lo + h + 1, :]
                       for h in range(H)])
        d = jnp.where(seen_t[None], d, -jnp.inf)
        g = heads(g_rows)
        m_t = jnp.maximum(g, jnp.max(d, axis=1, keepdims=True))
        p = jnp.exp(d - m_t) * sq[:, :L, :]
        s_inter = jnp.exp(g - m_t)
        den = s_inter * qn + jnp.sum(p, axis=1, keepdims=True)
        inv = 1.0 / jnp.maximum(jnp.abs(den), jnp.exp(-m_t))
        rhs = jnp.concatenate([(qT.astype(F32) * (s_inter * inv)).astype(BF16),
                               (p * inv).astype(BF16)], axis=1)
        lhs = jnp.concatenate([c_state.astype(BF16), vT], axis=2)
        hT_ref[:, r0:r0 + L] = _bdot(lhs, rhs).reshape(A_V, L).astype(BF16)

        w = heads(w_row)
        vw_aug = jnp.concatenate([(vT.astype(F32) * w).astype(BF16),
                                  jnp.broadcast_to(w, (H, A_PAD, L)).astype(BF16)], axis=1)
        upd = _bdot(vw_aug, k)
        sp = heads(s_prev)
        sl = heads(s_loc)
        c_state = sp * c_state + sl * upd[:, :A_DV, :]
        n_state = sp * n_state + sl * upd[:, A_DV:A_DV + 1, :]

    cT_ref[...] = c_state
    n_ref[...] = n_state
    m_ref[...] = m_state


def _mlstm(k, qT, vT, li, lf, liT, lfT, bsz, seq, reverse):
    tl = A_STEP_CHUNKS * A_CHUNK
    ns = seq // tl
    ng = li.shape[-1]
    half = A_QK // 2
    blk = (lambda i: ns - 1 - i) if reverse else (lambda i: i)
    gate = pl.BlockSpec((None, tl, ng), lambda b_, i: (b_, blk(i), 0))
    feat = lambda n: pl.BlockSpec((n, tl), lambda b_, i: (0, b_ * ns + blk(i)))
    return pl.pallas_call(
        functools.partial(_mlstm_kernel, reverse=reverse),
        grid=(bsz, ns),
        in_specs=[pl.BlockSpec((None, half, tl), lambda b_, i: (b_, 0, blk(i))),
                  pl.BlockSpec((None, tl, half), lambda b_, i: (b_, blk(i), 0)),
                  feat(A_V), gate, gate, feat(ng), feat(ng)],
        out_specs=feat(A_V),
        out_shape=jax.ShapeDtypeStruct((A_V, bsz * seq), BF16),
        scratch_shapes=[pltpu.VMEM((A_HEADS, A_DV, A_DQK), F32),
                        pltpu.VMEM((A_HEADS, 1, A_DQK), F32),
                        pltpu.VMEM((ng, A_CHUNK), F32)],
        compiler_params=_cparams("parallel", "arbitrary"),
        name="mlstm_bwd" if reverse else "mlstm_fwd",
    )(qT, k, vT, li, lf, liT, lfT)


def _a_out_kernel(x_ref, hfT_ref, hbT_ref, oT_ref, hg_ref, w_ref, y_ref):
    yT = hfT_ref[...].astype(F32) + hbT_ref[...].astype(F32)
    parts = []
    for h in range(A_HEADS):
        yh = yT[h * A_DV:(h + 1) * A_DV, :]
        ms = jnp.mean(yh * yh, axis=0, keepdims=True)
        parts.append(yh * lax.rsqrt(ms + RMS_EPS))
    ynT = jnp.concatenate(parts, axis=0)
    zT = (ynT * hg_ref[...] * jax.nn.sigmoid(oT_ref[...])).astype(BF16)
    y_ref[...] = x_ref[...] + _dot_tn(zT, w_ref[...])


def _a_out(x, hfT, hbT, oT, hg, w):
    m, d = x.shape
    row = pl.BlockSpec((ROW_TILE, d), lambda i: (i, 0))
    col = pl.BlockSpec((A_V, ROW_TILE), lambda i: (0, i))
    return pl.pallas_call(
        _a_out_kernel,
        grid=(m // ROW_TILE,),
        in_specs=[row, col, col, col, _resident((A_V, ROW_TILE)), _resident((A_V, d))],
        out_specs=row,
        out_shape=jax.ShapeDtypeStruct((m, d), F32),
        compiler_params=_cparams("parallel"),
        name="mlstm_out",
    )(x, hfT, hbT, oT, hg, w)


def _b_proj_kernel(x_ref, g_ref, w_ref, *rest):
    o_refs, xs_ref = rest[:-1], rest[-1]
    xn = _rms(x_ref[...], g_ref[...])
    nblk = D_MODEL // 128
    lhs = {1: xn.astype(BF16)}
    for c in range(nblk):
        xs_ref[c] = xn[:, c * 128:(c + 1) * 128]
    for dil in sorted({d for _, d in B_GROUPS if d > 1}):
        n = ROW_TILE // dil
        pieces = [jnp.concatenate([xs_ref[c, pl.ds(r, n, stride=dil), :] for c in range(nblk)],
                                  axis=-1).astype(BF16) for r in range(dil)]
        lhs[dil] = jnp.concatenate(pieces, axis=0)
    for j, o_ref in enumerate(o_refs):
        dil = B_GROUPS[j // 3][1]
        y = _dot(lhs[dil], w_ref[:, j * B_WIDTH:(j + 1) * B_WIDTH])
        if j % 3 == 0:
            y = y * (B_DH ** -0.5 * LOG2E)
        y = y.astype(BF16)
        n = ROW_TILE // dil
        for r in range(dil):
            o_ref[:, r * B_WIDTH:(r + 1) * B_WIDTH] = y[r * n:(r + 1) * n, :]


def _b_proj(x, g, w):
    m, d = x.shape
    dils = [dil for _, dil in B_GROUPS for _ in range(3)]
    return pl.pallas_call(
        _b_proj_kernel,
        grid=(m // ROW_TILE,),
        in_specs=[pl.BlockSpec((ROW_TILE, d), lambda i: (i, 0)), _resident((1, d)),
                  _resident(w.shape)],
        out_specs=[pl.BlockSpec((ROW_TILE // dil, dil * B_WIDTH), lambda i: (i, 0)) for dil in dils],
        out_shape=[jax.ShapeDtypeStruct((m // dil, dil * B_WIDTH), BF16) for dil in dils],
        scratch_shapes=[pltpu.VMEM((B_WIDTH // 128, ROW_TILE, 128), F32)],
        compiler_params=_cparams("parallel"),
        name="dilated_proj",
    )(x, g, w)


def _b_attn_kernel(q_ref, kp_ref, kc_ref, kn_ref, vp_ref, vc_ref, vn_ref,
                   bias_first_ref, bias_mid_ref, bias_last_ref, o_ref, stat_ref):
    k_all = jnp.concatenate([kp_ref[...], kc_ref[...], kn_ref[...]], axis=0)
    v_all = jnp.concatenate([vp_ref[...], vc_ref[...], vn_ref[...]], axis=0)
    lane = lax.broadcasted_iota(jnp.int32, (B_TQ, 128), 1)
    low = lane < B_DH
    stat_slot = lane // (STAT_LANES // 2)
    for sb in range(B_SUB):
        bias_ref = bias_first_ref if sb == 0 else (bias_last_ref if sb == B_SUB - 1 else bias_mid_ref)
        q0 = sb * B_TQ
        stat_row = jnp.zeros((B_TQ, 128), F32)
        for hp in range(B_HEADS // 2):
            cols = slice(hp * 128, (hp + 1) * 128)
            q2 = q_ref[q0:q0 + B_TQ, cols]
            k2 = k_all[q0:q0 + B_TK, cols]
            v2 = v_all[q0:q0 + B_TK, cols]
            zero = jnp.zeros_like(q2)
            outs = []
            for par, qsel in ((0, jnp.where(low, q2, zero)), (1, jnp.where(low, zero, q2))):
                hd = 2 * hp + par
                s = _dot_nt(qsel, k2) + bias_ref[hd]
                mx = jnp.max(s, axis=-1, keepdims=True)
                e = jnp.exp2(s - mx)
                den = jnp.sum(e, axis=-1, keepdims=True)
                outs.append(_dot(e.astype(BF16), v2))
                stat_row = jnp.where(stat_slot == 2 * hd, mx,
                                     jnp.where(stat_slot == 2 * hd + 1, den, stat_row))
            o_ref[q0:q0 + B_TQ, cols] = jnp.where(low, outs[0], outs[1]).astype(BF16)
        stat_ref[q0:q0 + B_TQ, :] = stat_row


def _b_attn(q, k, v, bias, bsz, seq, dil):
    n = seq // dil
    rows = B_SUB * B_TQ
    nq = n // rows
    w = B_WIDTH
    q = q.reshape(bsz, n, dil * w)
    k = k.reshape(bsz, n, dil * w)
    v = v.reshape(bsz, n, dil * w)
    hb = rows // B_HALF
    last_half = n // B_HALF - 1
    cur = pl.BlockSpec((None, rows, w), lambda b_, r, i: (b_, i, r))
    prev = pl.BlockSpec((None, B_HALF, w), lambda b_, r, i: (b_, jnp.maximum(i * hb - 1, 0), r))
    nxt = pl.BlockSpec((None, B_HALF, w), lambda b_, r, i: (b_, jnp.minimum((i + 1) * hb, last_half), r))
    at_start = lambda i: jnp.where(i == 0, 1, 0)
    at_end = lambda i: jnp.where(i == nq - 1, 2, 0)
    first_variant = (lambda i: at_start(i) + at_end(i)) if B_SUB == 1 else at_start
    bias_blk = (None, B_HEADS, B_TQ, B_TK)
    bias_first = pl.BlockSpec(bias_blk, lambda b_, r, i: (first_variant(i), 0, 0, 0))
    bias_mid = pl.BlockSpec(bias_blk, lambda b_, r, i: (0, 0, 0, 0), pipeline_mode=pl.Buffered(1))
    bias_last = pl.BlockSpec(bias_blk, lambda b_, r, i: (at_end(i), 0, 0, 0))
    out, stats = pl.pallas_call(
        _b_attn_kernel,
        grid=(bsz, dil, nq),
        in_specs=[cur, prev, cur, nxt, prev, cur, nxt, bias_first, bias_mid, bias_last],
        out_specs=[cur, pl.BlockSpec((None, rows, 128), lambda b_, r, i: (b_, i, r))],
        out_shape=[jax.ShapeDtypeStruct((bsz, n, dil * w), BF16),
                   jax.ShapeDtypeStruct((bsz, n, dil * 128), F32)],
        compiler_params=_cparams("parallel", "parallel", "arbitrary"),
        name=f"dilated_attn_d{dil}",
    )(q, k, k, k, v, v, v, bias, bias, bias)
    return out.reshape(bsz * n, dil * w), stats.reshape(bsz * n, dil * 128)


def _attn_bias(dil):
    slopes = jnp.exp2(-8.0 * (jnp.arange(B_HEADS, dtype=F32) + 1.0) / B_HEADS)
    tq = jnp.arange(B_TQ)[:, None]
    col = jnp.arange(B_TK)[None, :]
    rel = jnp.abs(col - B_HALF - tq)
    base = -slopes[:, None, None] * (rel * dil).astype(F32)[None] * LOG2E
    band = (rel <= B_HALF)[None]
    first = (col >= B_HALF)[None]
    lastv = (col < B_TQ + B_HALF)[None]
    variants = []
    for need_first, need_last in ((False, False), (True, False), (False, True), (True, True)):
        ok = band
        if need_first:
            ok = ok & first
        if need_last:
            ok = ok & lastv
        variants.append(jnp.where(ok, base, NEG_BIG))
    return jnp.stack(variants, axis=0)


def _b_out_kernel(x_ref, o0_ref, o1_ref, o2_ref, s0_ref, s1_ref, s2_ref, e_ref, w_ref, y_ref,
                  os_ref, ss_ref):
    for gi, (s_ref, (_, dil)) in enumerate(zip((s0_ref, s1_ref, s2_ref), B_GROUPS)):
        for r in range(dil):
            ss_ref[gi, pl.ds(r, ROW_TILE // dil, stride=dil), :] = s_ref[:, r * 128:(r + 1) * 128]
    half = STAT_LANES // 2
    mxs = [ss_ref[gi] for gi in range(len(B_GROUPS))]
    dens = [pltpu.roll(m, 128 - half, axis=1) for m in mxs]
    top = jnp.maximum(jnp.maximum(mxs[0], mxs[1]), mxs[2])
    ts = [jnp.exp2(m - top) for m in mxs]
    total = dens[0] * ts[0] + dens[1] * ts[1] + dens[2] * ts[2]
    lane = lax.broadcasted_iota(jnp.int32, total.shape, 1)
    picked = lane % STAT_LANES == 0
    acc = None
    for gi, (t, o_ref, (_, dil)) in enumerate(zip(ts, (o0_ref, o1_ref, o2_ref), B_GROUPS)):
        weight = jnp.where(picked, t / total, 0.0)
        hi = weight.astype(BF16)
        lo = (weight - hi.astype(F32)).astype(BF16)
        wide = _dot(jnp.concatenate([hi, lo], axis=1), e_ref[...])
        nblk = B_WIDTH // 128
        for r in range(dil):
            for c in range(nblk):
                lo = r * B_WIDTH + c * 128
                os_ref[gi, c, pl.ds(r, ROW_TILE // dil, stride=dil), :] = (
                    o_ref[:, lo:lo + 128].astype(F32))
        term = wide * jnp.concatenate([os_ref[gi, c] for c in range(nblk)], axis=-1)
        acc = term if acc is None else acc + term
    y_ref[...] = x_ref[...] + _dot(acc.astype(BF16), w_ref[...])


def _b_out(x, outs, stats, w):
    m, d = x.shape
    head_of_col = jnp.arange(B_WIDTH) // B_DH
    expand = (jnp.arange(128)[:, None] == head_of_col[None, :] * STAT_LANES).astype(BF16)
    expand = jnp.concatenate([expand, expand], axis=0)
    row = pl.BlockSpec((ROW_TILE, d), lambda i: (i, 0))
    grp = lambda lanes: [pl.BlockSpec((ROW_TILE // dil, dil * lanes), lambda i: (i, 0))
                         for _, dil in B_GROUPS]
    ng = len(B_GROUPS)
    return pl.pallas_call(
        _b_out_kernel,
        grid=(m // ROW_TILE,),
        in_specs=[row, *grp(B_WIDTH), *grp(128), _resident((256, B_WIDTH)),
                  _resident((B_WIDTH, d))],
        out_specs=row,
        out_shape=jax.ShapeDtypeStruct((m, d), F32),
        scratch_shapes=[pltpu.VMEM((ng, B_WIDTH // 128, ROW_TILE, 128), F32),
                        pltpu.VMEM((ng, ROW_TILE, 128), F32)],
        compiler_params=_cparams("parallel"),
        name="dilated_out",
    )(x, *outs, *stats, expand, w)


def _mlstm_mixer(x, g, w_in, conv_w, conv_b, gate_b, head_g, w_out, bsz, seq):
    w_qk = w_in[:, :A_QK].astype(BF16)
    w_voT = w_in[:, A_QK:A_QK + 2 * A_V].T.astype(BF16)
    wg = w_in[:, A_QK + 2 * A_V:]
    H = A_HEADS
    pick = lambda a, s: jnp.concatenate([a[..., s * H:(s + 1) * H], a[..., (s + 2) * H:(s + 3) * H]], axis=-1)
    w_li, w_lf = pick(wg, 0).astype(BF16), pick(wg, 1).astype(BF16)
    gb = gate_b.astype(F32)[None, :]
    b_li, b_lf = pick(gb, 0), pick(gb, 1)
    k, qT, vT, oT, li, lf, liT, lfT = _a_proj(x, g, w_qk, w_voT, w_li, w_lf, b_li, b_lf,
                                              conv_w.astype(F32), conv_b.astype(F32)[None, :], bsz, seq)
    hfT = _mlstm(k, qT, vT, li, lf, liT, lfT, bsz, seq, reverse=False)
    hbT = _mlstm(k, qT, vT, li, lf, liT, lfT, bsz, seq, reverse=True)
    hg = jnp.broadcast_to(head_g.astype(F32)[:, None], (A_V, ROW_TILE))
    return _a_out(x, hfT, hbT, oT, hg, w_out.astype(BF16))


def _dilated_mixer(x, g, w_in, w_out, bsz, seq):
    slabs = _b_proj(x, g, w_in.astype(BF16))
    outs, stats = [], []
    for gi, (_, dil) in enumerate(B_GROUPS):
        o, l = _b_attn(slabs[3 * gi], slabs[3 * gi + 1], slabs[3 * gi + 2], _attn_bias(dil),
                       bsz, seq, dil)
        outs.append(o)
        stats.append(l)
    return _b_out(x, outs, stats, w_out.astype(BF16))


def kernel(x, norm_ffn1, ffn1_gate, ffn1_up, ffn1_down, norm_mix, a_w_in, a_conv_w, a_conv_b,
           a_gate_b, a_head_g, a_w_out, b_w_in, b_w_out, norm_ffn2, ffn2_gate, ffn2_up,
           ffn2_down, norm_final):
    bsz, seq, d = x.shape
    h = x.reshape(bsz * seq, d)
    row = lambda a: a.astype(F32)[None, :]
    g_final = row(norm_final)
    for i in range(DEPTH):
        h = _ffn(h, row(norm_ffn1[i]), ffn1_gate, ffn1_up, ffn1_down, i, g_final, False)
        j = i // 2
        if i % 2 == 0:
            h = _mlstm_mixer(h, row(norm_mix[i]), a_w_in[j], a_conv_w[j], a_conv_b[j],
                             a_gate_b[j], a_head_g[j], a_w_out[j], bsz, seq)
        else:
            h = _dilated_mixer(h, row(norm_mix[i]), b_w_in[j], b_w_out[j], bsz, seq)
        h = _ffn(h, row(norm_ffn2[i]), ffn2_gate, ffn2_up, ffn2_down, i, g_final, i == DEPTH - 1)
    return h.reshape(bsz, seq, d)
```

```python
import functools

import jax
import jax.numpy as jnp
from jax import lax
from jax.experimental import pallas as pl
from jax.experimental.pallas import tpu as pltpu

F32 = jnp.float32
BF16 = jnp.bfloat16

D_MODEL = 1024
DEPTH = 4
D_FF = 2816
RMS_EPS = 1e-6

A_HEADS = 4
A_DQK = 128
A_DV = 256
A_CHUNK = 128
A_CONV = 5
A_QK = 2 * A_HEADS * A_DQK
A_V = A_HEADS * A_DV
A_PAD = 16
assert A_DQK == A_CHUNK

B_GROUPS = ((128, 1), (512, 4), (2048, 16))
B_HEADS = 16
B_DH = 64
B_WIDTH = B_HEADS * B_DH
B_HALF = 64
B_TQ = 128
B_TK = B_TQ + 2 * B_HALF
B_SUB = 8
LOG2E = 1.4426950408889634
STAT_LANES = 128 // B_HEADS
NEG_BIG = -1e30

ROW_TILE = 512
W_STAGE_ROWS = 128
A_STEP_CHUNKS = 8
VMEM_LIMIT = 56 * 1024 * 1024


def _cparams(*sem):
    return pltpu.CompilerParams(dimension_semantics=sem, vmem_limit_bytes=VMEM_LIMIT)


def _resident(shape):
    nd = len(shape)
    return pl.BlockSpec(shape, lambda *_: (0,) * nd, pipeline_mode=pl.Buffered(1))


def _rms(x, g):
    ms = jnp.mean(x * x, axis=-1, keepdims=True)
    return x * lax.rsqrt(ms + RMS_EPS) * g


def _dot(a, b):
    return jnp.dot(a, b, preferred_element_type=F32)


def _dot_nt(a, b):
    return lax.dot_general(a, b, (((1,), (1,)), ((), ())), preferred_element_type=F32)


def _dot_tn(a, b):
    return lax.dot_general(a, b, (((0,), (0,)), ((), ())), preferred_element_type=F32)


def _bdot(a, b):
    return lax.dot_general(a, b, (((2,), (1,)), ((0,), (0,))), preferred_element_type=F32)


def _split3(x):
    hi = x.astype(BF16)
    r1 = x - hi.astype(F32)
    mid = r1.astype(BF16)
    lo = (r1 - mid.astype(F32)).astype(BF16)
    return hi, mid, lo


def _stage_copy(src_hbm, layer, stage_ref, sem, chunk, c):
    slot = c % 2
    return pltpu.make_async_copy(src_hbm.at[layer, pl.ds(c * chunk, chunk), :],
                                 stage_ref.at[slot], sem.at[slot])


def _load_as_bf16(src_hbm, layer, dst_ref, stage_ref, sem):
    chunk = stage_ref.shape[1]
    n = src_hbm.shape[1] // chunk
    _stage_copy(src_hbm, layer, stage_ref, sem, chunk, 0).start()
    for c in range(n):
        if c + 1 < n:
            _stage_copy(src_hbm, layer, stage_ref, sem, chunk, c + 1).start()
        _stage_copy(src_hbm, layer, stage_ref, sem, chunk, c).wait()
        dst_ref[c * chunk:(c + 1) * chunk, :] = stage_ref[c % 2].astype(BF16)


def _ffn_kernel(x_ref, g_ref, wg_hbm, wu_hbm, wd_hbm, gf_ref, o_ref,
                wg_ref, wu_ref, wd_ref, stage_in_ref, stage_out_ref, sem, *, layer, final_norm):
    @pl.when(pl.program_id(0) == 0)
    def _():
        _load_as_bf16(wg_hbm, layer, wg_ref, stage_in_ref, sem)
        _load_as_bf16(wu_hbm, layer, wu_ref, stage_in_ref, sem)
        _load_as_bf16(wd_hbm, layer, wd_ref, stage_out_ref, sem)

    x = x_ref[...]
    xn = _rms(x, g_ref[...]).astype(BF16)
    gate = _dot(xn, wg_ref[...])
    up = _dot(xn, wu_ref[...])
    act = (gate * jax.nn.sigmoid(gate) * up).astype(BF16)
    y = x + 0.5 * _dot(act, wd_ref[...])
    if final_norm:
        y = _rms(y, gf_ref[...])
    o_ref[...] = y


def _ffn(x, g, wg, wu, wd, layer, g_final, final_norm):
    m, d = x.shape
    f = wg.shape[2]
    row = pl.BlockSpec((ROW_TILE, d), lambda i: (i, 0))
    hbm = pl.BlockSpec(memory_space=pltpu.MemorySpace.HBM)
    return pl.pallas_call(
        functools.partial(_ffn_kernel, layer=layer, final_norm=final_norm),
        grid=(m // ROW_TILE,),
        in_specs=[row, _resident((1, d)), hbm, hbm, hbm, _resident((1, d))],
        out_specs=row,
        out_shape=jax.ShapeDtypeStruct((m, d), F32),
        scratch_shapes=[pltpu.VMEM((d, f), BF16), pltpu.VMEM((d, f), BF16), pltpu.VMEM((f, d), BF16),
                        pltpu.VMEM((2, W_STAGE_ROWS, f), F32), pltpu.VMEM((2, W_STAGE_ROWS, d), F32),
                        pltpu.SemaphoreType.DMA((2,))],
        compiler_params=_cparams("arbitrary"),
        name="ffn",
    )(x, g, wg, wu, wd, g_final)


A_HALO = 8


def _a_proj_kernel(x_ref, xp_ref, xn_ref, g_ref, wqk_ref, wvoT_ref, wli_ref, wlf_ref, wliT_ref,
                   wlfT_ref, bli_ref, blf_ref, bliT_ref, blfT_ref, cw_ref, cb_ref,
                   k_ref, qT_ref, vT_ref, oT_ref, li_ref, lf_ref, liT_ref, lfT_ref, *, n_steps):
    i = pl.program_id(1)
    g = g_ref[...]
    xn = _rms(x_ref[...], g)
    before = _rms(xp_ref[...], g) * (i > 0).astype(F32)
    after = _rms(xn_ref[...], g) * (i < n_steps - 1).astype(F32)
    ext = jnp.concatenate([before, xn, after], axis=0).astype(BF16)
    xb = xn.astype(BF16)
    rows = xn.shape[0]
    total = rows + 2 * A_HALO

    qk = _dot(ext, wqk_ref[...])
    half = A_CONV // 2
    acc = qk[A_HALO:A_HALO + rows, :] * cw_ref[half:half + 1, :] + cb_ref[...]
    for s in range(1, half + 1):
        back = pltpu.roll(qk, s, axis=0)[A_HALO:A_HALO + rows, :]
        fwd = pltpu.roll(qk, total - s, axis=0)[A_HALO:A_HALO + rows, :]
        acc = acc + back * cw_ref[half - s:half - s + 1, :] + fwd * cw_ref[half + s:half + s + 1, :]
    y = acc * jax.nn.sigmoid(acc)
    hq = A_QK // 2
    k_ref[...] = y[:, hq:].astype(BF16)
    qT_ref[...] = (y[:, :hq] * (A_DQK ** -0.5)).T.astype(BF16)

    vT_ref[...] = _dot_nt(wvoT_ref[0:A_V, :], xb).astype(BF16)
    oT_ref[...] = _dot_nt(wvoT_ref[A_V:2 * A_V, :], xb)
    li_ref[...] = _dot(xb, wli_ref[...]) + bli_ref[...]
    lf_ref[...] = jax.nn.log_sigmoid(_dot(xb, wlf_ref[...]) + blf_ref[...])
    liT_ref[...] = _dot_nt(wliT_ref[...], xb) + bliT_ref[...]
    lfT_ref[...] = jax.nn.log_sigmoid(_dot_nt(wlfT_ref[...], xb) + blfT_ref[...])


def _a_proj(x, g, w_qk, w_voT, w_li, w_lf, b_li, b_lf, conv_w, conv_b, bsz, seq):
    m, d = x.shape
    ng = w_li.shape[1]
    hq = A_QK // 2
    n_steps = seq // ROW_TILE
    rh = ROW_TILE // A_HALO
    x3 = x.reshape(bsz, seq, d)
    flat = lambda b_, i: b_ * n_steps + i
    tok = lambda n: pl.BlockSpec((None, ROW_TILE, n), lambda b_, i: (b_, i, 0))
    feat = lambda n: pl.BlockSpec((n, ROW_TILE), lambda b_, i: (0, flat(b_, i)))
    return pl.pallas_call(
        functools.partial(_a_proj_kernel, n_steps=n_steps),
        grid=(bsz, n_steps),
        in_specs=[tok(d),
                  pl.BlockSpec((None, A_HALO, d), lambda b_, i: (b_, jnp.maximum(i * rh - 1, 0), 0)),
                  pl.BlockSpec((None, A_HALO, d),
                               lambda b_, i: (b_, jnp.minimum((i + 1) * rh, seq // A_HALO - 1), 0)),
                  _resident((1, d)), _resident(w_qk.shape), _resident(w_voT.shape),
                  _resident((d, ng)), _resident((d, ng)), _resident((ng, d)), _resident((ng, d)),
                  _resident((1, ng)), _resident((1, ng)), _resident((ng, 1)), _resident((ng, 1)),
                  _resident((A_CONV, A_QK)), _resident((1, A_QK))],
        out_specs=[tok(hq), pl.BlockSpec((None, hq, ROW_TILE), lambda b_, i: (b_, 0, i)),
                   feat(A_V), feat(A_V), tok(ng), tok(ng), feat(ng), feat(ng)],
        out_shape=[jax.ShapeDtypeStruct((bsz, seq, hq), BF16),
                   jax.ShapeDtypeStruct((bsz, hq, seq), BF16),
                   jax.ShapeDtypeStruct((A_V, m), BF16),
                   jax.ShapeDtypeStruct((A_V, m), F32),
                   jax.ShapeDtypeStruct((bsz, seq, ng), F32),
                   jax.ShapeDtypeStruct((bsz, seq, ng), F32),
                   jax.ShapeDtypeStruct((ng, m), F32),
                   jax.ShapeDtypeStruct((ng, m), F32)],
        compiler_params=_cparams("parallel", "parallel"),
        name="mlstm_proj",
    )(x3, x3, x3, g, w_qk, w_voT, w_li, w_lf, w_li.T, w_lf.T, b_li, b_lf, b_li.T, b_lf.T,
      conv_w, conv_b)


def _mlstm_kernel(qT_ref, k_ref, vT_ref, li_ref, lf_ref, liT_ref, lfT_ref, hT_ref,
                  cT_ref, n_ref, m_ref, *, reverse):
    L = A_CHUNK
    J = A_STEP_CHUNKS
    H = A_HEADS
    P = H // 2
    lo = 4 if reverse else 0

    @pl.when(pl.program_id(1) == 0)
    def _():
        cT_ref[...] = jnp.zeros_like(cT_ref)
        n_ref[...] = jnp.zeros_like(n_ref)
        m_ref[...] = jnp.zeros_like(m_ref)

    row_i = lax.broadcasted_iota(jnp.int32, (L, L), 0)
    col_i = lax.broadcasted_iota(jnp.int32, (L, L), 1)
    seen = (col_i >= row_i) if reverse else (col_i <= row_i)
    seen_t = (row_i >= col_i) if reverse else (row_i <= col_i)
    seen_bf = seen.astype(F32).astype(BF16)
    seen_t2 = jnp.concatenate([seen_t, seen_t], axis=1)
    left = lax.broadcasted_iota(jnp.int32, (L, 2 * L), 1) < L
    last = 0 if reverse else L - 1

    pair_rows = lambda rows: jnp.stack(
        [jnp.concatenate([rows[lo + 2 * p:lo + 2 * p + 1], rows[lo + 2 * p + 1:lo + 2 * p + 2]], axis=1)
         for p in range(P)])
    head_rows = lambda rows: rows[lo:lo + H][:, None, :]
    unpair = lambda a: jnp.stack([a[b][:, i * L:(i + 1) * L] for b in range(a.shape[0]) for i in range(2)])

    order = list(range(J - 1, -1, -1) if reverse else range(J))
    m_state = m_ref[...]
    d_l, g_l, w_l, sp_l, sl_l = [], [], [], [], []
    for j in order:
        r0 = j * L
        li = li_ref[r0:r0 + L, :]
        lf = lf_ref[r0:r0 + L, :]
        liT = liT_ref[:, r0:r0 + L]
        lfT = lfT_ref[:, r0:r0 + L]
        b_col = sum(_dot(seen_bf, x) for x in _split3(lf))
        b_row = sum(_dot_nt(x, seen_bf) for x in _split3(lfT))
        r_col = li - b_col
        b_tot = b_row[:, last:last + 1]
        a_row = b_tot - b_row + liT
        m_loc = jnp.max(a_row, axis=1, keepdims=True)
        w_l.append(head_rows(jnp.exp(a_row - m_loc)))
        g_l.append(pair_rows(b_row + m_state))
        m_new = jnp.maximum(b_tot + m_state, m_loc)
        sp_l.append(pair_rows(jnp.exp(b_tot + m_state - m_new)))
        sl_l.append(pair_rows(jnp.exp(m_loc - m_new)))
        m_state = m_new
        d_l.append(jnp.stack([jnp.concatenate(
            [r_col[:, lo + 2 * p + i:lo + 2 * p + i + 1] + b_row[lo + 2 * p + i:lo + 2 * p + i + 1, :]
             for i in range(2)], axis=1) for p in range(P)]))
    m_ref[...] = m_state

    cols = lambda ref, j, n: ref[:, j * L:(j + 1) * L].reshape(n, ref.shape[0] // n, L)
    qT = jnp.concatenate([cols(qT_ref, j, H) for j in order], axis=0)
    vT = jnp.concatenate([cols(vT_ref, j, H) for j in order], axis=0)
    k2 = jnp.stack([k_ref[j * L:(j + 1) * L, 2 * p * A_DQK:(2 * p + 2) * A_DQK]
                    for j in order for p in range(P)])

    w = jnp.concatenate(w_l, axis=0)
    vw_aug = jnp.concatenate([(vT.astype(F32) * w).astype(BF16),
                              jnp.broadcast_to(w, (J * H, A_PAD, L)).astype(BF16)], axis=1)
    vw4 = vw_aug.reshape(J * P, 2, A_DV + A_PAD, L)
    vw2 = jnp.concatenate([vw4[:, 0], vw4[:, 1]], axis=2)
    zk = jnp.zeros_like(k2)
    rk = jnp.concatenate([jnp.where(left[None], k2, zk), jnp.where(left[None], zk, k2)], axis=1)
    upd = _bdot(vw2, rk)

    c_state = cT_ref[...]
    n_state = n_ref[...]
    c_l, n_l = [], []
    for i in range(J):
        c_l.append(c_state)
        n_l.append(n_state)
        u = upd[i * P:(i + 1) * P]
        c_state = sp_l[i] * c_state + sl_l[i] * u[:, :A_DV, :]
        n_state = sp_l[i] * n_state + sl_l[i] * u[:, A_DV:A_DV + 1, :]
    cT_ref[...] = c_state
    n_ref[...] = n_state
    c_prev = jnp.concatenate(c_l, axis=0)
    n_prev = jnp.concatenate(n_l, axis=0)

    qT4 = qT.reshape(J * P, 2, A_DQK, L)
    zq = jnp.zeros((J * P, A_DQK, L), BF16)
    rq = jnp.concatenate([jnp.concatenate([qT4[:, 0], zq], axis=2),
                          jnp.concatenate([zq, qT4[:, 1]], axis=2)], axis=1)
    k_aug = jnp.concatenate(
        [k2, jnp.broadcast_to(n_prev, (J * P, A_PAD, 2 * A_DQK)).astype(BF16)], axis=1)
    sq = _bdot(k_aug, rq)
    qn = sq[:, L:L + 1, :]
    d = jnp.where(seen_t2[None], jnp.concatenate(d_l, axis=0), -jnp.inf)
    g = jnp.concatenate(g_l, axis=0)
    m_t = jnp.maximum(g, jnp.max(d, axis=1, keepdims=True))
    p = jnp.exp(d - m_t) * sq[:, :L, :]
    s_inter = jnp.exp(g - m_t)
    den = s_inter * qn + jnp.sum(p, axis=1, keepdims=True)
    inv = 1.0 / jnp.maximum(jnp.abs(den), jnp.exp(-m_t))
    rhs = jnp.concatenate([(qT.astype(F32) * unpair(s_inter * inv)).astype(BF16),
                           unpair((p * inv).astype(BF16))], axis=1)
    lhs = jnp.concatenate([unpair(c_prev).astype(BF16), vT], axis=2)
    out = _bdot(lhs, rhs).astype(BF16)
    for i, j in enumerate(order):
        hT_ref[:, j * L:(j + 1) * L] = out[i * H:(i + 1) * H].reshape(A_V, L)


def _mlstm(k, qT, vT, li, lf, liT, lfT, bsz, seq, reverse):
    tl = A_STEP_CHUNKS * A_CHUNK
    ns = seq // tl
    ng = li.shape[-1]
    half = A_QK // 2
    blk = (lambda i: ns - 1 - i) if reverse else (lambda i: i)
    gate = pl.BlockSpec((None, tl, ng), lambda b_, i: (b_, blk(i), 0))
    feat = lambda n: pl.BlockSpec((n, tl), lambda b_, i: (0, b_ * ns + blk(i)))
    return pl.pallas_call(
        functools.partial(_mlstm_kernel, reverse=reverse),
        grid=(bsz, ns),
        in_specs=[pl.BlockSpec((None, half, tl), lambda b_, i: (b_, 0, blk(i))),
                  pl.BlockSpec((None, tl, half), lambda b_, i: (b_, blk(i), 0)),
                  feat(A_V), gate, gate, feat(ng), feat(ng)],
        out_specs=feat(A_V),
        out_shape=jax.ShapeDtypeStruct((A_V, bsz * seq), BF16),
        scratch_shapes=[pltpu.VMEM((A_HEADS // 2, A_DV, 2 * A_DQK), F32),
                        pltpu.VMEM((A_HEADS // 2, 1, 2 * A_DQK), F32),
                        pltpu.VMEM((ng, A_CHUNK), F32)],
        compiler_params=_cparams("parallel", "arbitrary"),
        name="mlstm_bwd" if reverse else "mlstm_fwd",
    )(qT, k, vT, li, lf, liT, lfT)


def _a_out_kernel(x_ref, hfT_ref, hbT_ref, oT_ref, hg_ref, w_ref, y_ref):
    yT = hfT_ref[...].astype(F32) + hbT_ref[...].astype(F32)
    parts = []
    for h in range(A_HEADS):
        yh = yT[h * A_DV:(h + 1) * A_DV, :]
        ms = jnp.mean(yh * yh, axis=0, keepdims=True)
        parts.append(yh * lax.rsqrt(ms + RMS_EPS))
    ynT = jnp.concatenate(parts, axis=0)
    zT = (ynT * hg_ref[...] * jax.nn.sigmoid(oT_ref[...])).astype(BF16)
    y_ref[...] = x_ref[...] + _dot_tn(zT, w_ref[...])


def _a_out(x, hfT, hbT, oT, hg, w):
    m, d = x.shape
    row = pl.BlockSpec((ROW_TILE, d), lambda i: (i, 0))
    col = pl.BlockSpec((A_V, ROW_TILE), lambda i: (0, i))
    return pl.pallas_call(
        _a_out_kernel,
        grid=(m // ROW_TILE,),
        in_specs=[row, col, col, col, _resident((A_V, ROW_TILE)), _resident((A_V, d))],
        out_specs=row,
        out_shape=jax.ShapeDtypeStruct((m, d), F32),
        compiler_params=_cparams("parallel"),
        name="mlstm_out",
    )(x, hfT, hbT, oT, hg, w)


def _b_proj_kernel(x_ref, g_ref, w_ref, *rest):
    o_refs, xs_ref = rest[:-1], rest[-1]
    xn = _rms(x_ref[...], g_ref[...])
    nblk = D_MODEL // 128
    lhs = {1: xn.astype(BF16)}
    for c in range(nblk):
        xs_ref[c] = xn[:, c * 128:(c + 1) * 128]
    for dil in sorted({d for _, d in B_GROUPS if d > 1}):
        n = ROW_TILE // dil
        pieces = [jnp.concatenate([xs_ref[c, pl.ds(r, n, stride=dil), :] for c in range(nblk)],
                                  axis=-1).astype(BF16) for r in range(dil)]
        lhs[dil] = jnp.concatenate(pieces, axis=0)
    for j, o_ref in enumerate(o_refs):
        dil = B_GROUPS[j // 3][1]
        y = _dot(lhs[dil], w_ref[:, j * B_WIDTH:(j + 1) * B_WIDTH])
        if j % 3 == 0:
            y = y * (B_DH ** -0.5 * LOG2E)
        y = y.astype(BF16)
        n = ROW_TILE // dil
        for r in range(dil):
            o_ref[:, r * B_WIDTH:(r + 1) * B_WIDTH] = y[r * n:(r + 1) * n, :]


def _b_proj(x, g, w):
    m, d = x.shape
    dils = [dil for _, dil in B_GROUPS for _ in range(3)]
    return pl.pallas_call(
        _b_proj_kernel,
        grid=(m // ROW_TILE,),
        in_specs=[pl.BlockSpec((ROW_TILE, d), lambda i: (i, 0)), _resident((1, d)),
                  _resident(w.shape)],
        out_specs=[pl.BlockSpec((ROW_TILE // dil, dil * B_WIDTH), lambda i: (i, 0)) for dil in dils],
        out_shape=[jax.ShapeDtypeStruct((m // dil, dil * B_WIDTH), BF16) for dil in dils],
        scratch_shapes=[pltpu.VMEM((B_WIDTH // 128, ROW_TILE, 128), F32)],
        compiler_params=_cparams("parallel"),
        name="dilated_proj",
    )(x, g, w)


def _b_attn_kernel(q_ref, kp_ref, kc_ref, kn_ref, vp_ref, vc_ref, vn_ref,
                   bias_first_ref, bias_mid_ref, bias_last_ref, o_ref, stat_ref):
    k_all = jnp.concatenate([kp_ref[...], kc_ref[...], kn_ref[...]], axis=0)
    v_all = jnp.concatenate([vp_ref[...], vc_ref[...], vn_ref[...]], axis=0)
    lane = lax.broadcasted_iota(jnp.int32, (B_TQ, 128), 1)
    low = lane < B_DH
    stat_slot = lane // (STAT_LANES // 2)
    for sb in range(B_SUB):
        bias_ref = bias_first_ref if sb == 0 else (bias_last_ref if sb == B_SUB - 1 else bias_mid_ref)
        q0 = sb * B_TQ
        stat_row = jnp.zeros((B_TQ, 128), F32)
        for hp in range(B_HEADS // 2):
            cols = slice(hp * 128, (hp + 1) * 128)
            q2 = q_ref[q0:q0 + B_TQ, cols]
            k2 = k_all[q0:q0 + B_TK, cols]
            v2 = v_all[q0:q0 + B_TK, cols]
            zero = jnp.zeros_like(q2)
            outs = []
            for par, qsel in ((0, jnp.where(low, q2, zero)), (1, jnp.where(low, zero, q2))):
                hd = 2 * hp + par
                s = _dot_nt(qsel, k2) + bias_ref[hd]
                mx = jnp.max(s, axis=-1, keepdims=True)
                e = jnp.exp2(s - mx)
                den = jnp.sum(e, axis=-1, keepdims=True)
                outs.append(_dot(e.astype(BF16), v2))
                stat_row = jnp.where(stat_slot == 2 * hd, mx,
                                     jnp.where(stat_slot == 2 * hd + 1, den, stat_row))
            o_ref[q0:q0 + B_TQ, cols] = jnp.where(low, outs[0], outs[1]).astype(BF16)
        stat_ref[q0:q0 + B_TQ, :] = stat_row


def _b_attn(q, k, v, bias, bsz, seq, dil):
    n = seq // dil
    rows = B_SUB * B_TQ
    nq = n // rows
    w = B_WIDTH
    q = q.reshape(bsz, n, dil * w)
    k = k.reshape(bsz, n, dil * w)
    v = v.reshape(bsz, n, dil * w)
    hb = rows // B_HALF
    last_half = n // B_HALF - 1
    cur = pl.BlockSpec((None, rows, w), lambda b_, r, i: (b_, i, r))
    prev = pl.BlockSpec((None, B_HALF, w), lambda b_, r, i: (b_, jnp.maximum(i * hb - 1, 0), r))
    nxt = pl.BlockSpec((None, B_HALF, w), lambda b_, r, i: (b_, jnp.minimum((i + 1) * hb, last_half), r))
    at_start = lambda i: jnp.where(i == 0, 1, 0)
    at_end = lambda i: jnp.where(i == nq - 1, 2, 0)
    first_variant = (lambda i: at_start(i) + at_end(i)) if B_SUB == 1 else at_start
    bias_blk = (None, B_HEADS, B_TQ, B_TK)
    bias_first = pl.BlockSpec(bias_blk, lambda b_, r, i: (first_variant(i), 0, 0, 0))
    bias_mid = pl.BlockSpec(bias_blk, lambda b_, r, i: (0, 0, 0, 0), pipeline_mode=pl.Buffered(1))
    bias_last = pl.BlockSpec(bias_blk, lambda b_, r, i: (at_end(i), 0, 0, 0))
    out, stats = pl.pallas_call(
        _b_attn_kernel,
        grid=(bsz, dil, nq),
        in_specs=[cur, prev, cur, nxt, prev, cur, nxt, bias_first, bias_mid, bias_last],
        out_specs=[cur, pl.BlockSpec((None, rows, 128), lambda b_, r, i: (b_, i, r))],
        out_shape=[jax.ShapeDtypeStruct((bsz, n, dil * w), BF16),
                   jax.ShapeDtypeStruct((bsz, n, dil * 128), F32)],
        compiler_params=_cparams("parallel", "parallel", "arbitrary"),
        name=f"dilated_attn_d{dil}",
    )(q, k, k, k, v, v, v, bias, bias, bias)
    return out.reshape(bsz * n, dil * w), stats.reshape(bsz * n, dil * 128)


def _attn_bias(dil):
    slopes = jnp.exp2(-8.0 * (jnp.arange(B_HEADS, dtype=F32) + 1.0) / B_HEADS)
    tq = jnp.arange(B_TQ)[:, None]
    col = jnp.arange(B_TK)[None, :]
    rel = jnp.abs(col - B_HALF - tq)
    base = -slopes[:, None, None] * (rel * dil).astype(F32)[None] * LOG2E
    band = (rel <= B_HALF)[None]
    first = (col >= B_HALF)[None]
    lastv = (col < B_TQ + B_HALF)[None]
    variants = []
    for need_first, need_last in ((False, False), (True, False), (False, True), (True, True)):
        ok = band
        if need_first:
            ok = ok & first
        if need_last:
            ok = ok & lastv
        variants.append(jnp.where(ok, base, NEG_BIG))
    return jnp.stack(variants, axis=0)


def _b_out_kernel(x_ref, o0_ref, o1_ref, o2_ref, s0_ref, s1_ref, s2_ref, e_ref, w_ref, y_ref,
                  os_ref, ss_ref):
    for gi, (s_ref, (_, dil)) in enumerate(zip((s0_ref, s1_ref, s2_ref), B_GROUPS)):
        for r in range(dil):
            ss_ref[gi, pl.ds(r, ROW_TILE // dil, stride=dil), :] = s_ref[:, r * 128:(r + 1) * 128]
    half = STAT_LANES // 2
    mxs = [ss_ref[gi] for gi in range(len(B_GROUPS))]
    dens = [pltpu.roll(m, 128 - half, axis=1) for m in mxs]
    top = jnp.maximum(jnp.maximum(mxs[0], mxs[1]), mxs[2])
    ts = [jnp.exp2(m - top) for m in mxs]
    total = dens[0] * ts[0] + dens[1] * ts[1] + dens[2] * ts[2]
    lane = lax.broadcasted_iota(jnp.int32, total.shape, 1)
    picked = lane % STAT_LANES == 0
    acc = None
    for gi, (t, o_ref, (_, dil)) in enumerate(zip(ts, (o0_ref, o1_ref, o2_ref), B_GROUPS)):
        weight = jnp.where(picked, t / total, 0.0)
        hi = weight.astype(BF16)
        lo = (weight - hi.astype(F32)).astype(BF16)
        wide = _dot(jnp.concatenate([hi, lo], axis=1), e_ref[...])
        nblk = B_WIDTH // 128
        for r in range(dil):
            for c in range(nblk):
                lo = r * B_WIDTH + c * 128
                os_ref[gi, c, pl.ds(r, ROW_TILE // dil, stride=dil), :] = (
                    o_ref[:, lo:lo + 128].astype(F32))
        term = wide * jnp.concatenate([os_ref[gi, c] for c in range(nblk)], axis=-1)
        acc = term if acc is None else acc + term
    y_ref[...] = x_ref[...] + _dot(acc.astype(BF16), w_ref[...])


def _b_out(x, outs, stats, w):
    m, d = x.shape
    head_of_col = jnp.arange(B_WIDTH) // B_DH
    expand = (jnp.arange(128)[:, None] == head_of_col[None, :] * STAT_LANES).astype(BF16)
    expand = jnp.concatenate([expand, expand], axis=0)
    row = pl.BlockSpec((ROW_TILE, d), lambda i: (i, 0))
    grp = lambda lanes: [pl.BlockSpec((ROW_TILE // dil, dil * lanes), lambda i: (i, 0))
                         for _, dil in B_GROUPS]
    ng = len(B_GROUPS)
    return pl.pallas_call(
        _b_out_kernel,
        grid=(m // ROW_TILE,),
        in_specs=[row, *grp(B_WIDTH), *grp(128), _resident((256, B_WIDTH)),
                  _resident((B_WIDTH, d))],
        out_specs=row,
        out_shape=jax.ShapeDtypeStruct((m, d), F32),
        scratch_shapes=[pltpu.VMEM((ng, B_WIDTH // 128, ROW_TILE, 128), F32),
                        pltpu.VMEM((ng, ROW_TILE, 128), F32)],
        compiler_params=_cparams("parallel"),
        name="dilated_out",
    )(x, *outs, *stats, expand, w)


def _mlstm_mixer(x, g, w_in, conv_w, conv_b, gate_b, head_g, w_out, bsz, seq):
    w_qk = w_in[:, :A_QK].astype(BF16)
    w_voT = w_in[:, A_QK:A_QK + 2 * A_V].T.astype(BF16)
    wg = w_in[:, A_QK + 2 * A_V:]
    H = A_HEADS
    pick = lambda a, s: jnp.concatenate([a[..., s * H:(s + 1) * H], a[..., (s + 2) * H:(s + 3) * H]], axis=-1)
    w_li, w_lf = pick(wg, 0).astype(BF16), pick(wg, 1).astype(BF16)
    gb = gate_b.astype(F32)[None, :]
    b_li, b_lf = pick(gb, 0), pick(gb, 1)
    k, qT, vT, oT, li, lf, liT, lfT = _a_proj(x, g, w_qk, w_voT, w_li, w_lf, b_li, b_lf,
                                              conv_w.astype(F32), conv_b.astype(F32)[None, :], bsz, seq)
    hfT = _mlstm(k, qT, vT, li, lf, liT, lfT, bsz, seq, reverse=False)
    hbT = _mlstm(k, qT, vT, li, lf, liT, lfT, bsz, seq, reverse=True)
    hg = jnp.broadcast_to(head_g.astype(F32)[:, None], (A_V, ROW_TILE))
    return _a_out(x, hfT, hbT, oT, hg, w_out.astype(BF16))


def _dilated_mixer(x, g, w_in, w_out, bsz, seq):
    slabs = _b_proj(x, g, w_in.astype(BF16))
    outs, stats = [], []
    for gi, (_, dil) in enumerate(B_GROUPS):
        o, l = _b_attn(slabs[3 * gi], slabs[3 * gi + 1], slabs[3 * gi + 2], _attn_bias(dil),
                       bsz, seq, dil)
        outs.append(o)
        stats.append(l)
    return _b_out(x, outs, stats, w_out.astype(BF16))


def kernel(x, norm_ffn1, ffn1_gate, ffn1_up, ffn1_down, norm_mix, a_w_in, a_conv_w, a_conv_b,
           a_gate_b, a_head_g, a_w_out, b_w_in, b_w_out, norm_ffn2, ffn2_gate, ffn2_up,
           ffn2_down, norm_final):
    bsz, seq, d = x.shape
    h = x.reshape(bsz * seq, d)
    row = lambda a: a.astype(F32)[None, :]
    g_final = row(norm_final)
    for i in range(DEPTH):
        h = _ffn(h, row(norm_ffn1[i]), ffn1_gate, ffn1_up, ffn1_down, i, g_final, False)
        j = i // 2
        if i % 2 == 0:
            h = _mlstm_mixer(h, row(norm_mix[i]), a_w_in[j], a_conv_w[j], a_conv_b[j],
                             a_gate_b[j], a_head_g[j], a_w_out[j], bsz, seq)
        else:
            h = _dilated_mixer(h, row(norm_mix[i]), b_w_in[j], b_w_out[j], bsz, seq)
        h = _ffn(h, row(norm_ffn2[i]), ffn2_gate, ffn2_up, ffn2_down, i, g_final, i == DEPTH - 1)
    return h.reshape(bsz, seq, d)
```

```python
import functools

import jax
import jax.numpy as jnp
from jax import lax
from jax.experimental import pallas as pl
from jax.experimental.pallas import tpu as pltpu

F32 = jnp.float32
BF16 = jnp.bfloat16

D_MODEL = 1024
DEPTH = 4
D_FF = 2816
RMS_EPS = 1e-6

A_HEADS = 4
A_DQK = 128
A_DV = 256
A_CHUNK = 128
A_CONV = 5
A_QK = 2 * A_HEADS * A_DQK
A_V = A_HEADS * A_DV
A_PAD = 16
assert A_DQK == A_CHUNK

B_GROUPS = ((128, 1), (512, 4), (2048, 16))
B_HEADS = 16
B_DH = 64
B_WIDTH = B_HEADS * B_DH
B_HALF = 64
B_TQ = 128
B_TK = B_TQ + 2 * B_HALF
B_SUB = 8
LOG2E = 1.4426950408889634
STAT_LANES = 128 // B_HEADS
NEG_BIG = -1e30

ROW_TILE = 512
W_STAGE_ROWS = 256
W_STAGE_SLOTS = 3
A_STEP_CHUNKS = 8
VMEM_LIMIT = 56 * 1024 * 1024


def _cparams(*sem):
    return pltpu.CompilerParams(dimension_semantics=sem, vmem_limit_bytes=VMEM_LIMIT)


def _resident(shape):
    nd = len(shape)
    return pl.BlockSpec(shape, lambda *_: (0,) * nd, pipeline_mode=pl.Buffered(1))


def _rms(x, g):
    ms = jnp.mean(x * x, axis=-1, keepdims=True)
    return x * lax.rsqrt(ms + RMS_EPS) * g


def _dot(a, b):
    return jnp.dot(a, b, preferred_element_type=F32)


def _dot_nt(a, b):
    return lax.dot_general(a, b, (((1,), (1,)), ((), ())), preferred_element_type=F32)


def _dot_tn(a, b):
    return lax.dot_general(a, b, (((0,), (0,)), ((), ())), preferred_element_type=F32)


def _bdot(a, b):
    return lax.dot_general(a, b, (((2,), (1,)), ((0,), (0,))), preferred_element_type=F32)


def _split3(x):
    hi = x.astype(BF16)
    r1 = x - hi.astype(F32)
    mid = r1.astype(BF16)
    lo = (r1 - mid.astype(F32)).astype(BF16)
    return hi, mid, lo


def _stage_copy(src_hbm, layer, stage_ref, sem, c):
    slots, chunk = stage_ref.shape[0], stage_ref.shape[1]
    slot = c % slots
    return pltpu.make_async_copy(src_hbm.at[layer, pl.ds(c * chunk, chunk), :],
                                 stage_ref.at[slot], sem.at[slot])


def _load_as_bf16(src_hbm, layer, dst_ref, stage_ref, sem):
    slots, chunk = stage_ref.shape[0], stage_ref.shape[1]
    n = src_hbm.shape[1] // chunk
    for c in range(min(slots - 1, n)):
        _stage_copy(src_hbm, layer, stage_ref, sem, c).start()
    for c in range(n):
        if c + slots - 1 < n:
            _stage_copy(src_hbm, layer, stage_ref, sem, c + slots - 1).start()
        _stage_copy(src_hbm, layer, stage_ref, sem, c).wait()
        dst_ref[c * chunk:(c + 1) * chunk, :] = stage_ref[c % slots].astype(BF16)


def _ffn_kernel(x_ref, g_ref, wg_hbm, wu_hbm, wd_hbm, gf_ref, o_ref,
                wg_ref, wu_ref, wd_ref, stage_in_ref, stage_out_ref, sem, *, layer, final_norm):
    @pl.when(pl.program_id(0) == 0)
    def _():
        _load_as_bf16(wg_hbm, layer, wg_ref, stage_in_ref, sem)
        _load_as_bf16(wu_hbm, layer, wu_ref, stage_in_ref, sem)
        _load_as_bf16(wd_hbm, layer, wd_ref, stage_out_ref, sem)

    x = x_ref[...]
    xn = _rms(x, g_ref[...]).astype(BF16)
    gate = _dot(xn, wg_ref[...])
    up = _dot(xn, wu_ref[...])
    act = (gate * jax.nn.sigmoid(gate) * up).astype(BF16)
    y = x + 0.5 * _dot(act, wd_ref[...])
    if final_norm:
        y = _rms(y, gf_ref[...])
    o_ref[...] = y


def _ffn(x, g, wg, wu, wd, layer, g_final, final_norm):
    m, d = x.shape
    f = wg.shape[2]
    row = pl.BlockSpec((ROW_TILE, d), lambda i: (i, 0))
    hbm = pl.BlockSpec(memory_space=pltpu.MemorySpace.HBM)
    return pl.pallas_call(
        functools.partial(_ffn_kernel, layer=layer, final_norm=final_norm),
        grid=(m // ROW_TILE,),
        in_specs=[row, _resident((1, d)), hbm, hbm, hbm, _resident((1, d))],
        out_specs=row,
        out_shape=jax.ShapeDtypeStruct((m, d), F32),
        scratch_shapes=[pltpu.VMEM((d, f), BF16), pltpu.VMEM((d, f), BF16), pltpu.VMEM((f, d), BF16),
                        pltpu.VMEM((W_STAGE_SLOTS, W_STAGE_ROWS, f), F32),
                        pltpu.VMEM((W_STAGE_SLOTS, W_STAGE_ROWS, d), F32),
                        pltpu.SemaphoreType.DMA((W_STAGE_SLOTS,))],
        compiler_params=_cparams("arbitrary"),
        name="ffn",
    )(x, g, wg, wu, wd, g_final)


A_HALO = 8


def _a_proj_kernel(x_ref, xp_ref, xn_ref, g_ref, wtok_ref, wfeat_ref,
                   bli_ref, blf_ref, bliT_ref, blfT_ref, cw_ref, cb_ref,
                   k_ref, qT_ref, vT_ref, oT_ref, li_ref, lf_ref, liT_ref, lfT_ref, *, n_steps):
    i = pl.program_id(1)
    g = g_ref[...]
    xn = _rms(x_ref[...], g)
    before = _rms(xp_ref[...], g) * (i > 0).astype(F32)
    after = _rms(xn_ref[...], g) * (i < n_steps - 1).astype(F32)
    ext = jnp.concatenate([before, xn, after], axis=0).astype(BF16)
    xb = xn.astype(BF16)
    rows = xn.shape[0]
    total = rows + 2 * A_HALO
    ng = li_ref.shape[-1]

    tok = _dot(ext, wtok_ref[...])
    qk = tok[:, :A_QK]
    gates = tok[A_HALO:A_HALO + rows, A_QK:A_QK + 2 * ng]
    li_ref[...] = gates[:, :ng] + bli_ref[...]
    lf_ref[...] = jax.nn.log_sigmoid(gates[:, ng:] + blf_ref[...])
    half = A_CONV // 2
    acc = qk[A_HALO:A_HALO + rows, :] * cw_ref[half:half + 1, :] + cb_ref[...]
    for s in range(1, half + 1):
        back = pltpu.roll(qk, s, axis=0)[A_HALO:A_HALO + rows, :]
        fwd = pltpu.roll(qk, total - s, axis=0)[A_HALO:A_HALO + rows, :]
        acc = acc + back * cw_ref[half - s:half - s + 1, :] + fwd * cw_ref[half + s:half + s + 1, :]
    y = acc * jax.nn.sigmoid(acc)
    hq = A_QK // 2
    k_ref[...] = y[:, hq:].astype(BF16)
    qT_ref[...] = (y[:, :hq] * (A_DQK ** -0.5)).T.astype(BF16)

    feat = _dot_nt(wfeat_ref[...], xb)
    vT_ref[...] = feat[:A_V, :].astype(BF16)
    oT_ref[...] = feat[A_V:2 * A_V, :].astype(BF16)
    liT_ref[...] = feat[2 * A_V:2 * A_V + ng, :] + bliT_ref[...]
    lfT_ref[...] = jax.nn.log_sigmoid(feat[2 * A_V + ng:, :] + blfT_ref[...])


def _a_proj(x, g, w_tok, w_feat, b_li, b_lf, conv_w, conv_b, bsz, seq):
    m, d = x.shape
    ng = b_li.shape[1]
    hq = A_QK // 2
    n_steps = seq // ROW_TILE
    rh = ROW_TILE // A_HALO
    x3 = x.reshape(bsz, seq, d)
    flat = lambda b_, i: b_ * n_steps + i
    tok = lambda n: pl.BlockSpec((None, ROW_TILE, n), lambda b_, i: (b_, i, 0))
    feat = lambda n: pl.BlockSpec((n, ROW_TILE), lambda b_, i: (0, flat(b_, i)))
    return pl.pallas_call(
        functools.partial(_a_proj_kernel, n_steps=n_steps),
        grid=(bsz, n_steps),
        in_specs=[tok(d),
                  pl.BlockSpec((None, A_HALO, d), lambda b_, i: (b_, jnp.maximum(i * rh - 1, 0), 0)),
                  pl.BlockSpec((None, A_HALO, d),
                               lambda b_, i: (b_, jnp.minimum((i + 1) * rh, seq // A_HALO - 1), 0)),
                  _resident((1, d)), _resident(w_tok.shape), _resident(w_feat.shape),
                  _resident((1, ng)), _resident((1, ng)), _resident((ng, 1)), _resident((ng, 1)),
                  _resident((A_CONV, A_QK)), _resident((1, A_QK))],
        out_specs=[tok(hq), pl.BlockSpec((None, hq, ROW_TILE), lambda b_, i: (b_, 0, i)),
                   feat(A_V), feat(A_V), tok(ng), tok(ng), feat(ng), feat(ng)],
        out_shape=[jax.ShapeDtypeStruct((bsz, seq, hq), BF16),
                   jax.ShapeDtypeStruct((bsz, hq, seq), BF16),
                   jax.ShapeDtypeStruct((A_V, m), BF16),
                   jax.ShapeDtypeStruct((A_V, m), BF16),
                   jax.ShapeDtypeStruct((bsz, seq, ng), F32),
                   jax.ShapeDtypeStruct((bsz, seq, ng), F32),
                   jax.ShapeDtypeStruct((ng, m), F32),
                   jax.ShapeDtypeStruct((ng, m), F32)],
        compiler_params=_cparams("parallel", "parallel"),
        name="mlstm_proj",
    )(x3, x3, x3, g, w_tok, w_feat, b_li, b_lf, b_li.T, b_lf.T, conv_w, conv_b)


def _mlstm_kernel(qT_ref, k_ref, vT_ref, li_ref, lf_ref, liT_ref, lfT_ref, hT_ref,
                  cT_ref, n_ref, m_ref, *, reverse):
    L = A_CHUNK
    J = A_STEP_CHUNKS
    H = A_HEADS
    P = H // 2
    lo = 4 if reverse else 0

    @pl.when(pl.program_id(1) == 0)
    def _():
        cT_ref[...] = jnp.zeros_like(cT_ref)
        n_ref[...] = jnp.zeros_like(n_ref)
        m_ref[...] = jnp.zeros_like(m_ref)

    row_i = lax.broadcasted_iota(jnp.int32, (L, L), 0)
    col_i = lax.broadcasted_iota(jnp.int32, (L, L), 1)
    seen = (col_i >= row_i) if reverse else (col_i <= row_i)
    seen_t = (row_i >= col_i) if reverse else (row_i <= col_i)
    seen_bf = seen.astype(F32).astype(BF16)
    seen_t2 = jnp.concatenate([seen_t, seen_t], axis=1)
    left = lax.broadcasted_iota(jnp.int32, (L, 2 * L), 1) < L
    last = 0 if reverse else L - 1

    pair_rows = lambda rows: jnp.stack(
        [jnp.concatenate([rows[lo + 2 * p:lo + 2 * p + 1], rows[lo + 2 * p + 1:lo + 2 * p + 2]], axis=1)
         for p in range(P)])
    head_rows = lambda rows: rows[lo:lo + H][:, None, :]
    unpair = lambda a: jnp.stack([a[b][:, i * L:(i + 1) * L] for b in range(a.shape[0]) for i in range(2)])

    order = list(range(J - 1, -1, -1) if reverse else range(J))
    m_state = m_ref[...]
    d_l, g_l, w_l, sp_l, sl_l = [], [], [], [], []
    for j in order:
        r0 = j * L
        li = li_ref[r0:r0 + L, :]
        lf = lf_ref[r0:r0 + L, :]
        liT = liT_ref[:, r0:r0 + L]
        lfT = lfT_ref[:, r0:r0 + L]
        b_col = sum(_dot(seen_bf, x) for x in _split3(lf))
        b_row = sum(_dot_nt(x, seen_bf) for x in _split3(lfT))
        r_col = li - b_col
        b_tot = b_row[:, last:last + 1]
        a_row = b_tot - b_row + liT
        m_loc = jnp.max(a_row, axis=1, keepdims=True)
        w_l.append(head_rows(jnp.exp(a_row - m_loc)))
        g_l.append(pair_rows(b_row + m_state))
        m_new = jnp.maximum(b_tot + m_state, m_loc)
        sp_l.append(pair_rows(jnp.exp(b_tot + m_state - m_new)))
        sl_l.append(pair_rows(jnp.exp(m_loc - m_new)))
        m_state = m_new
        d_l.append(jnp.stack([jnp.concatenate(
            [r_col[:, lo + 2 * p + i:lo + 2 * p + i + 1] + b_row[lo + 2 * p + i:lo + 2 * p + i + 1, :]
             for i in range(2)], axis=1) for p in range(P)]))
    m_ref[...] = m_state

    cols = lambda ref, j, n: ref[:, j * L:(j + 1) * L].reshape(n, ref.shape[0] // n, L)
    qT = jnp.concatenate([cols(qT_ref, j, H) for j in order], axis=0)
    vT = jnp.concatenate([cols(vT_ref, j, H) for j in order], axis=0)
    k2 = jnp.stack([k_ref[j * L:(j + 1) * L, 2 * p * A_DQK:(2 * p + 2) * A_DQK]
                    for j in order for p in range(P)])

    w = jnp.concatenate(w_l, axis=0)
    vw_aug = jnp.concatenate([(vT.astype(F32) * w).astype(BF16),
                              jnp.broadcast_to(w, (J * H, A_PAD, L)).astype(BF16)], axis=1)
    vw4 = vw_aug.reshape(J * P, 2, A_DV + A_PAD, L)
    vw2 = jnp.concatenate([vw4[:, 0], vw4[:, 1]], axis=2)
    zk = jnp.zeros_like(k2)
    rk = jnp.concatenate([jnp.where(left[None], k2, zk), jnp.where(left[None], zk, k2)], axis=1)
    upd = _bdot(vw2, rk)

    c_state = cT_ref[...]
    n_state = n_ref[...]
    c_l, n_l = [], []
    for i in range(J):
        c_l.append(c_state)
        n_l.append(n_state)
        u = upd[i * P:(i + 1) * P]
        c_state = sp_l[i] * c_state + sl_l[i] * u[:, :A_DV, :]
        n_state = sp_l[i] * n_state + sl_l[i] * u[:, A_DV:A_DV + 1, :]
    cT_ref[...] = c_state
    n_ref[...] = n_state
    c_prev = jnp.concatenate(c_l, axis=0)
    n_prev = jnp.concatenate(n_l, axis=0)

    qT4 = qT.reshape(J * P, 2, A_DQK, L)
    zq = jnp.zeros((J * P, A_DQK, L), BF16)
    rq = jnp.concatenate([jnp.concatenate([qT4[:, 0], zq], axis=2),
                          jnp.concatenate([zq, qT4[:, 1]], axis=2)], axis=1)
    k_aug = jnp.concatenate(
        [k2, jnp.broadcast_to(n_prev, (J * P, A_PAD, 2 * A_DQK)).astype(BF16)], axis=1)
    sq = _bdot(k_aug, rq)
    qn = sq[:, L:L + 1, :]
    d = jnp.where(seen_t2[None], jnp.concatenate(d_l, axis=0), -jnp.inf)
    g = jnp.concatenate(g_l, axis=0)
    m_t = jnp.maximum(g, jnp.max(d, axis=1, keepdims=True))
    p = jnp.exp(d - m_t) * sq[:, :L, :]
    s_inter = jnp.exp(g - m_t)
    den = s_inter * qn + jnp.sum(p, axis=1, keepdims=True)
    inv = 1.0 / jnp.maximum(jnp.abs(den), jnp.exp(-m_t))
    rhs = jnp.concatenate([(qT.astype(F32) * unpair(s_inter * inv)).astype(BF16),
                           unpair((p * inv).astype(BF16))], axis=1)
    lhs = jnp.concatenate([unpair(c_prev).astype(BF16), vT], axis=2)
    out = _bdot(lhs, rhs).astype(BF16)
    for i, j in enumerate(order):
        hT_ref[:, j * L:(j + 1) * L] = out[i * H:(i + 1) * H].reshape(A_V, L)


def _mlstm(k, qT, vT, li, lf, liT, lfT, bsz, seq, reverse):
    tl = A_STEP_CHUNKS * A_CHUNK
    ns = seq // tl
    ng = li.shape[-1]
    half = A_QK // 2
    blk = (lambda i: ns - 1 - i) if reverse else (lambda i: i)
    gate = pl.BlockSpec((None, tl, ng), lambda b_, i: (b_, blk(i), 0))
    feat = lambda n: pl.BlockSpec((n, tl), lambda b_, i: (0, b_ * ns + blk(i)))
    return pl.pallas_call(
        functools.partial(_mlstm_kernel, reverse=reverse),
        grid=(bsz, ns),
        in_specs=[pl.BlockSpec((None, half, tl), lambda b_, i: (b_, 0, blk(i))),
                  pl.BlockSpec((None, tl, half), lambda b_, i: (b_, blk(i), 0)),
                  feat(A_V), gate, gate, feat(ng), feat(ng)],
        out_specs=feat(A_V),
        out_shape=jax.ShapeDtypeStruct((A_V, bsz * seq), BF16),
        scratch_shapes=[pltpu.VMEM((A_HEADS // 2, A_DV, 2 * A_DQK), F32),
                        pltpu.VMEM((A_HEADS // 2, 1, 2 * A_DQK), F32),
                        pltpu.VMEM((ng, A_CHUNK), F32)],
        compiler_params=_cparams("parallel", "arbitrary"),
        name="mlstm_bwd" if reverse else "mlstm_fwd",
    )(qT, k, vT, li, lf, liT, lfT)


def _a_out_kernel(x_ref, hfT_ref, hbT_ref, oT_ref, hg_ref, w_ref, y_ref):
    yT = hfT_ref[...].astype(F32) + hbT_ref[...].astype(F32)
    parts = []
    for h in range(A_HEADS):
        yh = yT[h * A_DV:(h + 1) * A_DV, :]
        ms = jnp.mean(yh * yh, axis=0, keepdims=True)
        parts.append(yh * lax.rsqrt(ms + RMS_EPS))
    ynT = jnp.concatenate(parts, axis=0)
    zT = (ynT * hg_ref[...] * jax.nn.sigmoid(oT_ref[...].astype(F32))).astype(BF16)
    y_ref[...] = x_ref[...] + _dot_tn(zT, w_ref[...])


def _a_out(x, hfT, hbT, oT, hg, w):
    m, d = x.shape
    row = pl.BlockSpec((ROW_TILE, d), lambda i: (i, 0))
    col = pl.BlockSpec((A_V, ROW_TILE), lambda i: (0, i))
    return pl.pallas_call(
        _a_out_kernel,
        grid=(m // ROW_TILE,),
        in_specs=[row, col, col, col, _resident((A_V, ROW_TILE)), _resident((A_V, d))],
        out_specs=row,
        out_shape=jax.ShapeDtypeStruct((m, d), F32),
        compiler_params=_cparams("parallel"),
        name="mlstm_out",
    )(x, hfT, hbT, oT, hg, w)


def _b_proj_kernel(x_ref, g_ref, w_ref, *rest):
    o_refs, xs_ref = rest[:-1], rest[-1]
    xn = _rms(x_ref[...], g_ref[...])
    nblk = D_MODEL // 128
    lhs = {1: xn.astype(BF16)}
    for c in range(nblk):
        xs_ref[c] = xn[:, c * 128:(c + 1) * 128]
    for dil in sorted({d for _, d in B_GROUPS if d > 1}):
        n = ROW_TILE // dil
        pieces = [jnp.concatenate([xs_ref[c, pl.ds(r, n, stride=dil), :] for c in range(nblk)],
                                  axis=-1).astype(BF16) for r in range(dil)]
        lhs[dil] = jnp.concatenate(pieces, axis=0)
    for j, o_ref in enumerate(o_refs):
        dil = B_GROUPS[j // 3][1]
        y = _dot(lhs[dil], w_ref[:, j * B_WIDTH:(j + 1) * B_WIDTH])
        if j % 3 == 0:
            y = y * (B_DH ** -0.5 * LOG2E)
        y = y.astype(BF16)
        n = ROW_TILE // dil
        for r in range(dil):
            o_ref[:, r * B_WIDTH:(r + 1) * B_WIDTH] = y[r * n:(r + 1) * n, :]


def _b_proj(x, g, w):
    m, d = x.shape
    dils = [dil for _, dil in B_GROUPS for _ in range(3)]
    return pl.pallas_call(
        _b_proj_kernel,
        grid=(m // ROW_TILE,),
        in_specs=[pl.BlockSpec((ROW_TILE, d), lambda i: (i, 0)), _resident((1, d)),
                  _resident(w.shape)],
        out_specs=[pl.BlockSpec((ROW_TILE // dil, dil * B_WIDTH), lambda i: (i, 0)) for dil in dils],
        out_shape=[jax.ShapeDtypeStruct((m // dil, dil * B_WIDTH), BF16) for dil in dils],
        scratch_shapes=[pltpu.VMEM((B_WIDTH // 128, ROW_TILE, 128), F32)],
        compiler_params=_cparams("parallel"),
        name="dilated_proj",
    )(x, g, w)


def _b_attn_kernel(q_ref, kp_ref, kc_ref, kn_ref, vp_ref, vc_ref, vn_ref,
                   bias_first_ref, bias_mid_ref, bias_last_ref, o_ref, stat_ref):
    k_all = jnp.concatenate([kp_ref[...], kc_ref[...], kn_ref[...]], axis=0)
    v_all = jnp.concatenate([vp_ref[...], vc_ref[...], vn_ref[...]], axis=0)
    lane = lax.broadcasted_iota(jnp.int32, (B_TQ, 128), 1)
    low = lane < B_DH
    stat_slot = lane // (STAT_LANES // 2)
    for sb in range(B_SUB):
        bias_ref = bias_first_ref if sb == 0 else (bias_last_ref if sb == B_SUB - 1 else bias_mid_ref)
        q0 = sb * B_TQ
        stat_row = jnp.zeros((B_TQ, 128), F32)
        for hp in range(B_HEADS // 2):
            cols = slice(hp * 128, (hp + 1) * 128)
            q2 = q_ref[q0:q0 + B_TQ, cols]
            k2 = k_all[q0:q0 + B_TK, cols]
            v2 = v_all[q0:q0 + B_TK, cols]
            zero = jnp.zeros_like(q2)
            outs = []
            for par, qsel in ((0, jnp.where(low, q2, zero)), (1, jnp.where(low, zero, q2))):
                hd = 2 * hp + par
                s = _dot_nt(qsel, k2) + bias_ref[hd]
                mx = jnp.max(s, axis=-1, keepdims=True)
                e = jnp.exp2(s - mx)
                den = jnp.sum(e, axis=-1, keepdims=True)
                outs.append(_dot(e.astype(BF16), v2))
                stat_row = jnp.where(stat_slot == 2 * hd, mx,
                                     jnp.where(stat_slot == 2 * hd + 1, den, stat_row))
            o_ref[q0:q0 + B_TQ, cols] = jnp.where(low, outs[0], outs[1]).astype(BF16)
        stat_ref[q0:q0 + B_TQ, :] = stat_row


def _b_attn(q, k, v, bias, bsz, seq, dil):
    n = seq // dil
    rows = B_SUB * B_TQ
    nq = n // rows
    w = B_WIDTH
    q = q.reshape(bsz, n, dil * w)
    k = k.reshape(bsz, n, dil * w)
    v = v.reshape(bsz, n, dil * w)
    hb = rows // B_HALF
    last_half = n // B_HALF - 1
    cur = pl.BlockSpec((None, rows, w), lambda b_, r, i: (b_, i, r))
    prev = pl.BlockSpec((None, B_HALF, w), lambda b_, r, i: (b_, jnp.maximum(i * hb - 1, 0), r))
    nxt = pl.BlockSpec((None, B_HALF, w), lambda b_, r, i: (b_, jnp.minimum((i + 1) * hb, last_half), r))
    at_start = lambda i: jnp.where(i == 0, 1, 0)
    at_end = lambda i: jnp.where(i == nq - 1, 2, 0)
    first_variant = (lambda i: at_start(i) + at_end(i)) if B_SUB == 1 else at_start
    bias_blk = (None, B_HEADS, B_TQ, B_TK)
    bias_first = pl.BlockSpec(bias_blk, lambda b_, r, i: (first_variant(i), 0, 0, 0))
    bias_mid = pl.BlockSpec(bias_blk, lambda b_, r, i: (0, 0, 0, 0), pipeline_mode=pl.Buffered(1))
    bias_last = pl.BlockSpec(bias_blk, lambda b_, r, i: (at_end(i), 0, 0, 0))
    out, stats = pl.pallas_call(
        _b_attn_kernel,
        grid=(bsz, dil, nq),
        in_specs=[cur, prev, cur, nxt, prev, cur, nxt, bias_first, bias_mid, bias_last],
        out_specs=[cur, pl.BlockSpec((None, rows, 128), lambda b_, r, i: (b_, i, r))],
        out_shape=[jax.ShapeDtypeStruct((bsz, n, dil * w), BF16),
                   jax.ShapeDtypeStruct((bsz, n, dil * 128), F32)],
        compiler_params=_cparams("parallel", "parallel", "arbitrary"),
        name=f"dilated_attn_d{dil}",
    )(q, k, k, k, v, v, v, bias, bias, bias)
    return out.reshape(bsz * n, dil * w), stats.reshape(bsz * n, dil * 128)


def _attn_bias(dil):
    slopes = jnp.exp2(-8.0 * (jnp.arange(B_HEADS, dtype=F32) + 1.0) / B_HEADS)
    tq = jnp.arange(B_TQ)[:, None]
    col = jnp.arange(B_TK)[None, :]
    rel = jnp.abs(col - B_HALF - tq)
    base = -slopes[:, None, None] * (rel * dil).astype(F32)[None] * LOG2E
    band = (rel <= B_HALF)[None]
    first = (col >= B_HALF)[None]
    lastv = (col < B_TQ + B_HALF)[None]
    variants = []
    for need_first, need_last in ((False, False), (True, False), (False, True), (True, True)):
        ok = band
        if need_first:
            ok = ok & first
        if need_last:
            ok = ok & lastv
        variants.append(jnp.where(ok, base, NEG_BIG))
    return jnp.stack(variants, axis=0)


def _b_out_kernel(x_ref, o0_ref, o1_ref, o2_ref, s0_ref, s1_ref, s2_ref, e_ref, w_ref, y_ref,
                  os_ref, ss_ref):
    for gi, (s_ref, (_, dil)) in enumerate(zip((s0_ref, s1_ref, s2_ref), B_GROUPS)):
        for r in range(dil if dil > 1 else 0):
            ss_ref[gi, pl.ds(r, ROW_TILE // dil, stride=dil), :] = s_ref[:, r * 128:(r + 1) * 128]
    half = STAT_LANES // 2
    mxs = [s_ref[...] if dil == 1 else ss_ref[gi]
           for gi, (s_ref, (_, dil)) in enumerate(zip((s0_ref, s1_ref, s2_ref), B_GROUPS))]
    dens = [pltpu.roll(m, 128 - half, axis=1) for m in mxs]
    top = jnp.maximum(jnp.maximum(mxs[0], mxs[1]), mxs[2])
    ts = [jnp.exp2(m - top) for m in mxs]
    total = dens[0] * ts[0] + dens[1] * ts[1] + dens[2] * ts[2]
    lane = lax.broadcasted_iota(jnp.int32, total.shape, 1)
    picked = lane % STAT_LANES == 0
    acc = None
    for gi, (t, o_ref, (_, dil)) in enumerate(zip(ts, (o0_ref, o1_ref, o2_ref), B_GROUPS)):
        weight = jnp.where(picked, t / total, 0.0)
        hi = weight.astype(BF16)
        lo = (weight - hi.astype(F32)).astype(BF16)
        wide = _dot(jnp.concatenate([hi, lo], axis=1), e_ref[...])
        nblk = B_WIDTH // 128
        if dil == 1:
            vals = o_ref[...].astype(F32)
        else:
            for r in range(dil):
                for c in range(nblk):
                    lo = r * B_WIDTH + c * 128
                    os_ref[gi, c, pl.ds(r, ROW_TILE // dil, stride=dil), :] = (
                        o_ref[:, lo:lo + 128].astype(F32))
            vals = jnp.concatenate([os_ref[gi, c] for c in range(nblk)], axis=-1)
        term = wide * vals
        acc = term if acc is None else acc + term
    y_ref[...] = x_ref[...] + _dot(acc.astype(BF16), w_ref[...])


def _b_out(x, outs, stats, w):
    m, d = x.shape
    head_of_col = jnp.arange(B_WIDTH) // B_DH
    expand = (jnp.arange(128)[:, None] == head_of_col[None, :] * STAT_LANES).astype(BF16)
    expand = jnp.concatenate([expand, expand], axis=0)
    row = pl.BlockSpec((ROW_TILE, d), lambda i: (i, 0))
    grp = lambda lanes: [pl.BlockSpec((ROW_TILE // dil, dil * lanes), lambda i: (i, 0))
                         for _, dil in B_GROUPS]
    ng = len(B_GROUPS)
    return pl.pallas_call(
        _b_out_kernel,
        grid=(m // ROW_TILE,),
        in_specs=[row, *grp(B_WIDTH), *grp(128), _resident((256, B_WIDTH)),
                  _resident((B_WIDTH, d))],
        out_specs=row,
        out_shape=jax.ShapeDtypeStruct((m, d), F32),
        scratch_shapes=[pltpu.VMEM((ng, B_WIDTH // 128, ROW_TILE, 128), F32),
                        pltpu.VMEM((ng, ROW_TILE, 128), F32)],
        compiler_params=_cparams("parallel"),
        name="dilated_out",
    )(x, *outs, *stats, expand, w)


def _mlstm_mixer(x, g, w_in, conv_w, conv_b, gate_b, head_g, w_out, bsz, seq):
    wg = w_in[:, A_QK + 2 * A_V:]
    H = A_HEADS
    pick = lambda a, s: jnp.concatenate([a[..., s * H:(s + 1) * H], a[..., (s + 2) * H:(s + 3) * H]], axis=-1)
    w_li, w_lf = pick(wg, 0), pick(wg, 1)
    gate_pad = jnp.zeros((w_in.shape[0], 128 - 4 * H), w_in.dtype)
    w_tok = jnp.concatenate([w_in[:, :A_QK], w_li, w_lf, gate_pad], axis=1).astype(BF16)
    w_feat = jnp.concatenate([w_in[:, A_QK:A_QK + 2 * A_V], w_li, w_lf], axis=1).T.astype(BF16)
    gb = gate_b.astype(F32)[None, :]
    b_li, b_lf = pick(gb, 0), pick(gb, 1)
    k, qT, vT, oT, li, lf, liT, lfT = _a_proj(x, g, w_tok, w_feat, b_li, b_lf,
                                              conv_w.astype(F32), conv_b.astype(F32)[None, :], bsz, seq)
    hfT = _mlstm(k, qT, vT, li, lf, liT, lfT, bsz, seq, reverse=False)
    hbT = _mlstm(k, qT, vT, li, lf, liT, lfT, bsz, seq, reverse=True)
    hg = jnp.broadcast_to(head_g.astype(F32)[:, None], (A_V, ROW_TILE))
    return _a_out(x, hfT, hbT, oT, hg, w_out.astype(BF16))


def _dilated_mixer(x, g, w_in, w_out, bsz, seq):
    slabs = _b_proj(x, g, w_in.astype(BF16))
    outs, stats = [], []
    for gi, (_, dil) in enumerate(B_GROUPS):
        o, l = _b_attn(slabs[3 * gi], slabs[3 * gi + 1], slabs[3 * gi + 2], _attn_bias(dil),
                       bsz, seq, dil)
        outs.append(o)
        stats.append(l)
    return _b_out(x, outs, stats, w_out.astype(BF16))


def kernel(x, norm_ffn1, ffn1_gate, ffn1_up, ffn1_down, norm_mix, a_w_in, a_conv_w, a_conv_b,
           a_gate_b, a_head_g, a_w_out, b_w_in, b_w_out, norm_ffn2, ffn2_gate, ffn2_up,
           ffn2_down, norm_final):
    bsz, seq, d = x.shape
    h = x.reshape(bsz * seq, d)
    row = lambda a: a.astype(F32)[None, :]
    g_final = row(norm_final)
    for i in range(DEPTH):
        h = _ffn(h, row(norm_ffn1[i]), ffn1_gate, ffn1_up, ffn1_down, i, g_final, False)
        j = i // 2
        if i % 2 == 0:
            h = _mlstm_mixer(h, row(norm_mix[i]), a_w_in[j], a_conv_w[j], a_conv_b[j],
                             a_gate_b[j], a_head_g[j], a_w_out[j], bsz, seq)
        else:
            h = _dilated_mixer(h, row(norm_mix[i]), b_w_in[j], b_w_out[j], bsz, seq)
        h = _ffn(h, row(norm_ffn2[i]), ffn2_gate, ffn2_up, ffn2_down, i, g_final, i == DEPTH - 1)
    return h.reshape(bsz, seq, d)
```

```python
import functools

import jax
import jax.numpy as jnp
from jax import lax
from jax.experimental import pallas as pl
from jax.experimental.pallas import tpu as pltpu

F32 = jnp.float32
BF16 = jnp.bfloat16

D_MODEL = 1024
DEPTH = 4
D_FF = 2816
RMS_EPS = 1e-6

A_HEADS = 4
A_DQK = 128
A_DV = 256
A_CHUNK = 128
A_CONV = 5
A_QK = 2 * A_HEADS * A_DQK
A_V = A_HEADS * A_DV
A_PAD = 16
assert A_DQK == A_CHUNK

B_GROUPS = ((128, 1), (512, 4), (2048, 16))
B_HEADS = 16
B_DH = 64
B_WIDTH = B_HEADS * B_DH
B_HALF = 64
B_TQ = 128
B_TK = B_TQ + 2 * B_HALF
B_SUB = 8
LOG2E = 1.4426950408889634
STAT_LANES = 128 // B_HEADS
NEG_BIG = -1e30

ROW_TILE = 512
W_STAGE_ROWS = 256
W_STAGE_SLOTS = 3
A_STEP_CHUNKS = 8
VMEM_LIMIT = 56 * 1024 * 1024


def _cparams(*sem):
    return pltpu.CompilerParams(dimension_semantics=sem, vmem_limit_bytes=VMEM_LIMIT)


def _resident(shape):
    nd = len(shape)
    return pl.BlockSpec(shape, lambda *_: (0,) * nd, pipeline_mode=pl.Buffered(1))


def _rms(x, g):
    ms = jnp.mean(x * x, axis=-1, keepdims=True)
    return x * lax.rsqrt(ms + RMS_EPS) * g


def _dot(a, b):
    return jnp.dot(a, b, preferred_element_type=F32)


def _dot_nt(a, b):
    return lax.dot_general(a, b, (((1,), (1,)), ((), ())), preferred_element_type=F32)


def _dot_tn(a, b):
    return lax.dot_general(a, b, (((0,), (0,)), ((), ())), preferred_element_type=F32)


def _bdot(a, b):
    return lax.dot_general(a, b, (((2,), (1,)), ((0,), (0,))), preferred_element_type=F32)


def _split3(x):
    hi = x.astype(BF16)
    r1 = x - hi.astype(F32)
    mid = r1.astype(BF16)
    lo = (r1 - mid.astype(F32)).astype(BF16)
    return hi, mid, lo


def _stage_copy(src_hbm, layer, stage_ref, sem, c):
    slots, chunk = stage_ref.shape[0], stage_ref.shape[1]
    slot = c % slots
    return pltpu.make_async_copy(src_hbm.at[layer, pl.ds(c * chunk, chunk), :],
                                 stage_ref.at[slot], sem.at[slot])


def _load_as_bf16(src_hbm, layer, dst_ref, stage_ref, sem):
    slots, chunk = stage_ref.shape[0], stage_ref.shape[1]
    n = src_hbm.shape[1] // chunk
    for c in range(min(slots - 1, n)):
        _stage_copy(src_hbm, layer, stage_ref, sem, c).start()
    for c in range(n):
        if c + slots - 1 < n:
            _stage_copy(src_hbm, layer, stage_ref, sem, c + slots - 1).start()
        _stage_copy(src_hbm, layer, stage_ref, sem, c).wait()
        dst_ref[c * chunk:(c + 1) * chunk, :] = stage_ref[c % slots].astype(BF16)


def _ffn_kernel(x_ref, g_ref, wg_hbm, wu_hbm, wd_hbm, gf_ref, o_ref,
                wg_ref, wu_ref, wd_ref, stage_in_ref, stage_out_ref, sem, *, layer, final_norm):
    @pl.when(pl.program_id(0) == 0)
    def _():
        _load_as_bf16(wg_hbm, layer, wg_ref, stage_in_ref, sem)
        _load_as_bf16(wu_hbm, layer, wu_ref, stage_in_ref, sem)
        _load_as_bf16(wd_hbm, layer, wd_ref, stage_out_ref, sem)

    x = x_ref[...]
    xn = _rms(x, g_ref[...]).astype(BF16)
    gate = _dot(xn, wg_ref[...])
    up = _dot(xn, wu_ref[...])
    act = (gate * jax.nn.sigmoid(gate) * up).astype(BF16)
    y = x + 0.5 * _dot(act, wd_ref[...])
    if final_norm:
        y = _rms(y, gf_ref[...])
    o_ref[...] = y


def _ffn(x, g, wg, wu, wd, layer, g_final, final_norm):
    m, d = x.shape
    f = wg.shape[2]
    row = pl.BlockSpec((ROW_TILE, d), lambda i: (i, 0))
    hbm = pl.BlockSpec(memory_space=pltpu.MemorySpace.HBM)
    return pl.pallas_call(
        functools.partial(_ffn_kernel, layer=layer, final_norm=final_norm),
        grid=(m // ROW_TILE,),
        in_specs=[row, _resident((1, d)), hbm, hbm, hbm, _resident((1, d))],
        out_specs=row,
        out_shape=jax.ShapeDtypeStruct((m, d), F32),
        scratch_shapes=[pltpu.VMEM((d, f), BF16), pltpu.VMEM((d, f), BF16), pltpu.VMEM((f, d), BF16),
                        pltpu.VMEM((W_STAGE_SLOTS, W_STAGE_ROWS, f), F32),
                        pltpu.VMEM((W_STAGE_SLOTS, W_STAGE_ROWS, d), F32),
                        pltpu.SemaphoreType.DMA((W_STAGE_SLOTS,))],
        compiler_params=_cparams("arbitrary"),
        name="ffn",
    )(x, g, wg, wu, wd, g_final)


A_HALO = 8


def _a_proj_kernel(x_ref, xp_ref, xn_ref, g_ref, wtok_ref, wfeat_ref,
                   bli_ref, blf_ref, bliT_ref, blfT_ref, cw_ref, cb_ref,
                   k_ref, qT_ref, vT_ref, oT_ref, li_ref, lf_ref, liT_ref, lfT_ref, *, n_steps):
    i = pl.program_id(1)
    g = g_ref[...]
    xn = _rms(x_ref[...], g)
    before = _rms(xp_ref[...], g) * (i > 0).astype(F32)
    after = _rms(xn_ref[...], g) * (i < n_steps - 1).astype(F32)
    ext = jnp.concatenate([before, xn, after], axis=0).astype(BF16)
    xb = xn.astype(BF16)
    rows = xn.shape[0]
    ng = li_ref.shape[-1]

    tok = _dot(ext, wtok_ref[...])
    qk = tok[:, :A_QK]
    gates = tok[A_HALO:A_HALO + rows, A_QK:A_QK + 2 * ng]
    li_ref[...] = gates[:, :ng] + bli_ref[...]
    lf_ref[...] = jax.nn.log_sigmoid(gates[:, ng:] + blf_ref[...])
    half = A_CONV // 2
    hq = A_QK // 2
    pieces = 4
    pr = rows // pieces
    fr = A_V // (pieces // 2)
    span = pr + 2 * A_HALO
    for c in range(pieces):
        p0 = c * pr
        slab = qk[p0:p0 + span, :]
        acc = slab[A_HALO:A_HALO + pr, :] * cw_ref[half:half + 1, :] + cb_ref[...]
        for s in range(1, half + 1):
            back = pltpu.roll(slab, s, axis=0)[A_HALO:A_HALO + pr, :]
            fwd = pltpu.roll(slab, span - s, axis=0)[A_HALO:A_HALO + pr, :]
            acc = acc + back * cw_ref[half - s:half - s + 1, :] + fwd * cw_ref[half + s:half + s + 1, :]
        y = acc * jax.nn.sigmoid(acc)
        k_ref[p0:p0 + pr, :] = y[:, hq:].astype(BF16)
        qT_ref[:, p0:p0 + pr] = (y[:, :hq] * (A_DQK ** -0.5)).T.astype(BF16)

        f0 = c * fr
        is_last = c == pieces - 1
        feat = _dot_nt(wfeat_ref[f0:(2 * A_V + 2 * ng if is_last else f0 + fr), :], xb)
        dst = vT_ref if f0 < A_V else oT_ref
        dst[f0 % A_V:f0 % A_V + fr, :] = feat[:fr, :].astype(BF16)
        if is_last:
            liT_ref[...] = feat[fr:fr + ng, :] + bliT_ref[...]
            lfT_ref[...] = jax.nn.log_sigmoid(feat[fr + ng:, :] + blfT_ref[...])


def _a_proj(x, g, w_tok, w_feat, b_li, b_lf, conv_w, conv_b, bsz, seq):
    m, d = x.shape
    ng = b_li.shape[1]
    hq = A_QK // 2
    n_steps = seq // ROW_TILE
    rh = ROW_TILE // A_HALO
    x3 = x.reshape(bsz, seq, d)
    flat = lambda b_, i: b_ * n_steps + i
    tok = lambda n: pl.BlockSpec((None, ROW_TILE, n), lambda b_, i: (b_, i, 0))
    feat = lambda n: pl.BlockSpec((n, ROW_TILE), lambda b_, i: (0, flat(b_, i)))
    return pl.pallas_call(
        functools.partial(_a_proj_kernel, n_steps=n_steps),
        grid=(bsz, n_steps),
        in_specs=[tok(d),
                  pl.BlockSpec((None, A_HALO, d), lambda b_, i: (b_, jnp.maximum(i * rh - 1, 0), 0)),
                  pl.BlockSpec((None, A_HALO, d),
                               lambda b_, i: (b_, jnp.minimum((i + 1) * rh, seq // A_HALO - 1), 0)),
                  _resident((1, d)), _resident(w_tok.shape), _resident(w_feat.shape),
                  _resident((1, ng)), _resident((1, ng)), _resident((ng, 1)), _resident((ng, 1)),
                  _resident((A_CONV, A_QK)), _resident((1, A_QK))],
        out_specs=[tok(hq), pl.BlockSpec((None, hq, ROW_TILE), lambda b_, i: (b_, 0, i)),
                   feat(A_V), feat(A_V), tok(ng), tok(ng), feat(ng), feat(ng)],
        out_shape=[jax.ShapeDtypeStruct((bsz, seq, hq), BF16),
                   jax.ShapeDtypeStruct((bsz, hq, seq), BF16),
                   jax.ShapeDtypeStruct((A_V, m), BF16),
                   jax.ShapeDtypeStruct((A_V, m), BF16),
                   jax.ShapeDtypeStruct((bsz, seq, ng), F32),
                   jax.ShapeDtypeStruct((bsz, seq, ng), F32),
                   jax.ShapeDtypeStruct((ng, m), F32),
                   jax.ShapeDtypeStruct((ng, m), F32)],
        compiler_params=_cparams("parallel", "parallel"),
        name="mlstm_proj",
    )(x3, x3, x3, g, w_tok, w_feat, b_li, b_lf, b_li.T, b_lf.T, conv_w, conv_b)


def _mlstm_kernel(qT_ref, k_ref, vT_ref, li_ref, lf_ref, liT_ref, lfT_ref, hT_ref,
                  cT_ref, n_ref, m_ref, *, reverse):
    L = A_CHUNK
    J = A_STEP_CHUNKS
    H = A_HEADS
    P = H // 2
    lo = 4 if reverse else 0

    @pl.when(pl.program_id(1) == 0)
    def _():
        cT_ref[...] = jnp.zeros_like(cT_ref)
        n_ref[...] = jnp.zeros_like(n_ref)
        m_ref[...] = jnp.zeros_like(m_ref)

    row_i = lax.broadcasted_iota(jnp.int32, (L, L), 0)
    col_i = lax.broadcasted_iota(jnp.int32, (L, L), 1)
    seen = (col_i >= row_i) if reverse else (col_i <= row_i)
    seen_t = (row_i >= col_i) if reverse else (row_i <= col_i)
    seen_bf = seen.astype(F32).astype(BF16)
    seen_t2 = jnp.concatenate([seen_t, seen_t], axis=1)
    left = lax.broadcasted_iota(jnp.int32, (L, 2 * L), 1) < L
    last = 0 if reverse else L - 1

    pair_rows = lambda rows: jnp.stack(
        [jnp.concatenate([rows[lo + 2 * p:lo + 2 * p + 1], rows[lo + 2 * p + 1:lo + 2 * p + 2]], axis=1)
         for p in range(P)])
    head_rows = lambda rows: rows[lo:lo + H][:, None, :]
    unpair = lambda a: jnp.stack([a[b][:, i * L:(i + 1) * L] for b in range(a.shape[0]) for i in range(2)])

    order = list(range(J - 1, -1, -1) if reverse else range(J))
    m_state = m_ref[...]
    d_l, g_l, w_l, sp_l, sl_l = [], [], [], [], []
    for j in order:
        r0 = j * L
        li = li_ref[r0:r0 + L, :]
        lf = lf_ref[r0:r0 + L, :]
        liT = liT_ref[:, r0:r0 + L]
        lfT = lfT_ref[:, r0:r0 + L]
        b_col = sum(_dot(seen_bf, x) for x in _split3(lf))
        b_row = sum(_dot_nt(x, seen_bf) for x in _split3(lfT))
        r_col = li - b_col
        b_tot = b_row[:, last:last + 1]
        a_row = b_tot - b_row + liT
        m_loc = jnp.max(a_row, axis=1, keepdims=True)
        w_l.append(head_rows(jnp.exp(a_row - m_loc)))
        g_l.append(pair_rows(b_row + m_state))
        m_new = jnp.maximum(b_tot + m_state, m_loc)
        sp_l.append(pair_rows(jnp.exp(b_tot + m_state - m_new)))
        sl_l.append(pair_rows(jnp.exp(m_loc - m_new)))
        m_state = m_new
        d_l.append(jnp.stack([jnp.concatenate(
            [r_col[:, lo + 2 * p + i:lo + 2 * p + i + 1] + b_row[lo + 2 * p + i:lo + 2 * p + i + 1, :]
             for i in range(2)], axis=1) for p in range(P)]))
    m_ref[...] = m_state

    cols = lambda ref, j, n: ref[:, j * L:(j + 1) * L].reshape(n, ref.shape[0] // n, L)
    qT = jnp.concatenate([cols(qT_ref, j, H) for j in order], axis=0)
    vT = jnp.concatenate([cols(vT_ref, j, H) for j in order], axis=0)
    k2 = jnp.stack([k_ref[j * L:(j + 1) * L, 2 * p * A_DQK:(2 * p + 2) * A_DQK]
                    for j in order for p in range(P)])

    w = jnp.concatenate(w_l, axis=0)
    vw_aug = jnp.concatenate([(vT.astype(F32) * w).astype(BF16),
                              jnp.broadcast_to(w, (J * H, A_PAD, L)).astype(BF16)], axis=1)
    vw4 = vw_aug.reshape(J * P, 2, A_DV + A_PAD, L)
    vw2 = jnp.concatenate([vw4[:, 0], vw4[:, 1]], axis=2)
    zk = jnp.zeros_like(k2)
    rk = jnp.concatenate([jnp.where(left[None], k2, zk), jnp.where(left[None], zk, k2)], axis=1)
    upd = _bdot(vw2, rk)

    c_state = cT_ref[...]
    n_state = n_ref[...]
    c_l, n_l = [], []
    for i in range(J):
        c_l.append(c_state)
        n_l.append(n_state)
        u = upd[i * P:(i + 1) * P]
        c_state = sp_l[i] * c_state + sl_l[i] * u[:, :A_DV, :]
        n_state = sp_l[i] * n_state + sl_l[i] * u[:, A_DV:A_DV + 1, :]
    cT_ref[...] = c_state
    n_ref[...] = n_state
    c_prev = jnp.concatenate(c_l, axis=0)
    n_prev = jnp.concatenate(n_l, axis=0)

    qT4 = qT.reshape(J * P, 2, A_DQK, L)
    zq = jnp.zeros((J * P, A_DQK, L), BF16)
    rq = jnp.concatenate([jnp.concatenate([qT4[:, 0], zq], axis=2),
                          jnp.concatenate([zq, qT4[:, 1]], axis=2)], axis=1)
    k_aug = jnp.concatenate(
        [k2, jnp.broadcast_to(n_prev, (J * P, A_PAD, 2 * A_DQK)).astype(BF16)], axis=1)
    sq = _bdot(k_aug, rq)
    qn = sq[:, L:L + 1, :]
    d = jnp.where(seen_t2[None], jnp.concatenate(d_l, axis=0), -jnp.inf)
    g = jnp.concatenate(g_l, axis=0)
    m_t = jnp.maximum(g, jnp.max(d, axis=1, keepdims=True))
    p = jnp.exp(d - m_t) * sq[:, :L, :]
    s_inter = jnp.exp(g - m_t)
    den = s_inter * qn + jnp.sum(p, axis=1, keepdims=True)
    inv = 1.0 / jnp.maximum(jnp.abs(den), jnp.exp(-m_t))
    rhs = jnp.concatenate([(qT.astype(F32) * unpair(s_inter * inv)).astype(BF16),
                           unpair((p * inv).astype(BF16))], axis=1)
    lhs = jnp.concatenate([unpair(c_prev).astype(BF16), vT], axis=2)
    out = _bdot(lhs, rhs).astype(BF16)
    for i, j in enumerate(order):
        hT_ref[:, j * L:(j + 1) * L] = out[i * H:(i + 1) * H].reshape(A_V, L)


def _mlstm(k, qT, vT, li, lf, liT, lfT, bsz, seq, reverse):
    tl = A_STEP_CHUNKS * A_CHUNK
    ns = seq // tl
    ng = li.shape[-1]
    half = A_QK // 2
    blk = (lambda i: ns - 1 - i) if reverse else (lambda i: i)
    gate = pl.BlockSpec((None, tl, ng), lambda b_, i: (b_, blk(i), 0))
    feat = lambda n: pl.BlockSpec((n, tl), lambda b_, i: (0, b_ * ns + blk(i)))
    return pl.pallas_call(
        functools.partial(_mlstm_kernel, reverse=reverse),
        grid=(bsz, ns),
        in_specs=[pl.BlockSpec((None, half, tl), lambda b_, i: (b_, 0, blk(i))),
                  pl.BlockSpec((None, tl, half), lambda b_, i: (b_, blk(i), 0)),
                  feat(A_V), gate, gate, feat(ng), feat(ng)],
        out_specs=feat(A_V),
        out_shape=jax.ShapeDtypeStruct((A_V, bsz * seq), BF16),
        scratch_shapes=[pltpu.VMEM((A_HEADS // 2, A_DV, 2 * A_DQK), F32),
                        pltpu.VMEM((A_HEADS // 2, 1, 2 * A_DQK), F32),
                        pltpu.VMEM((ng, A_CHUNK), F32)],
        compiler_params=_cparams("parallel", "arbitrary"),
        name="mlstm_bwd" if reverse else "mlstm_fwd",
    )(qT, k, vT, li, lf, liT, lfT)


def _a_out_kernel(x_ref, hfT_ref, hbT_ref, oT_ref, hg_ref, w_ref, y_ref):
    yT = hfT_ref[...].astype(F32) + hbT_ref[...].astype(F32)
    parts = []
    for h in range(A_HEADS):
        yh = yT[h * A_DV:(h + 1) * A_DV, :]
        ms = jnp.mean(yh * yh, axis=0, keepdims=True)
        parts.append(yh * lax.rsqrt(ms + RMS_EPS))
    ynT = jnp.concatenate(parts, axis=0)
    zT = (ynT * hg_ref[...] * jax.nn.sigmoid(oT_ref[...].astype(F32))).astype(BF16)
    y_ref[...] = x_ref[...] + _dot_tn(zT, w_ref[...])


def _a_out(x, hfT, hbT, oT, hg, w):
    m, d = x.shape
    row = pl.BlockSpec((ROW_TILE, d), lambda i: (i, 0))
    col = pl.BlockSpec((A_V, ROW_TILE), lambda i: (0, i))
    return pl.pallas_call(
        _a_out_kernel,
        grid=(m // ROW_TILE,),
        in_specs=[row, col, col, col, _resident((A_V, ROW_TILE)), _resident((A_V, d))],
        out_specs=row,
        out_shape=jax.ShapeDtypeStruct((m, d), F32),
        compiler_params=_cparams("parallel"),
        name="mlstm_out",
    )(x, hfT, hbT, oT, hg, w)


def _b_proj_kernel(x_ref, g_ref, w_ref, *rest):
    o_refs, xs_ref = rest[:-1], rest[-1]
    xn = _rms(x_ref[...], g_ref[...])
    nblk = D_MODEL // 128
    lhs = {1: xn.astype(BF16)}
    for c in range(nblk):
        xs_ref[c] = xn[:, c * 128:(c + 1) * 128]
    for dil in sorted({d for _, d in B_GROUPS if d > 1}):
        n = ROW_TILE // dil
        pieces = [jnp.concatenate([xs_ref[c, pl.ds(r, n, stride=dil), :] for c in range(nblk)],
                                  axis=-1).astype(BF16) for r in range(dil)]
        lhs[dil] = jnp.concatenate(pieces, axis=0)
    for j, o_ref in enumerate(o_refs):
        dil = B_GROUPS[j // 3][1]
        y = _dot(lhs[dil], w_ref[:, j * B_WIDTH:(j + 1) * B_WIDTH])
        if j % 3 == 0:
            y = y * (B_DH ** -0.5 * LOG2E)
        y = y.astype(BF16)
        n = ROW_TILE // dil
        for r in range(dil):
            o_ref[:, r * B_WIDTH:(r + 1) * B_WIDTH] = y[r * n:(r + 1) * n, :]


def _b_proj(x, g, w):
    m, d = x.shape
    dils = [dil for _, dil in B_GROUPS for _ in range(3)]
    return pl.pallas_call(
        _b_proj_kernel,
        grid=(m // ROW_TILE,),
        in_specs=[pl.BlockSpec((ROW_TILE, d), lambda i: (i, 0)), _resident((1, d)),
                  _resident(w.shape)],
        out_specs=[pl.BlockSpec((ROW_TILE // dil, dil * B_WIDTH), lambda i: (i, 0)) for dil in dils],
        out_shape=[jax.ShapeDtypeStruct((m // dil, dil * B_WIDTH), BF16) for dil in dils],
        scratch_shapes=[pltpu.VMEM((B_WIDTH // 128, ROW_TILE, 128), F32)],
        compiler_params=_cparams("parallel"),
        name="dilated_proj",
    )(x, g, w)


def _b_attn_kernel(q_ref, kp_ref, kc_ref, kn_ref, vp_ref, vc_ref, vn_ref,
                   bias_first_ref, bias_mid_ref, bias_last_ref, o_ref, stat_ref):
    k_all = jnp.concatenate([kp_ref[...], kc_ref[...], kn_ref[...]], axis=0)
    v_all = jnp.concatenate([vp_ref[...], vc_ref[...], vn_ref[...]], axis=0)
    lane = lax.broadcasted_iota(jnp.int32, (B_TQ, 128), 1)
    low = lane < B_DH
    stat_slot = lane // (STAT_LANES // 2)
    for sb in range(B_SUB):
        bias_ref = bias_first_ref if sb == 0 else (bias_last_ref if sb == B_SUB - 1 else bias_mid_ref)
        q0 = sb * B_TQ
        stat_row = jnp.zeros((B_TQ, 128), F32)
        for hp in range(B_HEADS // 2):
            cols = slice(hp * 128, (hp + 1) * 128)
            q2 = q_ref[q0:q0 + B_TQ, cols]
            k2 = k_all[q0:q0 + B_TK, cols]
            v2 = v_all[q0:q0 + B_TK, cols]
            zero = jnp.zeros_like(q2)
            outs = []
            for par, qsel in ((0, jnp.where(low, q2, zero)), (1, jnp.where(low, zero, q2))):
                hd = 2 * hp + par
                s = _dot_nt(qsel, k2) + bias_ref[hd]
                mx = jnp.max(s, axis=-1, keepdims=True)
                e = jnp.exp2(s - mx)
                den = jnp.sum(e, axis=-1, keepdims=True)
                outs.append(_dot(e.astype(BF16), v2))
                stat_row = jnp.where(stat_slot == 2 * hd, mx,
                                     jnp.where(stat_slot == 2 * hd + 1, den, stat_row))
            o_ref[q0:q0 + B_TQ, cols] = jnp.where(low, outs[0], outs[1]).astype(BF16)
        stat_ref[q0:q0 + B_TQ, :] = stat_row


def _b_attn(q, k, v, bias, bsz, seq, dil):
    n = seq // dil
    rows = B_SUB * B_TQ
    nq = n // rows
    w = B_WIDTH
    q = q.reshape(bsz, n, dil * w)
    k = k.reshape(bsz, n, dil * w)
    v = v.reshape(bsz, n, dil * w)
    hb = rows // B_HALF
    last_half = n // B_HALF - 1
    cur = pl.BlockSpec((None, rows, w), lambda b_, r, i: (b_, i, r))
    prev = pl.BlockSpec((None, B_HALF, w), lambda b_, r, i: (b_, jnp.maximum(i * hb - 1, 0), r))
    nxt = pl.BlockSpec((None, B_HALF, w), lambda b_, r, i: (b_, jnp.minimum((i + 1) * hb, last_half), r))
    at_start = lambda i: jnp.where(i == 0, 1, 0)
    at_end = lambda i: jnp.where(i == nq - 1, 2, 0)
    first_variant = (lambda i: at_start(i) + at_end(i)) if B_SUB == 1 else at_start
    bias_blk = (None, B_HEADS, B_TQ, B_TK)
    bias_first = pl.BlockSpec(bias_blk, lambda b_, r, i: (first_variant(i), 0, 0, 0))
    bias_mid = pl.BlockSpec(bias_blk, lambda b_, r, i: (0, 0, 0, 0), pipeline_mode=pl.Buffered(1))
    bias_last = pl.BlockSpec(bias_blk, lambda b_, r, i: (at_end(i), 0, 0, 0))
    out, stats = pl.pallas_call(
        _b_attn_kernel,
        grid=(bsz, dil, nq),
        in_specs=[cur, prev, cur, nxt, prev, cur, nxt, bias_first, bias_mid, bias_last],
        out_specs=[cur, pl.BlockSpec((None, rows, 128), lambda b_, r, i: (b_, i, r))],
        out_shape=[jax.ShapeDtypeStruct((bsz, n, dil * w), BF16),
                   jax.ShapeDtypeStruct((bsz, n, dil * 128), F32)],
        compiler_params=_cparams("parallel", "parallel", "arbitrary"),
        name=f"dilated_attn_d{dil}",
    )(q, k, k, k, v, v, v, bias, bias, bias)
    return out.reshape(bsz * n, dil * w), stats.reshape(bsz * n, dil * 128)


def _attn_bias(dil):
    slopes = jnp.exp2(-8.0 * (jnp.arange(B_HEADS, dtype=F32) + 1.0) / B_HEADS)
    tq = jnp.arange(B_TQ)[:, None]
    col = jnp.arange(B_TK)[None, :]
    rel = jnp.abs(col - B_HALF - tq)
    base = -slopes[:, None, None] * (rel * dil).astype(F32)[None] * LOG2E
    band = (rel <= B_HALF)[None]
    first = (col >= B_HALF)[None]
    lastv = (col < B_TQ + B_HALF)[None]
    variants = []
    for need_first, need_last in ((False, False), (True, False), (False, True), (True, True)):
        ok = band
        if need_first:
            ok = ok & first
        if need_last:
            ok = ok & lastv
        variants.append(jnp.where(ok, base, NEG_BIG))
    return jnp.stack(variants, axis=0)


def _b_out_kernel(x_ref, o0_ref, o1_ref, o2_ref, s0_ref, s1_ref, s2_ref, e_ref, w_ref, y_ref,
                  os_ref, ss_ref):
    for gi, (s_ref, (_, dil)) in enumerate(zip((s0_ref, s1_ref, s2_ref), B_GROUPS)):
        for r in range(dil if dil > 1 else 0):
            ss_ref[gi, pl.ds(r, ROW_TILE // dil, stride=dil), :] = s_ref[:, r * 128:(r + 1) * 128]
    half = STAT_LANES // 2
    mxs = [s_ref[...] if dil == 1 else ss_ref[gi]
           for gi, (s_ref, (_, dil)) in enumerate(zip((s0_ref, s1_ref, s2_ref), B_GROUPS))]
    dens = [pltpu.roll(m, 128 - half, axis=1) for m in mxs]
    top = jnp.maximum(jnp.maximum(mxs[0], mxs[1]), mxs[2])
    ts = [jnp.exp2(m - top) for m in mxs]
    total = dens[0] * ts[0] + dens[1] * ts[1] + dens[2] * ts[2]
    lane = lax.broadcasted_iota(jnp.int32, total.shape, 1)
    picked = lane % STAT_LANES == 0
    acc = None
    for gi, (t, o_ref, (_, dil)) in enumerate(zip(ts, (o0_ref, o1_ref, o2_ref), B_GROUPS)):
        weight = jnp.where(picked, t / total, 0.0)
        hi = weight.astype(BF16)
        lo = (weight - hi.astype(F32)).astype(BF16)
        wide = _dot(jnp.concatenate([hi, lo], axis=1), e_ref[...])
        nblk = B_WIDTH // 128
        if dil == 1:
            vals = o_ref[...].astype(F32)
        else:
            for r in range(dil):
                for c in range(nblk):
                    lo = r * B_WIDTH + c * 128
                    os_ref[gi, c, pl.ds(r, ROW_TILE // dil, stride=dil), :] = (
                        o_ref[:, lo:lo + 128].astype(F32))
            vals = jnp.concatenate([os_ref[gi, c] for c in range(nblk)], axis=-1)
        term = wide * vals
        acc = term if acc is None else acc + term
    y_ref[...] = x_ref[...] + _dot(acc.astype(BF16), w_ref[...])


def _b_out(x, outs, stats, w):
    m, d = x.shape
    head_of_col = jnp.arange(B_WIDTH) // B_DH
    expand = (jnp.arange(128)[:, None] == head_of_col[None, :] * STAT_LANES).astype(BF16)
    expand = jnp.concatenate([expand, expand], axis=0)
    row = pl.BlockSpec((ROW_TILE, d), lambda i: (i, 0))
    grp = lambda lanes: [pl.BlockSpec((ROW_TILE // dil, dil * lanes), lambda i: (i, 0))
                         for _, dil in B_GROUPS]
    ng = len(B_GROUPS)
    return pl.pallas_call(
        _b_out_kernel,
        grid=(m // ROW_TILE,),
        in_specs=[row, *grp(B_WIDTH), *grp(128), _resident((256, B_WIDTH)),
                  _resident((B_WIDTH, d))],
        out_specs=row,
        out_shape=jax.ShapeDtypeStruct((m, d), F32),
        scratch_shapes=[pltpu.VMEM((ng, B_WIDTH // 128, ROW_TILE, 128), F32),
                        pltpu.VMEM((ng, ROW_TILE, 128), F32)],
        compiler_params=_cparams("parallel"),
        name="dilated_out",
    )(x, *outs, *stats, expand, w)


def _mlstm_mixer(x, g, w_in, conv_w, conv_b, gate_b, head_g, w_out, bsz, seq):
    wg = w_in[:, A_QK + 2 * A_V:]
    H = A_HEADS
    pick = lambda a, s: jnp.concatenate([a[..., s * H:(s + 1) * H], a[..., (s + 2) * H:(s + 3) * H]], axis=-1)
    w_li, w_lf = pick(wg, 0), pick(wg, 1)
    gate_pad = jnp.zeros((w_in.shape[0], 128 - 4 * H), w_in.dtype)
    w_tok = jnp.concatenate([w_in[:, :A_QK], w_li, w_lf, gate_pad], axis=1).astype(BF16)
    w_feat = jnp.concatenate([w_in[:, A_QK:A_QK + 2 * A_V], w_li, w_lf], axis=1).T.astype(BF16)
    gb = gate_b.astype(F32)[None, :]
    b_li, b_lf = pick(gb, 0), pick(gb, 1)
    k, qT, vT, oT, li, lf, liT, lfT = _a_proj(x, g, w_tok, w_feat, b_li, b_lf,
                                              conv_w.astype(F32), conv_b.astype(F32)[None, :], bsz, seq)
    hfT = _mlstm(k, qT, vT, li, lf, liT, lfT, bsz, seq, reverse=False)
    hbT = _mlstm(k, qT, vT, li, lf, liT, lfT, bsz, seq, reverse=True)
    hg = jnp.broadcast_to(head_g.astype(F32)[:, None], (A_V, ROW_TILE))
    return _a_out(x, hfT, hbT, oT, hg, w_out.astype(BF16))


def _dilated_mixer(x, g, w_in, w_out, bsz, seq):
    slabs = _b_proj(x, g, w_in.astype(BF16))
    outs, stats = [], []
    for gi, (_, dil) in enumerate(B_GROUPS):
        o, l = _b_attn(slabs[3 * gi], slabs[3 * gi + 1], slabs[3 * gi + 2], _attn_bias(dil),
                       bsz, seq, dil)
        outs.append(o)
        stats.append(l)
    return _b_out(x, outs, stats, w_out.astype(BF16))


def kernel(x, norm_ffn1, ffn1_gate, ffn1_up, ffn1_down, norm_mix, a_w_in, a_conv_w, a_conv_b,
           a_gate_b, a_head_g, a_w_out, b_w_in, b_w_out, norm_ffn2, ffn2_gate, ffn2_up,
           ffn2_down, norm_final):
    bsz, seq, d = x.shape
    h = x.reshape(bsz * seq, d)
    row = lambda a: a.astype(F32)[None, :]
    g_final = row(norm_final)
    for i in range(DEPTH):
        h = _ffn(h, row(norm_ffn1[i]), ffn1_gate, ffn1_up, ffn1_down, i, g_final, False)
        j = i // 2
        if i % 2 == 0:
            h = _mlstm_mixer(h, row(norm_mix[i]), a_w_in[j], a_conv_w[j], a_conv_b[j],
                             a_gate_b[j], a_head_g[j], a_w_out[j], bsz, seq)
        else:
            h = _dilated_mixer(h, row(norm_mix[i]), b_w_in[j], b_w_out[j], bsz, seq)
        h = _ffn(h, row(norm_ffn2[i]), ffn2_gate, ffn2_up, ffn2_down, i, g_final, i == DEPTH - 1)
    return h.reshape(bsz, seq, d)
```

```python
import functools

import jax
import jax.numpy as jnp
from jax import lax
from jax.experimental import pallas as pl
from jax.experimental.pallas import tpu as pltpu

F32 = jnp.float32
BF16 = jnp.bfloat16

D_MODEL = 1024
DEPTH = 4
D_FF = 2816
RMS_EPS = 1e-6

A_HEADS = 4
A_DQK = 128
A_DV = 256
A_CHUNK = 128
A_CONV = 5
A_QK = 2 * A_HEADS * A_DQK
A_V = A_HEADS * A_DV
A_PAD = 16
assert A_DQK == A_CHUNK

B_GROUPS = ((128, 1), (512, 4), (2048, 16))
B_HEADS = 16
B_DH = 64
B_WIDTH = B_HEADS * B_DH
B_HALF = 64
B_TQ = 128
B_TK = B_TQ + 2 * B_HALF
B_SUB = 8
LOG2E = 1.4426950408889634
DEN_LANE_EVEN = 64
DEN_LANE_ODD = 16
NEG_BIG = -1e30

ROW_TILE = 512
W_STAGE_ROWS = 256
W_STAGE_SLOTS = 3
A_STEP_CHUNKS = 8
VMEM_LIMIT = 56 * 1024 * 1024


def _cparams(*sem):
    return pltpu.CompilerParams(dimension_semantics=sem, vmem_limit_bytes=VMEM_LIMIT)


def _resident(shape):
    nd = len(shape)
    return pl.BlockSpec(shape, lambda *_: (0,) * nd, pipeline_mode=pl.Buffered(1))


def _rms(x, g):
    ms = jnp.mean(x * x, axis=-1, keepdims=True)
    return x * lax.rsqrt(ms + RMS_EPS) * g


def _dot(a, b):
    return jnp.dot(a, b, preferred_element_type=F32)


def _dot_nt(a, b):
    return lax.dot_general(a, b, (((1,), (1,)), ((), ())), preferred_element_type=F32)


def _dot_tn(a, b):
    return lax.dot_general(a, b, (((0,), (0,)), ((), ())), preferred_element_type=F32)


def _bdot(a, b):
    return lax.dot_general(a, b, (((2,), (1,)), ((0,), (0,))), preferred_element_type=F32)


def _split3(x):
    hi = x.astype(BF16)
    r1 = x - hi.astype(F32)
    mid = r1.astype(BF16)
    lo = (r1 - mid.astype(F32)).astype(BF16)
    return hi, mid, lo


def _stage_copy(src_hbm, layer, stage_ref, sem, c):
    slots, chunk = stage_ref.shape[0], stage_ref.shape[1]
    slot = c % slots
    return pltpu.make_async_copy(src_hbm.at[layer, pl.ds(c * chunk, chunk), :],
                                 stage_ref.at[slot], sem.at[slot])


def _load_as_bf16(src_hbm, layer, dst_ref, stage_ref, sem):
    slots, chunk = stage_ref.shape[0], stage_ref.shape[1]
    n = src_hbm.shape[1] // chunk
    for c in range(min(slots - 1, n)):
        _stage_copy(src_hbm, layer, stage_ref, sem, c).start()
    for c in range(n):
        if c + slots - 1 < n:
            _stage_copy(src_hbm, layer, stage_ref, sem, c + slots - 1).start()
        _stage_copy(src_hbm, layer, stage_ref, sem, c).wait()
        dst_ref[c * chunk:(c + 1) * chunk, :] = stage_ref[c % slots].astype(BF16)


def _ffn_kernel(x_ref, g_ref, wg_hbm, wu_hbm, wd_hbm, gf_ref, o_ref,
                wg_ref, wu_ref, wd_ref, stage_in_ref, stage_out_ref, sem, *, layer, final_norm):
    @pl.when(pl.program_id(0) == 0)
    def _():
        _load_as_bf16(wg_hbm, layer, wg_ref, stage_in_ref, sem)
        _load_as_bf16(wu_hbm, layer, wu_ref, stage_in_ref, sem)
        _load_as_bf16(wd_hbm, layer, wd_ref, stage_out_ref, sem)

    halves = 2
    hr = x_ref.shape[0] // halves
    g = g_ref[...]
    xs = [x_ref[h * hr:(h + 1) * hr, :] for h in range(halves)]
    acts = []
    for h in range(halves):
        xn = _rms(xs[h], g).astype(BF16)
        gate = _dot(xn, wg_ref[...])
        up = _dot(xn, wu_ref[...])
        acts.append((gate * jax.nn.sigmoid(gate) * up).astype(BF16))
    for h in range(halves):
        y = xs[h] + 0.5 * _dot(acts[h], wd_ref[...])
        if final_norm:
            y = _rms(y, gf_ref[...])
        o_ref[h * hr:(h + 1) * hr, :] = y


def _ffn(x, g, wg, wu, wd, layer, g_final, final_norm):
    m, d = x.shape
    f = wg.shape[2]
    row = pl.BlockSpec((ROW_TILE, d), lambda i: (i, 0))
    hbm = pl.BlockSpec(memory_space=pltpu.MemorySpace.HBM)
    return pl.pallas_call(
        functools.partial(_ffn_kernel, layer=layer, final_norm=final_norm),
        grid=(m // ROW_TILE,),
        in_specs=[row, _resident((1, d)), hbm, hbm, hbm, _resident((1, d))],
        out_specs=row,
        out_shape=jax.ShapeDtypeStruct((m, d), F32),
        scratch_shapes=[pltpu.VMEM((d, f), BF16), pltpu.VMEM((d, f), BF16), pltpu.VMEM((f, d), BF16),
                        pltpu.VMEM((W_STAGE_SLOTS, W_STAGE_ROWS, f), F32),
                        pltpu.VMEM((W_STAGE_SLOTS, W_STAGE_ROWS, d), F32),
                        pltpu.SemaphoreType.DMA((W_STAGE_SLOTS,))],
        compiler_params=_cparams("arbitrary"),
        name="ffn",
    )(x, g, wg, wu, wd, g_final)


A_HALO = 8


def _a_proj_kernel(x_ref, xp_ref, xn_ref, g_ref, wtok_ref, wfeat_ref,
                   bli_ref, blf_ref, bliT_ref, blfT_ref, cw_ref, cb_ref,
                   k_ref, qT_ref, vT_ref, oT_ref, li_ref, lf_ref, liT_ref, lfT_ref, *, n_steps):
    i = pl.program_id(1)
    g = g_ref[...]
    xn = _rms(x_ref[...], g)
    before = _rms(xp_ref[...], g) * (i > 0).astype(F32)
    after = _rms(xn_ref[...], g) * (i < n_steps - 1).astype(F32)
    ext = jnp.concatenate([before, xn, after], axis=0).astype(BF16)
    xb = xn.astype(BF16)
    rows = xn.shape[0]
    ng = li_ref.shape[-1]

    tok = _dot(ext, wtok_ref[...])
    qk = tok[:, :A_QK]
    gates = tok[A_HALO:A_HALO + rows, A_QK:A_QK + 2 * ng]
    li_ref[...] = gates[:, :ng] + bli_ref[...]
    lf_ref[...] = jax.nn.log_sigmoid(gates[:, ng:] + blf_ref[...])
    half = A_CONV // 2
    hq = A_QK // 2
    pieces = 4
    pr = rows // pieces
    fr = A_V // (pieces // 2)
    span = pr + 2 * A_HALO
    for c in range(pieces):
        p0 = c * pr
        slab = qk[p0:p0 + span, :]
        acc = slab[A_HALO:A_HALO + pr, :] * cw_ref[half:half + 1, :] + cb_ref[...]
        for s in range(1, half + 1):
            back = pltpu.roll(slab, s, axis=0)[A_HALO:A_HALO + pr, :]
            fwd = pltpu.roll(slab, span - s, axis=0)[A_HALO:A_HALO + pr, :]
            acc = acc + back * cw_ref[half - s:half - s + 1, :] + fwd * cw_ref[half + s:half + s + 1, :]
        y = acc * jax.nn.sigmoid(acc)
        k_ref[p0:p0 + pr, :] = y[:, hq:].astype(BF16)
        qT_ref[:, p0:p0 + pr] = (y[:, :hq] * (A_DQK ** -0.5)).T.astype(BF16)

        f0 = c * fr
        is_last = c == pieces - 1
        feat = _dot_nt(wfeat_ref[f0:(2 * A_V + 2 * ng if is_last else f0 + fr), :], xb)
        dst = vT_ref if f0 < A_V else oT_ref
        dst[f0 % A_V:f0 % A_V + fr, :] = feat[:fr, :].astype(BF16)
        if is_last:
            liT_ref[...] = feat[fr:fr + ng, :] + bliT_ref[...]
            lfT_ref[...] = jax.nn.log_sigmoid(feat[fr + ng:, :] + blfT_ref[...])


def _a_proj(x, g, w_tok, w_feat, b_li, b_lf, conv_w, conv_b, bsz, seq):
    m, d = x.shape
    ng = b_li.shape[1]
    hq = A_QK // 2
    n_steps = seq // ROW_TILE
    rh = ROW_TILE // A_HALO
    x3 = x.reshape(bsz, seq, d)
    flat = lambda b_, i: b_ * n_steps + i
    tok = lambda n: pl.BlockSpec((None, ROW_TILE, n), lambda b_, i: (b_, i, 0))
    feat = lambda n: pl.BlockSpec((n, ROW_TILE), lambda b_, i: (0, flat(b_, i)))
    return pl.pallas_call(
        functools.partial(_a_proj_kernel, n_steps=n_steps),
        grid=(bsz, n_steps),
        in_specs=[tok(d),
                  pl.BlockSpec((None, A_HALO, d), lambda b_, i: (b_, jnp.maximum(i * rh - 1, 0), 0)),
                  pl.BlockSpec((None, A_HALO, d),
                               lambda b_, i: (b_, jnp.minimum((i + 1) * rh, seq // A_HALO - 1), 0)),
                  _resident((1, d)), _resident(w_tok.shape), _resident(w_feat.shape),
                  _resident((1, ng)), _resident((1, ng)), _resident((ng, 1)), _resident((ng, 1)),
                  _resident((A_CONV, A_QK)), _resident((1, A_QK))],
        out_specs=[tok(hq), pl.BlockSpec((None, hq, ROW_TILE), lambda b_, i: (b_, 0, i)),
                   feat(A_V), feat(A_V), tok(ng), tok(ng), feat(ng), feat(ng)],
        out_shape=[jax.ShapeDtypeStruct((bsz, seq, hq), BF16),
                   jax.ShapeDtypeStruct((bsz, hq, seq), BF16),
                   jax.ShapeDtypeStruct((A_V, m), BF16),
                   jax.ShapeDtypeStruct((A_V, m), BF16),
                   jax.ShapeDtypeStruct((bsz, seq, ng), F32),
                   jax.ShapeDtypeStruct((bsz, seq, ng), F32),
                   jax.ShapeDtypeStruct((ng, m), F32),
                   jax.ShapeDtypeStruct((ng, m), F32)],
        compiler_params=_cparams("parallel", "parallel"),
        name="mlstm_proj",
    )(x3, x3, x3, g, w_tok, w_feat, b_li, b_lf, b_li.T, b_lf.T, conv_w, conv_b)


def _mlstm_kernel(qT_ref, k_ref, vT_ref, li_ref, lf_ref, liT_ref, lfT_ref, hT_ref,
                  cT_ref, n_ref, m_ref, *, reverse):
    L = A_CHUNK
    J = A_STEP_CHUNKS
    H = A_HEADS
    P = H // 2
    lo = 4 if reverse else 0

    @pl.when(pl.program_id(1) == 0)
    def _():
        cT_ref[...] = jnp.zeros_like(cT_ref)
        n_ref[...] = jnp.zeros_like(n_ref)
        m_ref[...] = jnp.zeros_like(m_ref)

    row_i = lax.broadcasted_iota(jnp.int32, (L, L), 0)
    col_i = lax.broadcasted_iota(jnp.int32, (L, L), 1)
    seen = (col_i >= row_i) if reverse else (col_i <= row_i)
    seen_t = (row_i >= col_i) if reverse else (row_i <= col_i)
    seen_bf = seen.astype(F32).astype(BF16)
    seen3 = jnp.concatenate([seen_bf, seen_bf, seen_bf], axis=1)
    seen_t2 = jnp.concatenate([seen_t, seen_t], axis=1)
    left = lax.broadcasted_iota(jnp.int32, (L, 2 * L), 1) < L
    last = 0 if reverse else L - 1

    pair_rows = lambda rows: jnp.stack(
        [jnp.concatenate([rows[lo + 2 * p:lo + 2 * p + 1], rows[lo + 2 * p + 1:lo + 2 * p + 2]], axis=1)
         for p in range(P)])
    head_rows = lambda rows: rows[lo:lo + H][:, None, :]
    unpair = lambda a: jnp.stack([a[b][:, i * L:(i + 1) * L] for b in range(a.shape[0]) for i in range(2)])

    order = list(range(J - 1, -1, -1) if reverse else range(J))
    m_state = m_ref[...]
    d_l, g_l, w_l, sp_l, sl_l = [], [], [], [], []
    for j in order:
        r0 = j * L
        li = li_ref[r0:r0 + L, :]
        lf = lf_ref[r0:r0 + L, :]
        liT = liT_ref[:, r0:r0 + L]
        lfT = lfT_ref[:, r0:r0 + L]
        b_col = _dot(seen3, jnp.concatenate(_split3(lf), axis=0))
        b_row = _dot_nt(jnp.concatenate(_split3(lfT), axis=1), seen3)
        r_col = li - b_col
        b_tot = b_row[:, last:last + 1]
        a_row = b_tot - b_row + liT
        m_loc = jnp.max(a_row, axis=1, keepdims=True)
        w_l.append(head_rows(jnp.exp(a_row - m_loc)))
        g_l.append(pair_rows(b_row + m_state))
        m_new = jnp.maximum(b_tot + m_state, m_loc)
        sp_l.append(pair_rows(jnp.exp(b_tot + m_state - m_new)))
        sl_l.append(pair_rows(jnp.exp(m_loc - m_new)))
        m_state = m_new
        d_l.append(jnp.stack([jnp.concatenate(
            [r_col[:, lo + 2 * p + i:lo + 2 * p + i + 1] + b_row[lo + 2 * p + i:lo + 2 * p + i + 1, :]
             for i in range(2)], axis=1) for p in range(P)]))
    m_ref[...] = m_state

    cols = lambda ref, j, n: ref[:, j * L:(j + 1) * L].reshape(n, ref.shape[0] // n, L)
    qT = jnp.concatenate([cols(qT_ref, j, H) for j in order], axis=0)
    vT = jnp.concatenate([cols(vT_ref, j, H) for j in order], axis=0)
    k2 = jnp.stack([k_ref[j * L:(j + 1) * L, 2 * p * A_DQK:(2 * p + 2) * A_DQK]
                    for j in order for p in range(P)])

    w = jnp.concatenate(w_l, axis=0)
    vw_aug = jnp.concatenate([(vT.astype(F32) * w).astype(BF16),
                              jnp.broadcast_to(w, (J * H, A_PAD, L)).astype(BF16)], axis=1)
    vw4 = vw_aug.reshape(J * P, 2, A_DV + A_PAD, L)
    vw2 = jnp.concatenate([vw4[:, 0], vw4[:, 1]], axis=2)
    zk = jnp.zeros_like(k2)
    rk = jnp.concatenate([jnp.where(left[None], k2, zk), jnp.where(left[None], zk, k2)], axis=1)
    upd = _bdot(vw2, rk)

    c_state = cT_ref[...]
    n_state = n_ref[...]
    c_l, n_l = [], []
    for i in range(J):
        c_l.append(c_state)
        n_l.append(n_state)
        u = upd[i * P:(i + 1) * P]
        c_state = sp_l[i] * c_state + sl_l[i] * u[:, :A_DV, :]
        n_state = sp_l[i] * n_state + sl_l[i] * u[:, A_DV:A_DV + 1, :]
    cT_ref[...] = c_state
    n_ref[...] = n_state
    c_prev = jnp.concatenate(c_l, axis=0)
    n_prev = jnp.concatenate(n_l, axis=0)

    qT4 = qT.reshape(J * P, 2, A_DQK, L)
    zq = jnp.zeros((J * P, A_DQK, L), BF16)
    rq = jnp.concatenate([jnp.concatenate([qT4[:, 0], zq], axis=2),
                          jnp.concatenate([zq, qT4[:, 1]], axis=2)], axis=1)
    k_aug = jnp.concatenate(
        [k2, jnp.broadcast_to(n_prev, (J * P, A_PAD, 2 * A_DQK)).astype(BF16)], axis=1)
    sq = _bdot(k_aug, rq)
    qn = sq[:, L:L + 1, :]
    d = jnp.where(seen_t2[None], jnp.concatenate(d_l, axis=0), -jnp.inf)
    g = jnp.concatenate(g_l, axis=0)
    m_t = jnp.maximum(g, jnp.max(d, axis=1, keepdims=True))
    p = jnp.exp(d - m_t) * sq[:, :L, :]
    s_inter = jnp.exp(g - m_t)
    den = s_inter * qn + jnp.sum(p, axis=1, keepdims=True)
    inv = 1.0 / jnp.maximum(jnp.abs(den), jnp.exp(-m_t))
    rhs = jnp.concatenate([(qT.astype(F32) * unpair(s_inter * inv)).astype(BF16),
                           unpair((p * inv).astype(BF16))], axis=1)
    lhs = jnp.concatenate([unpair(c_prev).astype(BF16), vT], axis=2)
    out = _bdot(lhs, rhs).astype(BF16)
    for i, j in enumerate(order):
        hT_ref[:, j * L:(j + 1) * L] = out[i * H:(i + 1) * H].reshape(A_V, L)


def _mlstm(k, qT, vT, li, lf, liT, lfT, bsz, seq, reverse):
    tl = A_STEP_CHUNKS * A_CHUNK
    ns = seq // tl
    ng = li.shape[-1]
    half = A_QK // 2
    blk = (lambda i: ns - 1 - i) if reverse else (lambda i: i)
    gate = pl.BlockSpec((None, tl, ng), lambda b_, i: (b_, blk(i), 0))
    feat = lambda n: pl.BlockSpec((n, tl), lambda b_, i: (0, b_ * ns + blk(i)))
    return pl.pallas_call(
        functools.partial(_mlstm_kernel, reverse=reverse),
        grid=(bsz, ns),
        in_specs=[pl.BlockSpec((None, half, tl), lambda b_, i: (b_, 0, blk(i))),
                  pl.BlockSpec((None, tl, half), lambda b_, i: (b_, blk(i), 0)),
                  feat(A_V), gate, gate, feat(ng), feat(ng)],
        out_specs=feat(A_V),
        out_shape=jax.ShapeDtypeStruct((A_V, bsz * seq), BF16),
        scratch_shapes=[pltpu.VMEM((A_HEADS // 2, A_DV, 2 * A_DQK), F32),
                        pltpu.VMEM((A_HEADS // 2, 1, 2 * A_DQK), F32),
                        pltpu.VMEM((ng, A_CHUNK), F32)],
        compiler_params=_cparams("parallel", "arbitrary"),
        name="mlstm_bwd" if reverse else "mlstm_fwd",
    )(qT, k, vT, li, lf, liT, lfT)


def _a_out_kernel(x_ref, hfT_ref, hbT_ref, oT_ref, hg_ref, w_ref, y_ref):
    yT = hfT_ref[...].astype(F32) + hbT_ref[...].astype(F32)
    parts = []
    for h in range(A_HEADS):
        yh = yT[h * A_DV:(h + 1) * A_DV, :]
        ms = jnp.mean(yh * yh, axis=0, keepdims=True)
        parts.append(yh * lax.rsqrt(ms + RMS_EPS))
    ynT = jnp.concatenate(parts, axis=0)
    zT = (ynT * hg_ref[...] * jax.nn.sigmoid(oT_ref[...].astype(F32))).astype(BF16)
    y_ref[...] = x_ref[...] + _dot_tn(zT, w_ref[...])


def _a_out(x, hfT, hbT, oT, hg, w):
    m, d = x.shape
    row = pl.BlockSpec((ROW_TILE, d), lambda i: (i, 0))
    col = pl.BlockSpec((A_V, ROW_TILE), lambda i: (0, i))
    return pl.pallas_call(
        _a_out_kernel,
        grid=(m // ROW_TILE,),
        in_specs=[row, col, col, col, _resident((A_V, ROW_TILE)), _resident((A_V, d))],
        out_specs=row,
        out_shape=jax.ShapeDtypeStruct((m, d), F32),
        compiler_params=_cparams("parallel"),
        name="mlstm_out",
    )(x, hfT, hbT, oT, hg, w)


def _b_proj_kernel(x_ref, g_ref, w_ref, *rest):
    o_refs, xs_ref = rest[:-1], rest[-1]
    xn = _rms(x_ref[...], g_ref[...])
    nblk = D_MODEL // 128
    lhs = {1: xn.astype(BF16)}
    for c in range(nblk):
        xs_ref[c] = xn[:, c * 128:(c + 1) * 128]
    for dil in sorted({d for _, d in B_GROUPS if d > 1}):
        n = ROW_TILE // dil
        pieces = [jnp.concatenate([xs_ref[c, pl.ds(r, n, stride=dil), :] for c in range(nblk)],
                                  axis=-1).astype(BF16) for r in range(dil)]
        lhs[dil] = jnp.concatenate(pieces, axis=0)
    for j, o_ref in enumerate(o_refs):
        dil = B_GROUPS[j // 3][1]
        y = _dot(lhs[dil], w_ref[:, j * B_WIDTH:(j + 1) * B_WIDTH])
        if j % 3 == 0:
            y = y * (B_DH ** -0.5 * LOG2E)
        y = y.astype(BF16)
        n = ROW_TILE // dil
        for r in range(dil):
            o_ref[:, r * B_WIDTH:(r + 1) * B_WIDTH] = y[r * n:(r + 1) * n, :]


def _b_proj(x, g, w):
    m, d = x.shape
    dils = [dil for _, dil in B_GROUPS for _ in range(3)]
    return pl.pallas_call(
        _b_proj_kernel,
        grid=(m // ROW_TILE,),
        in_specs=[pl.BlockSpec((ROW_TILE, d), lambda i: (i, 0)), _resident((1, d)),
                  _resident(w.shape)],
        out_specs=[pl.BlockSpec((ROW_TILE // dil, dil * B_WIDTH), lambda i: (i, 0)) for dil in dils],
        out_shape=[jax.ShapeDtypeStruct((m // dil, dil * B_WIDTH), BF16) for dil in dils],
        scratch_shapes=[pltpu.VMEM((B_WIDTH // 128, ROW_TILE, 128), F32)],
        compiler_params=_cparams("parallel"),
        name="dilated_proj",
    )(x, g, w)


def _b_attn_kernel(q_ref, kp_ref, kc_ref, kn_ref, vp_ref, vc_ref, vn_ref,
                   bias_first_ref, bias_mid_ref, bias_last_ref, o_ref, stat_ref):
    k_all = jnp.concatenate([kp_ref[...], kc_ref[...], kn_ref[...]], axis=0)
    v_all = jnp.concatenate([vp_ref[...], vc_ref[...], vn_ref[...]], axis=0)
    lane = lax.broadcasted_iota(jnp.int32, (B_TQ, 128), 1)
    low = lane < B_DH
    low_k = lax.broadcasted_iota(jnp.int32, (B_TK, 128), 1) < B_DH
    for sb in range(B_SUB):
        bias_ref = bias_first_ref if sb == 0 else (bias_last_ref if sb == B_SUB - 1 else bias_mid_ref)
        q0 = sb * B_TQ
        stat_row = jnp.zeros((B_TQ, 128), F32)
        for hp in range(B_HEADS // 2):
            cols = slice(hp * 128, (hp + 1) * 128)
            q2 = q_ref[q0:q0 + B_TQ, cols]
            k2 = k_all[q0:q0 + B_TK, cols]
            v2 = v_all[q0:q0 + B_TK, cols]
            zero = jnp.zeros_like(q2)
            ones = jnp.ones_like(v2)
            outs = []
            for par, qsel, vsel in ((0, jnp.where(low, q2, zero), jnp.where(low_k, v2, ones)),
                                    (1, jnp.where(low, zero, q2), jnp.where(low_k, ones, v2))):
                hd = 2 * hp + par
                s = _dot_nt(qsel, k2) + bias_ref[hd]
                mx = jnp.max(s, axis=-1, keepdims=True)
                e = jnp.exp2(s - mx)
                res = _dot(e.astype(BF16), vsel)
                outs.append(res)
                den_lane = (DEN_LANE_ODD if par else DEN_LANE_EVEN) + hd
                stat_row = jnp.where(lane == hd, mx, jnp.where(lane == den_lane, res, stat_row))
            o_ref[q0:q0 + B_TQ, cols] = jnp.where(low, outs[0], outs[1]).astype(BF16)
        stat_ref[q0:q0 + B_TQ, :] = stat_row


def _b_attn(q, k, v, bias, bsz, seq, dil):
    n = seq // dil
    rows = B_SUB * B_TQ
    nq = n // rows
    w = B_WIDTH
    q = q.reshape(bsz, n, dil * w)
    k = k.reshape(bsz, n, dil * w)
    v = v.reshape(bsz, n, dil * w)
    hb = rows // B_HALF
    last_half = n // B_HALF - 1
    cur = pl.BlockSpec((None, rows, w), lambda b_, r, i: (b_, i, r))
    prev = pl.BlockSpec((None, B_HALF, w), lambda b_, r, i: (b_, jnp.maximum(i * hb - 1, 0), r))
    nxt = pl.BlockSpec((None, B_HALF, w), lambda b_, r, i: (b_, jnp.minimum((i + 1) * hb, last_half), r))
    at_start = lambda i: jnp.where(i == 0, 1, 0)
    at_end = lambda i: jnp.where(i == nq - 1, 2, 0)
    first_variant = (lambda i: at_start(i) + at_end(i)) if B_SUB == 1 else at_start
    bias_blk = (None, B_HEADS, B_TQ, B_TK)
    bias_first = pl.BlockSpec(bias_blk, lambda b_, r, i: (first_variant(i), 0, 0, 0))
    bias_mid = pl.BlockSpec(bias_blk, lambda b_, r, i: (0, 0, 0, 0), pipeline_mode=pl.Buffered(1))
    bias_last = pl.BlockSpec(bias_blk, lambda b_, r, i: (at_end(i), 0, 0, 0))
    out, stats = pl.pallas_call(
        _b_attn_kernel,
        grid=(bsz, dil, nq),
        in_specs=[cur, prev, cur, nxt, prev, cur, nxt, bias_first, bias_mid, bias_last],
        out_specs=[cur, pl.BlockSpec((None, rows, 128), lambda b_, r, i: (b_, i, r))],
        out_shape=[jax.ShapeDtypeStruct((bsz, n, dil * w), BF16),
                   jax.ShapeDtypeStruct((bsz, n, dil * 128), F32)],
        compiler_params=_cparams("parallel", "parallel", "arbitrary"),
        name=f"dilated_attn_d{dil}",
    )(q, k, k, k, v, v, v, bias, bias, bias)
    return out.reshape(bsz * n, dil * w), stats.reshape(bsz * n, dil * 128)


def _attn_bias(dil):
    slopes = jnp.exp2(-8.0 * (jnp.arange(B_HEADS, dtype=F32) + 1.0) / B_HEADS)
    tq = jnp.arange(B_TQ)[:, None]
    col = jnp.arange(B_TK)[None, :]
    rel = jnp.abs(col - B_HALF - tq)
    base = -slopes[:, None, None] * (rel * dil).astype(F32)[None] * LOG2E
    band = (rel <= B_HALF)[None]
    first = (col >= B_HALF)[None]
    lastv = (col < B_TQ + B_HALF)[None]
    variants = []
    for need_first, need_last in ((False, False), (True, False), (False, True), (True, True)):
        ok = band
        if need_first:
            ok = ok & first
        if need_last:
            ok = ok & lastv
        variants.append(jnp.where(ok, base, NEG_BIG))
    return jnp.stack(variants, axis=0)


def _b_out_kernel(x_ref, o0_ref, o1_ref, o2_ref, s0_ref, s1_ref, s2_ref, e_ref, w_ref, y_ref,
                  os_ref, ss_ref):
    for gi, (s_ref, (_, dil)) in enumerate(zip((s0_ref, s1_ref, s2_ref), B_GROUPS)):
        for r in range(dil if dil > 1 else 0):
            ss_ref[gi, pl.ds(r, ROW_TILE // dil, stride=dil), :] = s_ref[:, r * 128:(r + 1) * 128]
    mxs = [s_ref[...] if dil == 1 else ss_ref[gi]
           for gi, (s_ref, (_, dil)) in enumerate(zip((s0_ref, s1_ref, s2_ref), B_GROUPS))]
    lane = lax.broadcasted_iota(jnp.int32, mxs[0].shape, 1)
    dens = [jnp.where(lane % 2 == 0, pltpu.roll(m, 128 - DEN_LANE_EVEN, axis=1),
                      pltpu.roll(m, 128 - DEN_LANE_ODD, axis=1)) for m in mxs]
    top = jnp.maximum(jnp.maximum(mxs[0], mxs[1]), mxs[2])
    ts = [jnp.exp2(m - top) for m in mxs]
    total = dens[0] * ts[0] + dens[1] * ts[1] + dens[2] * ts[2]
    picked = lane < B_HEADS
    acc = None
    for gi, (t, o_ref, (_, dil)) in enumerate(zip(ts, (o0_ref, o1_ref, o2_ref), B_GROUPS)):
        weight = jnp.where(picked, t / total, 0.0)
        hi = weight.astype(BF16)
        lo = (weight - hi.astype(F32)).astype(BF16)
        wide = _dot(jnp.concatenate([hi, lo], axis=1), e_ref[...])
        nblk = B_WIDTH // 128
        if dil == 1:
            vals = o_ref[...].astype(F32)
        else:
            for r in range(dil):
                for c in range(nblk):
                    lo = r * B_WIDTH + c * 128
                    os_ref[gi, c, pl.ds(r, ROW_TILE // dil, stride=dil), :] = (
                        o_ref[:, lo:lo + 128].astype(F32))
            vals = jnp.concatenate([os_ref[gi, c] for c in range(nblk)], axis=-1)
        term = wide * vals
        acc = term if acc is None else acc + term
    y_ref[...] = x_ref[...] + _dot(acc.astype(BF16), w_ref[...])


def _b_out(x, outs, stats, w):
    m, d = x.shape
    head_of_col = jnp.arange(B_WIDTH) // B_DH
    expand = (jnp.arange(128)[:, None] == head_of_col[None, :]).astype(BF16)
    expand = jnp.concatenate([expand, expand], axis=0)
    row = pl.BlockSpec((ROW_TILE, d), lambda i: (i, 0))
    grp = lambda lanes: [pl.BlockSpec((ROW_TILE // dil, dil * lanes), lambda i: (i, 0))
                         for _, dil in B_GROUPS]
    ng = len(B_GROUPS)
    return pl.pallas_call(
        _b_out_kernel,
        grid=(m // ROW_TILE,),
        in_specs=[row, *grp(B_WIDTH), *grp(128), _resident((256, B_WIDTH)),
                  _resident((B_WIDTH, d))],
        out_specs=row,
        out_shape=jax.ShapeDtypeStruct((m, d), F32),
        scratch_shapes=[pltpu.VMEM((ng, B_WIDTH // 128, ROW_TILE, 128), F32),
                        pltpu.VMEM((ng, ROW_TILE, 128), F32)],
        compiler_params=_cparams("parallel"),
        name="dilated_out",
    )(x, *outs, *stats, expand, w)


def _mlstm_mixer(x, g, w_in, conv_w, conv_b, gate_b, head_g, w_out, bsz, seq):
    wg = w_in[:, A_QK + 2 * A_V:]
    H = A_HEADS
    pick = lambda a, s: jnp.concatenate([a[..., s * H:(s + 1) * H], a[..., (s + 2) * H:(s + 3) * H]], axis=-1)
    w_li, w_lf = pick(wg, 0), pick(wg, 1)
    gate_pad = jnp.zeros((w_in.shape[0], 128 - 4 * H), w_in.dtype)
    w_tok = jnp.concatenate([w_in[:, :A_QK], w_li, w_lf, gate_pad], axis=1).astype(BF16)
    w_feat = jnp.concatenate([w_in[:, A_QK:A_QK + 2 * A_V], w_li, w_lf], axis=1).T.astype(BF16)
    gb = gate_b.astype(F32)[None, :]
    b_li, b_lf = pick(gb, 0), pick(gb, 1)
    k, qT, vT, oT, li, lf, liT, lfT = _a_proj(x, g, w_tok, w_feat, b_li, b_lf,
                                              conv_w.astype(F32), conv_b.astype(F32)[None, :], bsz, seq)
    hfT = _mlstm(k, qT, vT, li, lf, liT, lfT, bsz, seq, reverse=False)
    hbT = _mlstm(k, qT, vT, li, lf, liT, lfT, bsz, seq, reverse=True)
    hg = jnp.broadcast_to(head_g.astype(F32)[:, None], (A_V, ROW_TILE))
    return _a_out(x, hfT, hbT, oT, hg, w_out.astype(BF16))


def _dilated_mixer(x, g, w_in, w_out, bsz, seq):
    slabs = _b_proj(x, g, w_in.astype(BF16))
    outs, stats = [], []
    for gi, (_, dil) in enumerate(B_GROUPS):
        o, l = _b_attn(slabs[3 * gi], slabs[3 * gi + 1], slabs[3 * gi + 2], _attn_bias(dil),
                       bsz, seq, dil)
        outs.append(o)
        stats.append(l)
    return _b_out(x, outs, stats, w_out.astype(BF16))


def kernel(x, norm_ffn1, ffn1_gate, ffn1_up, ffn1_down, norm_mix, a_w_in, a_conv_w, a_conv_b,
           a_gate_b, a_head_g, a_w_out, b_w_in, b_w_out, norm_ffn2, ffn2_gate, ffn2_up,
           ffn2_down, norm_final):
    bsz, seq, d = x.shape
    h = x.reshape(bsz * seq, d)
    row = lambda a: a.astype(F32)[None, :]
    g_final = row(norm_final)
    for i in range(DEPTH):
        h = _ffn(h, row(norm_ffn1[i]), ffn1_gate, ffn1_up, ffn1_down, i, g_final, False)
        j = i // 2
        if i % 2 == 0:
            h = _mlstm_mixer(h, row(norm_mix[i]), a_w_in[j], a_conv_w[j], a_conv_b[j],
                             a_gate_b[j], a_head_g[j], a_w_out[j], bsz, seq)
        else:
            h = _dilated_mixer(h, row(norm_mix[i]), b_w_in[j], b_w_out[j], bsz, seq)
        h = _ffn(h, row(norm_ffn2[i]), ffn2_gate, ffn2_up, ffn2_down, i, g_final, i == DEPTH - 1)
    return h.reshape(bsz, seq, d)
```

```python
import functools

import jax
import jax.numpy as jnp
from jax import lax
from jax.experimental import pallas as pl
from jax.experimental.pallas import tpu as pltpu

F32 = jnp.float32
BF16 = jnp.bfloat16

D_MODEL = 1024
DEPTH = 4
D_FF = 2816
RMS_EPS = 1e-6

A_HEADS = 4
A_DQK = 128
A_DV = 256
A_CHUNK = 128
A_CONV = 5
A_QK = 2 * A_HEADS * A_DQK
A_V = A_HEADS * A_DV
A_PAD = 16
assert A_DQK == A_CHUNK

B_GROUPS = ((128, 1), (512, 4), (2048, 16))
B_HEADS = 16
B_DH = 64
B_WIDTH = B_HEADS * B_DH
B_HALF = 64
B_TQ = 128
B_TK = B_TQ + 2 * B_HALF
B_SUB = 8
LOG2E = 1.4426950408889634
DEN_LANE_EVEN = 64
DEN_LANE_ODD = 16
NEG_BIG = -1e30

ROW_TILE = 512
W_STAGE_ROWS = 256
FFN_TILE = 1024
W_STAGE_SLOTS = 3
A_STEP_CHUNKS = 8
VMEM_LIMIT = 56 * 1024 * 1024


def _cparams(*sem):
    return pltpu.CompilerParams(dimension_semantics=sem, vmem_limit_bytes=VMEM_LIMIT)


def _resident(shape):
    nd = len(shape)
    return pl.BlockSpec(shape, lambda *_: (0,) * nd, pipeline_mode=pl.Buffered(1))


def _rms(x, g):
    ms = jnp.mean(x * x, axis=-1, keepdims=True)
    return x * lax.rsqrt(ms + RMS_EPS) * g


def _dot(a, b):
    return jnp.dot(a, b, preferred_element_type=F32)


def _dot_nt(a, b):
    return lax.dot_general(a, b, (((1,), (1,)), ((), ())), preferred_element_type=F32)


def _dot_tn(a, b):
    return lax.dot_general(a, b, (((0,), (0,)), ((), ())), preferred_element_type=F32)


def _bdot(a, b):
    return lax.dot_general(a, b, (((2,), (1,)), ((0,), (0,))), preferred_element_type=F32)


def _split3(x):
    hi = x.astype(BF16)
    r1 = x - hi.astype(F32)
    mid = r1.astype(BF16)
    lo = (r1 - mid.astype(F32)).astype(BF16)
    return hi, mid, lo


def _stage_copy(src_hbm, layer, stage_ref, sem, c):
    slots, chunk = stage_ref.shape[0], stage_ref.shape[1]
    slot = c % slots
    return pltpu.make_async_copy(src_hbm.at[layer, pl.ds(c * chunk, chunk), :],
                                 stage_ref.at[slot], sem.at[slot])


def _load_as_bf16(src_hbm, layer, dst_ref, stage_ref, sem):
    slots, chunk = stage_ref.shape[0], stage_ref.shape[1]
    n = src_hbm.shape[1] // chunk
    for c in range(min(slots - 1, n)):
        _stage_copy(src_hbm, layer, stage_ref, sem, c).start()
    for c in range(n):
        if c + slots - 1 < n:
            _stage_copy(src_hbm, layer, stage_ref, sem, c + slots - 1).start()
        _stage_copy(src_hbm, layer, stage_ref, sem, c).wait()
        dst_ref[c * chunk:(c + 1) * chunk, :] = stage_ref[c % slots].astype(BF16)


def _ffn_kernel(x_ref, g_ref, wg_hbm, wu_hbm, wd_hbm, gf_ref, o_ref,
                wg_ref, wu_ref, wd_ref, stage_in_ref, stage_out_ref, sem, *, layer, final_norm):
    @pl.when(pl.program_id(0) == 0)
    def _():
        _load_as_bf16(wg_hbm, layer, wg_ref, stage_in_ref, sem)
        _load_as_bf16(wu_hbm, layer, wu_ref, stage_in_ref, sem)
        _load_as_bf16(wd_hbm, layer, wd_ref, stage_out_ref, sem)

    halves = FFN_TILE // 256
    hr = x_ref.shape[0] // halves
    g = g_ref[...]
    xs = [x_ref[h * hr:(h + 1) * hr, :] for h in range(halves)]
    acts = []
    for h in range(halves):
        xn = _rms(xs[h], g).astype(BF16)
        gate = _dot(xn, wg_ref[...])
        up = _dot(xn, wu_ref[...])
        acts.append((gate * jax.nn.sigmoid(gate) * up).astype(BF16))
    for h in range(halves):
        y = xs[h] + 0.5 * _dot(acts[h], wd_ref[...])
        if final_norm:
            y = _rms(y, gf_ref[...])
        o_ref[h * hr:(h + 1) * hr, :] = y


def _ffn(x, g, wg, wu, wd, layer, g_final, final_norm):
    m, d = x.shape
    f = wg.shape[2]
    row = pl.BlockSpec((FFN_TILE, d), lambda i: (i, 0))
    hbm = pl.BlockSpec(memory_space=pltpu.MemorySpace.HBM)
    return pl.pallas_call(
        functools.partial(_ffn_kernel, layer=layer, final_norm=final_norm),
        grid=(m // FFN_TILE,),
        in_specs=[row, _resident((1, d)), hbm, hbm, hbm, _resident((1, d))],
        out_specs=row,
        out_shape=jax.ShapeDtypeStruct((m, d), F32),
        scratch_shapes=[pltpu.VMEM((d, f), BF16), pltpu.VMEM((d, f), BF16), pltpu.VMEM((f, d), BF16),
                        pltpu.VMEM((W_STAGE_SLOTS, W_STAGE_ROWS, f), F32),
                        pltpu.VMEM((W_STAGE_SLOTS, W_STAGE_ROWS, d), F32),
                        pltpu.SemaphoreType.DMA((W_STAGE_SLOTS,))],
        compiler_params=_cparams("arbitrary"),
        name="ffn",
    )(x, g, wg, wu, wd, g_final)


A_HALO = 8


def _a_proj_kernel(x_ref, xp_ref, xn_ref, g_ref, wtok_ref, wfeat_ref,
                   bli_ref, blf_ref, bliT_ref, blfT_ref, cw_ref, cb_ref,
                   k_ref, qT_ref, vT_ref, oT_ref, li_ref, lf_ref, liT_ref, lfT_ref, *, n_steps):
    i = pl.program_id(1)
    g = g_ref[...]
    xn = _rms(x_ref[...], g)
    before = _rms(xp_ref[...], g) * (i > 0).astype(F32)
    after = _rms(xn_ref[...], g) * (i < n_steps - 1).astype(F32)
    ext = jnp.concatenate([before, xn, after], axis=0).astype(BF16)
    xb = xn.astype(BF16)
    rows = xn.shape[0]
    ng = li_ref.shape[-1]

    tok = _dot(ext, wtok_ref[...])
    qk = tok[:, :A_QK]
    gates = tok[A_HALO:A_HALO + rows, A_QK:A_QK + 2 * ng]
    li_ref[...] = gates[:, :ng] + bli_ref[...]
    lf_ref[...] = jax.nn.log_sigmoid(gates[:, ng:] + blf_ref[...])
    half = A_CONV // 2
    hq = A_QK // 2
    pieces = 4
    pr = rows // pieces
    fr = A_V // (pieces // 2)
    span = pr + 2 * A_HALO
    for c in range(pieces):
        p0 = c * pr
        slab = qk[p0:p0 + span, :]
        acc = slab[A_HALO:A_HALO + pr, :] * cw_ref[half:half + 1, :] + cb_ref[...]
        for s in range(1, half + 1):
            back = pltpu.roll(slab, s, axis=0)[A_HALO:A_HALO + pr, :]
            fwd = pltpu.roll(slab, span - s, axis=0)[A_HALO:A_HALO + pr, :]
            acc = acc + back * cw_ref[half - s:half - s + 1, :] + fwd * cw_ref[half + s:half + s + 1, :]
        y = acc * jax.nn.sigmoid(acc)
        k_ref[p0:p0 + pr, :] = y[:, hq:].astype(BF16)
        qT_ref[:, p0:p0 + pr] = (y[:, :hq] * (A_DQK ** -0.5)).T.astype(BF16)

        f0 = c * fr
        is_last = c == pieces - 1
        feat = _dot_nt(wfeat_ref[f0:(2 * A_V + 2 * ng if is_last else f0 + fr), :], xb)
        dst = vT_ref if f0 < A_V else oT_ref
        dst[f0 % A_V:f0 % A_V + fr, :] = feat[:fr, :].astype(BF16)
        if is_last:
            liT_ref[...] = feat[fr:fr + ng, :] + bliT_ref[...]
            lfT_ref[...] = jax.nn.log_sigmoid(feat[fr + ng:, :] + blfT_ref[...])


def _a_proj(x, g, w_tok, w_feat, b_li, b_lf, conv_w, conv_b, bsz, seq):
    m, d = x.shape
    ng = b_li.shape[1]
    hq = A_QK // 2
    n_steps = seq // ROW_TILE
    rh = ROW_TILE // A_HALO
    x3 = x.reshape(bsz, seq, d)
    flat = lambda b_, i: b_ * n_steps + i
    tok = lambda n: pl.BlockSpec((None, ROW_TILE, n), lambda b_, i: (b_, i, 0))
    feat = lambda n: pl.BlockSpec((n, ROW_TILE), lambda b_, i: (0, flat(b_, i)))
    return pl.pallas_call(
        functools.partial(_a_proj_kernel, n_steps=n_steps),
        grid=(bsz, n_steps),
        in_specs=[tok(d),
                  pl.BlockSpec((None, A_HALO, d), lambda b_, i: (b_, jnp.maximum(i * rh - 1, 0), 0)),
                  pl.BlockSpec((None, A_HALO, d),
                               lambda b_, i: (b_, jnp.minimum((i + 1) * rh, seq // A_HALO - 1), 0)),
                  _resident((1, d)), _resident(w_tok.shape), _resident(w_feat.shape),
                  _resident((1, ng)), _resident((1, ng)), _resident((ng, 1)), _resident((ng, 1)),
                  _resident((A_CONV, A_QK)), _resident((1, A_QK))],
        out_specs=[tok(hq), pl.BlockSpec((None, hq, ROW_TILE), lambda b_, i: (b_, 0, i)),
                   feat(A_V), feat(A_V), tok(ng), tok(ng), feat(ng), feat(ng)],
        out_shape=[jax.ShapeDtypeStruct((bsz, seq, hq), BF16),
                   jax.ShapeDtypeStruct((bsz, hq, seq), BF16),
                   jax.ShapeDtypeStruct((A_V, m), BF16),
                   jax.ShapeDtypeStruct((A_V, m), BF16),
                   jax.ShapeDtypeStruct((bsz, seq, ng), F32),
                   jax.ShapeDtypeStruct((bsz, seq, ng), F32),
                   jax.ShapeDtypeStruct((ng, m), F32),
                   jax.ShapeDtypeStruct((ng, m), F32)],
        compiler_params=_cparams("parallel", "parallel"),
        name="mlstm_proj",
    )(x3, x3, x3, g, w_tok, w_feat, b_li, b_lf, b_li.T, b_lf.T, conv_w, conv_b)


def _mlstm_kernel(qT_ref, k_ref, vT_ref, li_ref, lf_ref, liT_ref, lfT_ref, hT_ref,
                  cT_ref, n_ref, m_ref, *, reverse):
    L = A_CHUNK
    J = A_STEP_CHUNKS
    H = A_HEADS
    P = H // 2
    lo = 4 if reverse else 0

    @pl.when(pl.program_id(1) == 0)
    def _():
        cT_ref[...] = jnp.zeros_like(cT_ref)
        n_ref[...] = jnp.zeros_like(n_ref)
        m_ref[...] = jnp.zeros_like(m_ref)

    row_i = lax.broadcasted_iota(jnp.int32, (L, L), 0)
    col_i = lax.broadcasted_iota(jnp.int32, (L, L), 1)
    seen = (col_i >= row_i) if reverse else (col_i <= row_i)
    seen_t = (row_i >= col_i) if reverse else (row_i <= col_i)
    seen_bf = seen.astype(F32).astype(BF16)
    seen3 = jnp.concatenate([seen_bf, seen_bf, seen_bf], axis=1)
    seen_t2 = jnp.concatenate([seen_t, seen_t], axis=1)
    left = lax.broadcasted_iota(jnp.int32, (L, 2 * L), 1) < L
    last = 0 if reverse else L - 1

    pair_rows = lambda rows: jnp.stack(
        [jnp.concatenate([rows[lo + 2 * p:lo + 2 * p + 1], rows[lo + 2 * p + 1:lo + 2 * p + 2]], axis=1)
         for p in range(P)])
    head_rows = lambda rows: rows[lo:lo + H][:, None, :]
    unpair = lambda a: jnp.stack([a[b][:, i * L:(i + 1) * L] for b in range(a.shape[0]) for i in range(2)])

    order = list(range(J - 1, -1, -1) if reverse else range(J))
    m_state = m_ref[...]
    d_l, g_l, w_l, sp_l, sl_l = [], [], [], [], []
    for j in order:
        r0 = j * L
        li = li_ref[r0:r0 + L, :]
        lf = lf_ref[r0:r0 + L, :]
        liT = liT_ref[:, r0:r0 + L]
        lfT = lfT_ref[:, r0:r0 + L]
        b_col = _dot(seen3, jnp.concatenate(_split3(lf), axis=0))
        b_row = _dot_nt(jnp.concatenate(_split3(lfT), axis=1), seen3)
        r_col = li - b_col
        b_tot = b_row[:, last:last + 1]
        a_row = b_tot - b_row + liT
        m_loc = jnp.max(a_row, axis=1, keepdims=True)
        w_l.append(head_rows(jnp.exp(a_row - m_loc)))
        g_l.append(pair_rows(b_row + m_state))
        m_new = jnp.maximum(b_tot + m_state, m_loc)
        sp_l.append(pair_rows(jnp.exp(b_tot + m_state - m_new)))
        sl_l.append(pair_rows(jnp.exp(m_loc - m_new)))
        m_state = m_new
        d_l.append(jnp.stack([jnp.concatenate(
            [r_col[:, lo + 2 * p + i:lo + 2 * p + i + 1] + b_row[lo + 2 * p + i:lo + 2 * p + i + 1, :]
             for i in range(2)], axis=1) for p in range(P)]))
    m_ref[...] = m_state

    cols = lambda ref, j, n: ref[:, j * L:(j + 1) * L].reshape(n, ref.shape[0] // n, L)
    qT = jnp.concatenate([cols(qT_ref, j, H) for j in order], axis=0)
    vT = jnp.concatenate([cols(vT_ref, j, H) for j in order], axis=0)
    k2 = jnp.stack([k_ref[j * L:(j + 1) * L, 2 * p * A_DQK:(2 * p + 2) * A_DQK]
                    for j in order for p in range(P)])

    w = jnp.concatenate(w_l, axis=0)
    vw_aug = jnp.concatenate([(vT.astype(F32) * w).astype(BF16),
                              jnp.broadcast_to(w, (J * H, A_PAD, L)).astype(BF16)], axis=1)
    vw4 = vw_aug.reshape(J * P, 2, A_DV + A_PAD, L)
    vw2 = jnp.concatenate([vw4[:, 0], vw4[:, 1]], axis=2)
    zk = jnp.zeros_like(k2)
    rk = jnp.concatenate([jnp.where(left[None], k2, zk), jnp.where(left[None], zk, k2)], axis=1)
    upd = _bdot(vw2, rk)

    c_state = cT_ref[...]
    n_state = n_ref[...]
    c_l, n_l = [], []
    for i in range(J):
        c_l.append(c_state)
        n_l.append(n_state)
        u = upd[i * P:(i + 1) * P]
        c_state = sp_l[i] * c_state + sl_l[i] * u[:, :A_DV, :]
        n_state = sp_l[i] * n_state + sl_l[i] * u[:, A_DV:A_DV + 1, :]
    cT_ref[...] = c_state
    n_ref[...] = n_state
    c_prev = jnp.concatenate(c_l, axis=0)
    n_prev = jnp.concatenate(n_l, axis=0)

    qT4 = qT.reshape(J * P, 2, A_DQK, L)
    zq = jnp.zeros((J * P, A_DQK, L), BF16)
    rq = jnp.concatenate([jnp.concatenate([qT4[:, 0], zq], axis=2),
                          jnp.concatenate([zq, qT4[:, 1]], axis=2)], axis=1)
    k_aug = jnp.concatenate(
        [k2, jnp.broadcast_to(n_prev, (J * P, A_PAD, 2 * A_DQK)).astype(BF16)], axis=1)
    sq = _bdot(k_aug, rq)
    qn = sq[:, L:L + 1, :]
    d = jnp.where(seen_t2[None], jnp.concatenate(d_l, axis=0), -jnp.inf)
    g = jnp.concatenate(g_l, axis=0)
    m_t = jnp.maximum(g, jnp.max(d, axis=1, keepdims=True))
    p = jnp.exp(d - m_t) * sq[:, :L, :]
    s_inter = jnp.exp(g - m_t)
    den = s_inter * qn + jnp.sum(p, axis=1, keepdims=True)
    inv = 1.0 / jnp.maximum(jnp.abs(den), jnp.exp(-m_t))
    rhs = jnp.concatenate([(qT.astype(F32) * unpair(s_inter * inv)).astype(BF16),
                           unpair((p * inv).astype(BF16))], axis=1)
    lhs = jnp.concatenate([unpair(c_prev).astype(BF16), vT], axis=2)
    out = _bdot(lhs, rhs).astype(BF16)
    for i, j in enumerate(order):
        hT_ref[:, j * L:(j + 1) * L] = out[i * H:(i + 1) * H].reshape(A_V, L)


def _mlstm(k, qT, vT, li, lf, liT, lfT, bsz, seq, reverse):
    tl = A_STEP_CHUNKS * A_CHUNK
    ns = seq // tl
    ng = li.shape[-1]
    half = A_QK // 2
    blk = (lambda i: ns - 1 - i) if reverse else (lambda i: i)
    gate = pl.BlockSpec((None, tl, ng), lambda b_, i: (b_, blk(i), 0))
    feat = lambda n: pl.BlockSpec((n, tl), lambda b_, i: (0, b_ * ns + blk(i)))
    return pl.pallas_call(
        functools.partial(_mlstm_kernel, reverse=reverse),
        grid=(bsz, ns),
        in_specs=[pl.BlockSpec((None, half, tl), lambda b_, i: (b_, 0, blk(i))),
                  pl.BlockSpec((None, tl, half), lambda b_, i: (b_, blk(i), 0)),
                  feat(A_V), gate, gate, feat(ng), feat(ng)],
        out_specs=feat(A_V),
        out_shape=jax.ShapeDtypeStruct((A_V, bsz * seq), BF16),
        scratch_shapes=[pltpu.VMEM((A_HEADS // 2, A_DV, 2 * A_DQK), F32),
                        pltpu.VMEM((A_HEADS // 2, 1, 2 * A_DQK), F32),
                        pltpu.VMEM((ng, A_CHUNK), F32)],
        compiler_params=_cparams("parallel", "arbitrary"),
        name="mlstm_bwd" if reverse else "mlstm_fwd",
    )(qT, k, vT, li, lf, liT, lfT)


def _a_out_kernel(x_ref, hfT_ref, hbT_ref, oT_ref, hg_ref, w_ref, y_ref):
    yT = hfT_ref[...].astype(F32) + hbT_ref[...].astype(F32)
    parts = []
    for h in range(A_HEADS):
        yh = yT[h * A_DV:(h + 1) * A_DV, :]
        ms = jnp.mean(yh * yh, axis=0, keepdims=True)
        parts.append(yh * lax.rsqrt(ms + RMS_EPS))
    ynT = jnp.concatenate(parts, axis=0)
    zT = (ynT * hg_ref[...] * jax.nn.sigmoid(oT_ref[...].astype(F32))).astype(BF16)
    y_ref[...] = x_ref[...] + _dot_tn(zT, w_ref[...])


def _a_out(x, hfT, hbT, oT, hg, w):
    m, d = x.shape
    row = pl.BlockSpec((ROW_TILE, d), lambda i: (i, 0))
    col = pl.BlockSpec((A_V, ROW_TILE), lambda i: (0, i))
    return pl.pallas_call(
        _a_out_kernel,
        grid=(m // ROW_TILE,),
        in_specs=[row, col, col, col, _resident((A_V, ROW_TILE)), _resident((A_V, d))],
        out_specs=row,
        out_shape=jax.ShapeDtypeStruct((m, d), F32),
        compiler_params=_cparams("parallel"),
        name="mlstm_out",
    )(x, hfT, hbT, oT, hg, w)


def _b_proj_kernel(x_ref, g_ref, w_ref, *rest):
    o_refs, xs_ref = rest[:-1], rest[-1]
    xn = _rms(x_ref[...], g_ref[...])
    nblk = D_MODEL // 128
    lhs = {1: xn.astype(BF16)}
    for c in range(nblk):
        xs_ref[c] = xn[:, c * 128:(c + 1) * 128]
    for dil in sorted({d for _, d in B_GROUPS if d > 1}):
        n = ROW_TILE // dil
        pieces = [jnp.concatenate([xs_ref[c, pl.ds(r, n, stride=dil), :] for c in range(nblk)],
                                  axis=-1).astype(BF16) for r in range(dil)]
        lhs[dil] = jnp.concatenate(pieces, axis=0)
    for j, o_ref in enumerate(o_refs):
        dil = B_GROUPS[j // 3][1]
        y = _dot(lhs[dil], w_ref[:, j * B_WIDTH:(j + 1) * B_WIDTH])
        if j % 3 == 0:
            y = y * (B_DH ** -0.5 * LOG2E)
        y = y.astype(BF16)
        n = ROW_TILE // dil
        for r in range(dil):
            o_ref[:, r * B_WIDTH:(r + 1) * B_WIDTH] = y[r * n:(r + 1) * n, :]


def _b_proj(x, g, w):
    m, d = x.shape
    dils = [dil for _, dil in B_GROUPS for _ in range(3)]
    return pl.pallas_call(
        _b_proj_kernel,
        grid=(m // ROW_TILE,),
        in_specs=[pl.BlockSpec((ROW_TILE, d), lambda i: (i, 0)), _resident((1, d)),
                  _resident(w.shape)],
        out_specs=[pl.BlockSpec((ROW_TILE // dil, dil * B_WIDTH), lambda i: (i, 0)) for dil in dils],
        out_shape=[jax.ShapeDtypeStruct((m // dil, dil * B_WIDTH), BF16) for dil in dils],
        scratch_shapes=[pltpu.VMEM((B_WIDTH // 128, ROW_TILE, 128), F32)],
        compiler_params=_cparams("parallel"),
        name="dilated_proj",
    )(x, g, w)


def _b_attn_kernel(q_ref, kp_ref, kc_ref, kn_ref, vp_ref, vc_ref, vn_ref,
                   bias_first_ref, bias_mid_ref, bias_last_ref, o_ref, stat_ref):
    k_all = jnp.concatenate([kp_ref[...], kc_ref[...], kn_ref[...]], axis=0)
    v_all = jnp.concatenate([vp_ref[...], vc_ref[...], vn_ref[...]], axis=0)
    lane = lax.broadcasted_iota(jnp.int32, (B_TQ, 128), 1)
    low = lane < B_DH
    low_k = lax.broadcasted_iota(jnp.int32, (B_TK, 128), 1) < B_DH
    for sb in range(B_SUB):
        bias_ref = bias_first_ref if sb == 0 else (bias_last_ref if sb == B_SUB - 1 else bias_mid_ref)
        q0 = sb * B_TQ
        stat_row = jnp.zeros((B_TQ, 128), F32)
        for hp in range(B_HEADS // 2):
            cols = slice(hp * 128, (hp + 1) * 128)
            q2 = q_ref[q0:q0 + B_TQ, cols]
            k2 = k_all[q0:q0 + B_TK, cols]
            v2 = v_all[q0:q0 + B_TK, cols]
            zero = jnp.zeros_like(q2)
            ones = jnp.ones_like(v2)
            outs = []
            for par, qsel, vsel in ((0, jnp.where(low, q2, zero), jnp.where(low_k, v2, ones)),
                                    (1, jnp.where(low, zero, q2), jnp.where(low_k, ones, v2))):
                hd = 2 * hp + par
                s = _dot_nt(qsel, k2) + bias_ref[hd]
                mx = jnp.max(s, axis=-1, keepdims=True)
                e = jnp.exp2(s - mx)
                res = _dot(e.astype(BF16), vsel)
                outs.append(res)
                den_lane = (DEN_LANE_ODD if par else DEN_LANE_EVEN) + hd
                stat_row = jnp.where(lane == hd, mx, jnp.where(lane == den_lane, res, stat_row))
            o_ref[q0:q0 + B_TQ, cols] = jnp.where(low, outs[0], outs[1]).astype(BF16)
        stat_ref[q0:q0 + B_TQ, :] = stat_row


def _b_attn(q, k, v, bias, bsz, seq, dil):
    n = seq // dil
    rows = B_SUB * B_TQ
    nq = n // rows
    w = B_WIDTH
    q = q.reshape(bsz, n, dil * w)
    k = k.reshape(bsz, n, dil * w)
    v = v.reshape(bsz, n, dil * w)
    hb = rows // B_HALF
    last_half = n // B_HALF - 1
    cur = pl.BlockSpec((None, rows, w), lambda b_, r, i: (b_, i, r))
    prev = pl.BlockSpec((None, B_HALF, w), lambda b_, r, i: (b_, jnp.maximum(i * hb - 1, 0), r))
    nxt = pl.BlockSpec((None, B_HALF, w), lambda b_, r, i: (b_, jnp.minimum((i + 1) * hb, last_half), r))
    at_start = lambda i: jnp.where(i == 0, 1, 0)
    at_end = lambda i: jnp.where(i == nq - 1, 2, 0)
    first_variant = (lambda i: at_start(i) + at_end(i)) if B_SUB == 1 else at_start
    bias_blk = (None, B_HEADS, B_TQ, B_TK)
    bias_first = pl.BlockSpec(bias_blk, lambda b_, r, i: (first_variant(i), 0, 0, 0))
    bias_mid = pl.BlockSpec(bias_blk, lambda b_, r, i: (0, 0, 0, 0), pipeline_mode=pl.Buffered(1))
    bias_last = pl.BlockSpec(bias_blk, lambda b_, r, i: (at_end(i), 0, 0, 0))
    out, stats = pl.pallas_call(
        _b_attn_kernel,
        grid=(bsz, dil, nq),
        in_specs=[cur, prev, cur, nxt, prev, cur, nxt, bias_first, bias_mid, bias_last],
        out_specs=[cur, pl.BlockSpec((None, rows, 128), lambda b_, r, i: (b_, i, r))],
        out_shape=[jax.ShapeDtypeStruct((bsz, n, dil * w), BF16),
                   jax.ShapeDtypeStruct((bsz, n, dil * 128), F32)],
        compiler_params=_cparams("parallel", "parallel", "arbitrary"),
        name=f"dilated_attn_d{dil}",
    )(q, k, k, k, v, v, v, bias, bias, bias)
    return out.reshape(bsz * n, dil * w), stats.reshape(bsz * n, dil * 128)


def _attn_bias(dil):
    slopes = jnp.exp2(-8.0 * (jnp.arange(B_HEADS, dtype=F32) + 1.0) / B_HEADS)
    tq = jnp.arange(B_TQ)[:, None]
    col = jnp.arange(B_TK)[None, :]
    rel = jnp.abs(col - B_HALF - tq)
    base = -slopes[:, None, None] * (rel * dil).astype(F32)[None] * LOG2E
    band = (rel <= B_HALF)[None]
    first = (col >= B_HALF)[None]
    lastv = (col < B_TQ + B_HALF)[None]
    variants = []
    for need_first, need_last in ((False, False), (True, False), (False, True), (True, True)):
        ok = band
        if need_first:
            ok = ok & first
        if need_last:
            ok = ok & lastv
        variants.append(jnp.where(ok, base, NEG_BIG))
    return jnp.stack(variants, axis=0)


def _b_out_kernel(x_ref, o0_ref, o1_ref, o2_ref, s0_ref, s1_ref, s2_ref, e_ref, w_ref, y_ref,
                  os_ref, ss_ref):
    for gi, (s_ref, (_, dil)) in enumerate(zip((s0_ref, s1_ref, s2_ref), B_GROUPS)):
        for r in range(dil if dil > 1 else 0):
            ss_ref[gi, pl.ds(r, ROW_TILE // dil, stride=dil), :] = s_ref[:, r * 128:(r + 1) * 128]
    mxs = [s_ref[...] if dil == 1 else ss_ref[gi]
           for gi, (s_ref, (_, dil)) in enumerate(zip((s0_ref, s1_ref, s2_ref), B_GROUPS))]
    lane = lax.broadcasted_iota(jnp.int32, mxs[0].shape, 1)
    dens = [jnp.where(lane % 2 == 0, pltpu.roll(m, 128 - DEN_LANE_EVEN, axis=1),
                      pltpu.roll(m, 128 - DEN_LANE_ODD, axis=1)) for m in mxs]
    top = jnp.maximum(jnp.maximum(mxs[0], mxs[1]), mxs[2])
    ts = [jnp.exp2(m - top) for m in mxs]
    total = dens[0] * ts[0] + dens[1] * ts[1] + dens[2] * ts[2]
    picked = lane < B_HEADS
    acc = None
    for gi, (t, o_ref, (_, dil)) in enumerate(zip(ts, (o0_ref, o1_ref, o2_ref), B_GROUPS)):
        weight = jnp.where(picked, t / total, 0.0)
        hi = weight.astype(BF16)
        lo = (weight - hi.astype(F32)).astype(BF16)
        wide = _dot(jnp.concatenate([hi, lo], axis=1), e_ref[...])
        nblk = B_WIDTH // 128
        if dil == 1:
            vals = o_ref[...].astype(F32)
        else:
            for r in range(dil):
                for c in range(nblk):
                    lo = r * B_WIDTH + c * 128
                    os_ref[gi, c, pl.ds(r, ROW_TILE // dil, stride=dil), :] = (
                        o_ref[:, lo:lo + 128].astype(F32))
            vals = jnp.concatenate([os_ref[gi, c] for c in range(nblk)], axis=-1)
        term = wide * vals
        acc = term if acc is None else acc + term
    y_ref[...] = x_ref[...] + _dot(acc.astype(BF16), w_ref[...])


def _b_out(x, outs, stats, w):
    m, d = x.shape
    head_of_col = jnp.arange(B_WIDTH) // B_DH
    expand = (jnp.arange(128)[:, None] == head_of_col[None, :]).astype(BF16)
    expand = jnp.concatenate([expand, expand], axis=0)
    row = pl.BlockSpec((ROW_TILE, d), lambda i: (i, 0))
    grp = lambda lanes: [pl.BlockSpec((ROW_TILE // dil, dil * lanes), lambda i: (i, 0))
                         for _, dil in B_GROUPS]
    ng = len(B_GROUPS)
    return pl.pallas_call(
        _b_out_kernel,
        grid=(m // ROW_TILE,),
        in_specs=[row, *grp(B_WIDTH), *grp(128), _resident((256, B_WIDTH)),
                  _resident((B_WIDTH, d))],
        out_specs=row,
        out_shape=jax.ShapeDtypeStruct((m, d), F32),
        scratch_shapes=[pltpu.VMEM((ng, B_WIDTH // 128, ROW_TILE, 128), F32),
                        pltpu.VMEM((ng, ROW_TILE, 128), F32)],
        compiler_params=_cparams("parallel"),
        name="dilated_out",
    )(x, *outs, *stats, expand, w)


def _mlstm_mixer(x, g, w_in, conv_w, conv_b, gate_b, head_g, w_out, bsz, seq):
    wg = w_in[:, A_QK + 2 * A_V:]
    H = A_HEADS
    pick = lambda a, s: jnp.concatenate([a[..., s * H:(s + 1) * H], a[..., (s + 2) * H:(s + 3) * H]], axis=-1)
    w_li, w_lf = pick(wg, 0), pick(wg, 1)
    gate_pad = jnp.zeros((w_in.shape[0], 128 - 4 * H), w_in.dtype)
    w_tok = jnp.concatenate([w_in[:, :A_QK], w_li, w_lf, gate_pad], axis=1).astype(BF16)
    w_feat = jnp.concatenate([w_in[:, A_QK:A_QK + 2 * A_V], w_li, w_lf], axis=1).T.astype(BF16)
    gb = gate_b.astype(F32)[None, :]
    b_li, b_lf = pick(gb, 0), pick(gb, 1)
    k, qT, vT, oT, li, lf, liT, lfT = _a_proj(x, g, w_tok, w_feat, b_li, b_lf,
                                              conv_w.astype(F32), conv_b.astype(F32)[None, :], bsz, seq)
    hfT = _mlstm(k, qT, vT, li, lf, liT, lfT, bsz, seq, reverse=False)
    hbT = _mlstm(k, qT, vT, li, lf, liT, lfT, bsz, seq, reverse=True)
    hg = jnp.broadcast_to(head_g.astype(F32)[:, None], (A_V, ROW_TILE))
    return _a_out(x, hfT, hbT, oT, hg, w_out.astype(BF16))


def _dilated_mixer(x, g, w_in, w_out, bsz, seq):
    slabs = _b_proj(x, g, w_in.astype(BF16))
    outs, stats = [], []
    for gi, (_, dil) in enumerate(B_GROUPS):
        o, l = _b_attn(slabs[3 * gi], slabs[3 * gi + 1], slabs[3 * gi + 2], _attn_bias(dil),
                       bsz, seq, dil)
        outs.append(o)
        stats.append(l)
    return _b_out(x, outs, stats, w_out.astype(BF16))


def kernel(x, norm_ffn1, ffn1_gate, ffn1_up, ffn1_down, norm_mix, a_w_in, a_conv_w, a_conv_b,
           a_gate_b, a_head_g, a_w_out, b_w_in, b_w_out, norm_ffn2, ffn2_gate, ffn2_up,
           ffn2_down, norm_final):
    bsz, seq, d = x.shape
    h = x.reshape(bsz * seq, d)
    row = lambda a: a.astype(F32)[None, :]
    g_final = row(norm_final)
    for i in range(DEPTH):
        h = _ffn(h, row(norm_ffn1[i]), ffn1_gate, ffn1_up, ffn1_down, i, g_final, False)
        j = i // 2
        if i % 2 == 0:
            h = _mlstm_mixer(h, row(norm_mix[i]), a_w_in[j], a_conv_w[j], a_conv_b[j],
                             a_gate_b[j], a_head_g[j], a_w_out[j], bsz, seq)
        else:
            h = _dilated_mixer(h, row(norm_mix[i]), b_w_in[j], b_w_out[j], bsz, seq)
        h = _ffn(h, row(norm_ffn2[i]), ffn2_gate, ffn2_up, ffn2_down, i, g_final, i == DEPTH - 1)
    return h.reshape(bsz, seq, d)
```

```python
import functools

import jax
import jax.numpy as jnp
from jax import lax
from jax.experimental import pallas as pl
from jax.experimental.pallas import tpu as pltpu

F32 = jnp.float32
BF16 = jnp.bfloat16

D_MODEL = 1024
DEPTH = 4
D_FF = 2816
RMS_EPS = 1e-6

A_HEADS = 4
A_DQK = 128
A_DV = 256
A_CHUNK = 128
A_CONV = 5
A_QK = 2 * A_HEADS * A_DQK
A_V = A_HEADS * A_DV
A_PAD = 16
assert A_DQK == A_CHUNK

B_GROUPS = ((128, 1), (512, 4), (2048, 16))
B_HEADS = 16
B_DH = 64
B_WIDTH = B_HEADS * B_DH
B_HALF = 64
B_TQ = 128
B_TK = B_TQ + 2 * B_HALF
B_SUB = 8
LOG2E = 1.4426950408889634
DEN_LANE_EVEN = 64
DEN_LANE_ODD = 16
NEG_BIG = -1e30

ROW_TILE = 512
W_STAGE_ROWS = 256
FFN_TILE = 1024
W_STAGE_SLOTS = 3
A_STEP_CHUNKS = 8
VMEM_LIMIT = 56 * 1024 * 1024


def _cparams(*sem):
    return pltpu.CompilerParams(dimension_semantics=sem, vmem_limit_bytes=VMEM_LIMIT)


def _resident(shape):
    nd = len(shape)
    return pl.BlockSpec(shape, lambda *_: (0,) * nd, pipeline_mode=pl.Buffered(1))


def _rms(x, g):
    ms = jnp.mean(x * x, axis=-1, keepdims=True)
    return x * lax.rsqrt(ms + RMS_EPS) * g


def _dot(a, b):
    return jnp.dot(a, b, preferred_element_type=F32)


def _dot_nt(a, b):
    return lax.dot_general(a, b, (((1,), (1,)), ((), ())), preferred_element_type=F32)


def _dot_tn(a, b):
    return lax.dot_general(a, b, (((0,), (0,)), ((), ())), preferred_element_type=F32)


def _bdot(a, b):
    return lax.dot_general(a, b, (((2,), (1,)), ((0,), (0,))), preferred_element_type=F32)


def _split3(x):
    hi = x.astype(BF16)
    r1 = x - hi.astype(F32)
    mid = r1.astype(BF16)
    lo = (r1 - mid.astype(F32)).astype(BF16)
    return hi, mid, lo


def _stage_copy(src_hbm, layer, stage_ref, sem, c):
    slots, chunk = stage_ref.shape[0], stage_ref.shape[1]
    slot = c % slots
    return pltpu.make_async_copy(src_hbm.at[layer, pl.ds(c * chunk, chunk), :],
                                 stage_ref.at[slot], sem.at[slot])


def _load_as_bf16(src_hbm, layer, dst_ref, stage_ref, sem):
    slots, chunk = stage_ref.shape[0], stage_ref.shape[1]
    n = src_hbm.shape[1] // chunk
    for c in range(min(slots - 1, n)):
        _stage_copy(src_hbm, layer, stage_ref, sem, c).start()
    for c in range(n):
        if c + slots - 1 < n:
            _stage_copy(src_hbm, layer, stage_ref, sem, c + slots - 1).start()
        _stage_copy(src_hbm, layer, stage_ref, sem, c).wait()
        dst_ref[c * chunk:(c + 1) * chunk, :] = stage_ref[c % slots].astype(BF16)


def _ffn_kernel(x_ref, g_ref, wg_hbm, wu_hbm, wd_hbm, gf_ref, o_ref,
                wg_ref, wu_ref, wd_ref, stage_in_ref, stage_out_ref, sem, *, layer, final_norm):
    @pl.when(pl.program_id(0) == 0)
    def _():
        _load_as_bf16(wg_hbm, layer, wg_ref, stage_in_ref, sem)
        _load_as_bf16(wu_hbm, layer, wu_ref, stage_in_ref, sem)
        _load_as_bf16(wd_hbm, layer, wd_ref, stage_out_ref, sem)

    halves = FFN_TILE // 256
    hr = x_ref.shape[0] // halves
    g = g_ref[...]
    xs = [x_ref[h * hr:(h + 1) * hr, :] for h in range(halves)]
    acts = []
    for h in range(halves):
        xn = _rms(xs[h], g).astype(BF16)
        gate = _dot(xn, wg_ref[...])
        up = _dot(xn, wu_ref[...])
        acts.append((gate * jax.nn.sigmoid(gate) * up).astype(BF16))
    for h in range(halves):
        y = xs[h] + 0.5 * _dot(acts[h], wd_ref[...])
        if final_norm:
            y = _rms(y, gf_ref[...])
        o_ref[h * hr:(h + 1) * hr, :] = y


def _ffn(x, g, wg, wu, wd, layer, g_final, final_norm):
    m, d = x.shape
    f = wg.shape[2]
    row = pl.BlockSpec((FFN_TILE, d), lambda i: (i, 0))
    hbm = pl.BlockSpec(memory_space=pltpu.MemorySpace.HBM)
    return pl.pallas_call(
        functools.partial(_ffn_kernel, layer=layer, final_norm=final_norm),
        grid=(m // FFN_TILE,),
        in_specs=[row, _resident((1, d)), hbm, hbm, hbm, _resident((1, d))],
        out_specs=row,
        out_shape=jax.ShapeDtypeStruct((m, d), F32),
        scratch_shapes=[pltpu.VMEM((d, f), BF16), pltpu.VMEM((d, f), BF16), pltpu.VMEM((f, d), BF16),
                        pltpu.VMEM((W_STAGE_SLOTS, W_STAGE_ROWS, f), F32),
                        pltpu.VMEM((W_STAGE_SLOTS, W_STAGE_ROWS, d), F32),
                        pltpu.SemaphoreType.DMA((W_STAGE_SLOTS,))],
        compiler_params=_cparams("arbitrary"),
        name="ffn",
    )(x, g, wg, wu, wd, g_final)


A_HALO = 8


def _a_proj_kernel(x_ref, xp_ref, xn_ref, g_ref, wtok_ref, wfeat_ref,
                   bli_ref, blf_ref, bliT_ref, blfT_ref, cw_ref, cb_ref,
                   k_ref, qT_ref, vT_ref, oT_ref, li_ref, lf_ref, liT_ref, lfT_ref, *, n_steps):
    i = pl.program_id(1)
    g = g_ref[...]
    xn = _rms(x_ref[...], g)
    before = _rms(xp_ref[...], g) * (i > 0).astype(F32)
    after = _rms(xn_ref[...], g) * (i < n_steps - 1).astype(F32)
    ext = jnp.concatenate([before, xn, after], axis=0).astype(BF16)
    xb = xn.astype(BF16)
    rows = xn.shape[0]
    ng = li_ref.shape[-1]

    tok = _dot(ext, wtok_ref[...])
    qk = tok[:, :A_QK]
    gates = tok[A_HALO:A_HALO + rows, A_QK:A_QK + 2 * ng]
    li_ref[...] = gates[:, :ng] + bli_ref[...]
    lf_ref[...] = jax.nn.log_sigmoid(gates[:, ng:] + blf_ref[...])
    half = A_CONV // 2
    hq = A_QK // 2
    pieces = 4
    pr = rows // pieces
    fr = A_V // (pieces // 2)
    span = pr + 2 * A_HALO
    for c in range(pieces):
        p0 = c * pr
        slab = qk[p0:p0 + span, :]
        acc = slab[A_HALO:A_HALO + pr, :] * cw_ref[half:half + 1, :] + cb_ref[...]
        for s in range(1, half + 1):
            back = pltpu.roll(slab, s, axis=0)[A_HALO:A_HALO + pr, :]
            fwd = pltpu.roll(slab, span - s, axis=0)[A_HALO:A_HALO + pr, :]
            acc = acc + back * cw_ref[half - s:half - s + 1, :] + fwd * cw_ref[half + s:half + s + 1, :]
        y = acc * jax.nn.sigmoid(acc)
        k_ref[p0:p0 + pr, :] = y[:, hq:].astype(BF16)
        qT_ref[:, p0:p0 + pr] = (y[:, :hq] * (A_DQK ** -0.5)).T.astype(BF16)

        f0 = c * fr
        is_last = c == pieces - 1
        feat = _dot_nt(wfeat_ref[f0:(2 * A_V + 2 * ng if is_last else f0 + fr), :], xb)
        dst = vT_ref if f0 < A_V else oT_ref
        dst[f0 % A_V:f0 % A_V + fr, :] = feat[:fr, :].astype(BF16)
        if is_last:
            liT_ref[...] = feat[fr:fr + ng, :] + bliT_ref[...]
            lfT_ref[...] = jax.nn.log_sigmoid(feat[fr + ng:, :] + blfT_ref[...])


def _a_proj(x, g, w_tok, w_feat, b_li, b_lf, conv_w, conv_b, bsz, seq):
    m, d = x.shape
    ng = b_li.shape[1]
    hq = A_QK // 2
    n_steps = seq // ROW_TILE
    rh = ROW_TILE // A_HALO
    x3 = x.reshape(bsz, seq, d)
    flat = lambda b_, i: b_ * n_steps + i
    tok = lambda n: pl.BlockSpec((None, ROW_TILE, n), lambda b_, i: (b_, i, 0))
    feat = lambda n: pl.BlockSpec((n, ROW_TILE), lambda b_, i: (0, flat(b_, i)))
    return pl.pallas_call(
        functools.partial(_a_proj_kernel, n_steps=n_steps),
        grid=(bsz, n_steps),
        in_specs=[tok(d),
                  pl.BlockSpec((None, A_HALO, d), lambda b_, i: (b_, jnp.maximum(i * rh - 1, 0), 0)),
                  pl.BlockSpec((None, A_HALO, d),
                               lambda b_, i: (b_, jnp.minimum((i + 1) * rh, seq // A_HALO - 1), 0)),
                  _resident((1, d)), _resident(w_tok.shape), _resident(w_feat.shape),
                  _resident((1, ng)), _resident((1, ng)), _resident((ng, 1)), _resident((ng, 1)),
                  _resident((A_CONV, A_QK)), _resident((1, A_QK))],
        out_specs=[tok(hq), pl.BlockSpec((None, hq, ROW_TILE), lambda b_, i: (b_, 0, i)),
                   feat(A_V), feat(A_V), tok(ng), tok(ng), feat(ng), feat(ng)],
        out_shape=[jax.ShapeDtypeStruct((bsz, seq, hq), BF16),
                   jax.ShapeDtypeStruct((bsz, hq, seq), BF16),
                   jax.ShapeDtypeStruct((A_V, m), BF16),
                   jax.ShapeDtypeStruct((A_V, m), BF16),
                   jax.ShapeDtypeStruct((bsz, seq, ng), F32),
                   jax.ShapeDtypeStruct((bsz, seq, ng), F32),
                   jax.ShapeDtypeStruct((ng, m), F32),
                   jax.ShapeDtypeStruct((ng, m), F32)],
        compiler_params=_cparams("parallel", "parallel"),
        name="mlstm_proj",
    )(x3, x3, x3, g, w_tok, w_feat, b_li, b_lf, b_li.T, b_lf.T, conv_w, conv_b)


def _mlstm_kernel(qT_ref, k_ref, vT_ref, li_ref, lf_ref, liT_ref, lfT_ref, *rest, reverse):
    add_ref = rest[0] if len(rest) == 5 else None
    hT_ref, cT_ref, n_ref, m_ref = rest[-4:]
    L = A_CHUNK
    J = A_STEP_CHUNKS
    H = A_HEADS
    P = H // 2
    lo = 4 if reverse else 0

    @pl.when(pl.program_id(1) == 0)
    def _():
        cT_ref[...] = jnp.zeros_like(cT_ref)
        n_ref[...] = jnp.zeros_like(n_ref)
        m_ref[...] = jnp.zeros_like(m_ref)

    row_i = lax.broadcasted_iota(jnp.int32, (L, L), 0)
    col_i = lax.broadcasted_iota(jnp.int32, (L, L), 1)
    seen = (col_i >= row_i) if reverse else (col_i <= row_i)
    seen_t = (row_i >= col_i) if reverse else (row_i <= col_i)
    seen_bf = seen.astype(F32).astype(BF16)
    seen3 = jnp.concatenate([seen_bf, seen_bf, seen_bf], axis=1)
    seen_t2 = jnp.concatenate([seen_t, seen_t], axis=1)
    left = lax.broadcasted_iota(jnp.int32, (L, 2 * L), 1) < L
    last = 0 if reverse else L - 1

    pair_rows = lambda rows: jnp.stack(
        [jnp.concatenate([rows[lo + 2 * p:lo + 2 * p + 1], rows[lo + 2 * p + 1:lo + 2 * p + 2]], axis=1)
         for p in range(P)])
    head_rows = lambda rows: rows[lo:lo + H][:, None, :]
    unpair = lambda a: jnp.stack([a[b][:, i * L:(i + 1) * L] for b in range(a.shape[0]) for i in range(2)])

    order = list(range(J - 1, -1, -1) if reverse else range(J))
    m_state = m_ref[...]
    d_l, g_l, w_l, sp_l, sl_l = [], [], [], [], []
    for j in order:
        r0 = j * L
        li = li_ref[r0:r0 + L, :]
        lf = lf_ref[r0:r0 + L, :]
        liT = liT_ref[:, r0:r0 + L]
        lfT = lfT_ref[:, r0:r0 + L]
        b_col = _dot(seen3, jnp.concatenate(_split3(lf), axis=0))
        b_row = _dot_nt(jnp.concatenate(_split3(lfT), axis=1), seen3)
        r_col = li - b_col
        b_tot = b_row[:, last:last + 1]
        a_row = b_tot - b_row + liT
        m_loc = jnp.max(a_row, axis=1, keepdims=True)
        w_l.append(head_rows(jnp.exp(a_row - m_loc)))
        g_l.append(pair_rows(b_row + m_state))
        m_new = jnp.maximum(b_tot + m_state, m_loc)
        sp_l.append(pair_rows(jnp.exp(b_tot + m_state - m_new)))
        sl_l.append(pair_rows(jnp.exp(m_loc - m_new)))
        m_state = m_new
        d_l.append(jnp.stack([jnp.concatenate(
            [r_col[:, lo + 2 * p + i:lo + 2 * p + i + 1] + b_row[lo + 2 * p + i:lo + 2 * p + i + 1, :]
             for i in range(2)], axis=1) for p in range(P)]))
    m_ref[...] = m_state

    cols = lambda ref, j, n: ref[:, j * L:(j + 1) * L].reshape(n, ref.shape[0] // n, L)
    qT = jnp.concatenate([cols(qT_ref, j, H) for j in order], axis=0)
    vT = jnp.concatenate([cols(vT_ref, j, H) for j in order], axis=0)
    k2 = jnp.stack([k_ref[j * L:(j + 1) * L, 2 * p * A_DQK:(2 * p + 2) * A_DQK]
                    for j in order for p in range(P)])

    w = jnp.concatenate(w_l, axis=0)
    vw_aug = jnp.concatenate([(vT.astype(F32) * w).astype(BF16),
                              jnp.broadcast_to(w, (J * H, A_PAD, L)).astype(BF16)], axis=1)
    vw4 = vw_aug.reshape(J * P, 2, A_DV + A_PAD, L)
    vw2 = jnp.concatenate([vw4[:, 0], vw4[:, 1]], axis=2)
    zk = jnp.zeros_like(k2)
    rk = jnp.concatenate([jnp.where(left[None], k2, zk), jnp.where(left[None], zk, k2)], axis=1)
    upd = _bdot(vw2, rk)

    c_state = cT_ref[...]
    n_state = n_ref[...]
    c_l, n_l = [], []
    for i in range(J):
        c_l.append(c_state)
        n_l.append(n_state)
        u = upd[i * P:(i + 1) * P]
        c_state = sp_l[i] * c_state + sl_l[i] * u[:, :A_DV, :]
        n_state = sp_l[i] * n_state + sl_l[i] * u[:, A_DV:A_DV + 1, :]
    cT_ref[...] = c_state
    n_ref[...] = n_state
    c_prev = jnp.concatenate(c_l, axis=0)
    n_prev = jnp.concatenate(n_l, axis=0)

    qT4 = qT.reshape(J * P, 2, A_DQK, L)
    zq = jnp.zeros((J * P, A_DQK, L), BF16)
    rq = jnp.concatenate([jnp.concatenate([qT4[:, 0], zq], axis=2),
                          jnp.concatenate([zq, qT4[:, 1]], axis=2)], axis=1)
    k_aug = jnp.concatenate(
        [k2, jnp.broadcast_to(n_prev, (J * P, A_PAD, 2 * A_DQK)).astype(BF16)], axis=1)
    sq = _bdot(k_aug, rq)
    qn = sq[:, L:L + 1, :]
    d = jnp.where(seen_t2[None], jnp.concatenate(d_l, axis=0), -jnp.inf)
    g = jnp.concatenate(g_l, axis=0)
    m_t = jnp.maximum(g, jnp.max(d, axis=1, keepdims=True))
    p = jnp.exp(d - m_t) * sq[:, :L, :]
    s_inter = jnp.exp(g - m_t)
    den = s_inter * qn + jnp.sum(p, axis=1, keepdims=True)
    inv = 1.0 / jnp.maximum(jnp.abs(den), jnp.exp(-m_t))
    rhs = jnp.concatenate([(qT.astype(F32) * unpair(s_inter * inv)).astype(BF16),
                           unpair((p * inv).astype(BF16))], axis=1)
    lhs = jnp.concatenate([unpair(c_prev).astype(BF16), vT], axis=2)
    out = _bdot(lhs, rhs)
    for i, j in enumerate(order):
        blk = out[i * H:(i + 1) * H].reshape(A_V, L)
        if add_ref is not None:
            blk = blk + add_ref[:, j * L:(j + 1) * L].astype(F32)
        hT_ref[:, j * L:(j + 1) * L] = blk.astype(BF16)


def _mlstm(k, qT, vT, li, lf, liT, lfT, bsz, seq, reverse, add=None):
    tl = A_STEP_CHUNKS * A_CHUNK
    ns = seq // tl
    ng = li.shape[-1]
    half = A_QK // 2
    blk = (lambda i: ns - 1 - i) if reverse else (lambda i: i)
    gate = pl.BlockSpec((None, tl, ng), lambda b_, i: (b_, blk(i), 0))
    feat = lambda n: pl.BlockSpec((n, tl), lambda b_, i: (0, b_ * ns + blk(i)))
    return pl.pallas_call(
        functools.partial(_mlstm_kernel, reverse=reverse),
        grid=(bsz, ns),
        in_specs=[pl.BlockSpec((None, half, tl), lambda b_, i: (b_, 0, blk(i))),
                  pl.BlockSpec((None, tl, half), lambda b_, i: (b_, blk(i), 0)),
                  feat(A_V), gate, gate, feat(ng), feat(ng)] + ([feat(A_V)] if add is not None else []),
        out_specs=feat(A_V),
        out_shape=jax.ShapeDtypeStruct((A_V, bsz * seq), BF16),
        scratch_shapes=[pltpu.VMEM((A_HEADS // 2, A_DV, 2 * A_DQK), F32),
                        pltpu.VMEM((A_HEADS // 2, 1, 2 * A_DQK), F32),
                        pltpu.VMEM((ng, A_CHUNK), F32)],
        compiler_params=_cparams("parallel", "arbitrary"),
        name="mlstm_bwd" if reverse else "mlstm_fwd",
    )(qT, k, vT, li, lf, liT, lfT, *([add] if add is not None else []))


def _a_out_kernel(x_ref, hT_ref, oT_ref, hg_ref, w_ref, y_ref):
    yT = hT_ref[...].astype(F32)
    parts = []
    for h in range(A_HEADS):
        yh = yT[h * A_DV:(h + 1) * A_DV, :]
        ms = jnp.mean(yh * yh, axis=0, keepdims=True)
        parts.append(yh * lax.rsqrt(ms + RMS_EPS))
    ynT = jnp.concatenate(parts, axis=0)
    zT = (ynT * hg_ref[...] * jax.nn.sigmoid(oT_ref[...].astype(F32))).astype(BF16)
    y_ref[...] = x_ref[...] + _dot_tn(zT, w_ref[...])


def _a_out(x, hT, oT, hg, w):
    m, d = x.shape
    row = pl.BlockSpec((ROW_TILE, d), lambda i: (i, 0))
    col = pl.BlockSpec((A_V, ROW_TILE), lambda i: (0, i))
    return pl.pallas_call(
        _a_out_kernel,
        grid=(m // ROW_TILE,),
        in_specs=[row, col, col, _resident((A_V, ROW_TILE)), _resident((A_V, d))],
        out_specs=row,
        out_shape=jax.ShapeDtypeStruct((m, d), F32),
        compiler_params=_cparams("parallel"),
        name="mlstm_out",
    )(x, hT, oT, hg, w)


def _b_proj_kernel(x_ref, g_ref, w_ref, *rest):
    o_refs, xs_ref = rest[:-1], rest[-1]
    xn = _rms(x_ref[...], g_ref[...])
    nblk = D_MODEL // 128
    lhs = {1: xn.astype(BF16)}
    for c in range(nblk):
        xs_ref[c] = xn[:, c * 128:(c + 1) * 128]
    for dil in sorted({d for _, d in B_GROUPS if d > 1}):
        n = ROW_TILE // dil
        pieces = [jnp.concatenate([xs_ref[c, pl.ds(r, n, stride=dil), :] for c in range(nblk)],
                                  axis=-1).astype(BF16) for r in range(dil)]
        lhs[dil] = jnp.concatenate(pieces, axis=0)
    for j, o_ref in enumerate(o_refs):
        dil = B_GROUPS[j // 3][1]
        y = _dot(lhs[dil], w_ref[:, j * B_WIDTH:(j + 1) * B_WIDTH])
        if j % 3 == 0:
            y = y * (B_DH ** -0.5 * LOG2E)
        y = y.astype(BF16)
        n = ROW_TILE // dil
        for r in range(dil):
            o_ref[:, r * B_WIDTH:(r + 1) * B_WIDTH] = y[r * n:(r + 1) * n, :]


def _b_proj(x, g, w):
    m, d = x.shape
    dils = [dil for _, dil in B_GROUPS for _ in range(3)]
    return pl.pallas_call(
        _b_proj_kernel,
        grid=(m // ROW_TILE,),
        in_specs=[pl.BlockSpec((ROW_TILE, d), lambda i: (i, 0)), _resident((1, d)),
                  _resident(w.shape)],
        out_specs=[pl.BlockSpec((ROW_TILE // dil, dil * B_WIDTH), lambda i: (i, 0)) for dil in dils],
        out_shape=[jax.ShapeDtypeStruct((m // dil, dil * B_WIDTH), BF16) for dil in dils],
        scratch_shapes=[pltpu.VMEM((B_WIDTH // 128, ROW_TILE, 128), F32)],
        compiler_params=_cparams("parallel"),
        name="dilated_proj",
    )(x, g, w)


def _b_attn_kernel(q_ref, kp_ref, kc_ref, kn_ref, vp_ref, vc_ref, vn_ref,
                   bias_first_ref, bias_mid_ref, bias_last_ref, o_ref, stat_ref):
    k_all = jnp.concatenate([kp_ref[...], kc_ref[...], kn_ref[...]], axis=0)
    v_all = jnp.concatenate([vp_ref[...], vc_ref[...], vn_ref[...]], axis=0)
    lane = lax.broadcasted_iota(jnp.int32, (B_TQ, 128), 1)
    low = lane < B_DH
    low_k = lax.broadcasted_iota(jnp.int32, (B_TK, 128), 1) < B_DH
    for sb in range(B_SUB):
        bias_ref = bias_first_ref if sb == 0 else (bias_last_ref if sb == B_SUB - 1 else bias_mid_ref)
        q0 = sb * B_TQ
        stat_row = jnp.zeros((B_TQ, 128), F32)
        for hp in range(B_HEADS // 2):
            cols = slice(hp * 128, (hp + 1) * 128)
            q2 = q_ref[q0:q0 + B_TQ, cols]
            k2 = k_all[q0:q0 + B_TK, cols]
            v2 = v_all[q0:q0 + B_TK, cols]
            zero = jnp.zeros_like(q2)
            ones = jnp.ones_like(v2)
            outs = []
            for par, qsel, vsel in ((0, jnp.where(low, q2, zero), jnp.where(low_k, v2, ones)),
                                    (1, jnp.where(low, zero, q2), jnp.where(low_k, ones, v2))):
                hd = 2 * hp + par
                s = _dot_nt(qsel, k2) + bias_ref[hd]
                mx = jnp.max(s, axis=-1, keepdims=True)
                e = jnp.exp2(s - mx)
                res = _dot(e.astype(BF16), vsel)
                outs.append(res)
                den_lane = (DEN_LANE_ODD if par else DEN_LANE_EVEN) + hd
                stat_row = jnp.where(lane == hd, mx, jnp.where(lane == den_lane, res, stat_row))
            o_ref[q0:q0 + B_TQ, cols] = jnp.where(low, outs[0], outs[1]).astype(BF16)
        stat_ref[q0:q0 + B_TQ, :] = stat_row


def _b_attn(q, k, v, bias, bsz, seq, dil):
    n = seq // dil
    rows = B_SUB * B_TQ
    nq = n // rows
    w = B_WIDTH
    q = q.reshape(bsz, n, dil * w)
    k = k.reshape(bsz, n, dil * w)
    v = v.reshape(bsz, n, dil * w)
    hb = rows // B_HALF
    last_half = n // B_HALF - 1
    cur = pl.BlockSpec((None, rows, w), lambda b_, r, i: (b_, i, r))
    prev = pl.BlockSpec((None, B_HALF, w), lambda b_, r, i: (b_, jnp.maximum(i * hb - 1, 0), r))
    nxt = pl.BlockSpec((None, B_HALF, w), lambda b_, r, i: (b_, jnp.minimum((i + 1) * hb, last_half), r))
    at_start = lambda i: jnp.where(i == 0, 1, 0)
    at_end = lambda i: jnp.where(i == nq - 1, 2, 0)
    first_variant = (lambda i: at_start(i) + at_end(i)) if B_SUB == 1 else at_start
    bias_blk = (None, B_HEADS, B_TQ, B_TK)
    bias_first = pl.BlockSpec(bias_blk, lambda b_, r, i: (first_variant(i), 0, 0, 0))
    bias_mid = pl.BlockSpec(bias_blk, lambda b_, r, i: (0, 0, 0, 0), pipeline_mode=pl.Buffered(1))
    bias_last = pl.BlockSpec(bias_blk, lambda b_, r, i: (at_end(i), 0, 0, 0))
    out, stats = pl.pallas_call(
        _b_attn_kernel,
        grid=(bsz, dil, nq),
        in_specs=[cur, prev, cur, nxt, prev, cur, nxt, bias_first, bias_mid, bias_last],
        out_specs=[cur, pl.BlockSpec((None, rows, 128), lambda b_, r, i: (b_, i, r))],
        out_shape=[jax.ShapeDtypeStruct((bsz, n, dil * w), BF16),
                   jax.ShapeDtypeStruct((bsz, n, dil * 128), F32)],
        compiler_params=_cparams("parallel", "parallel", "arbitrary"),
        name=f"dilated_attn_d{dil}",
    )(q, k, k, k, v, v, v, bias, bias, bias)
    return out.reshape(bsz * n, dil * w), stats.reshape(bsz * n, dil * 128)


def _attn_bias(dil):
    slopes = jnp.exp2(-8.0 * (jnp.arange(B_HEADS, dtype=F32) + 1.0) / B_HEADS)
    tq = jnp.arange(B_TQ)[:, None]
    col = jnp.arange(B_TK)[None, :]
    rel = jnp.abs(col - B_HALF - tq)
    base = -slopes[:, None, None] * (rel * dil).astype(F32)[None] * LOG2E
    band = (rel <= B_HALF)[None]
    first = (col >= B_HALF)[None]
    lastv = (col < B_TQ + B_HALF)[None]
    variants = []
    for need_first, need_last in ((False, False), (True, False), (False, True), (True, True)):
        ok = band
        if need_first:
            ok = ok & first
        if need_last:
            ok = ok & lastv
        variants.append(jnp.where(ok, base, NEG_BIG))
    return jnp.stack(variants, axis=0)


def _b_out_kernel(x_ref, o0_ref, o1_ref, o2_ref, s0_ref, s1_ref, s2_ref, e_ref, w_ref, y_ref,
                  os_ref, ss_ref):
    for gi, (s_ref, (_, dil)) in enumerate(zip((s0_ref, s1_ref, s2_ref), B_GROUPS)):
        for r in range(dil if dil > 1 else 0):
            ss_ref[gi, pl.ds(r, ROW_TILE // dil, stride=dil), :] = s_ref[:, r * 128:(r + 1) * 128]
    mxs = [s_ref[...] if dil == 1 else ss_ref[gi]
           for gi, (s_ref, (_, dil)) in enumerate(zip((s0_ref, s1_ref, s2_ref), B_GROUPS))]
    lane = lax.broadcasted_iota(jnp.int32, mxs[0].shape, 1)
    dens = [jnp.where(lane % 2 == 0, pltpu.roll(m, 128 - DEN_LANE_EVEN, axis=1),
                      pltpu.roll(m, 128 - DEN_LANE_ODD, axis=1)) for m in mxs]
    top = jnp.maximum(jnp.maximum(mxs[0], mxs[1]), mxs[2])
    ts = [jnp.exp2(m - top) for m in mxs]
    total = dens[0] * ts[0] + dens[1] * ts[1] + dens[2] * ts[2]
    picked = lane < B_HEADS
    acc = None
    for gi, (t, o_ref, (_, dil)) in enumerate(zip(ts, (o0_ref, o1_ref, o2_ref), B_GROUPS)):
        weight = jnp.where(picked, t / total, 0.0)
        hi = weight.astype(BF16)
        lo = (weight - hi.astype(F32)).astype(BF16)
        wide = _dot(jnp.concatenate([hi, lo], axis=1), e_ref[...])
        nblk = B_WIDTH // 128
        if dil == 1:
            vals = o_ref[...].astype(F32)
        else:
            for r in range(dil):
                for c in range(nblk):
                    lo = r * B_WIDTH + c * 128
                    os_ref[gi, c, pl.ds(r, ROW_TILE // dil, stride=dil), :] = (
                        o_ref[:, lo:lo + 128].astype(F32))
            vals = jnp.concatenate([os_ref[gi, c] for c in range(nblk)], axis=-1)
        term = wide * vals
        acc = term if acc is None else acc + term
    y_ref[...] = x_ref[...] + _dot(acc.astype(BF16), w_ref[...])


def _b_out(x, outs, stats, w):
    m, d = x.shape
    head_of_col = jnp.arange(B_WIDTH) // B_DH
    expand = (jnp.arange(128)[:, None] == head_of_col[None, :]).astype(BF16)
    expand = jnp.concatenate([expand, expand], axis=0)
    row = pl.BlockSpec((ROW_TILE, d), lambda i: (i, 0))
    grp = lambda lanes: [pl.BlockSpec((ROW_TILE // dil, dil * lanes), lambda i: (i, 0))
                         for _, dil in B_GROUPS]
    ng = len(B_GROUPS)
    return pl.pallas_call(
        _b_out_kernel,
        grid=(m // ROW_TILE,),
        in_specs=[row, *grp(B_WIDTH), *grp(128), _resident((256, B_WIDTH)),
                  _resident((B_WIDTH, d))],
        out_specs=row,
        out_shape=jax.ShapeDtypeStruct((m, d), F32),
        scratch_shapes=[pltpu.VMEM((ng, B_WIDTH // 128, ROW_TILE, 128), F32),
                        pltpu.VMEM((ng, ROW_TILE, 128), F32)],
        compiler_params=_cparams("parallel"),
        name="dilated_out",
    )(x, *outs, *stats, expand, w)


def _mlstm_mixer(x, g, w_in, conv_w, conv_b, gate_b, head_g, w_out, bsz, seq):
    wg = w_in[:, A_QK + 2 * A_V:]
    H = A_HEADS
    pick = lambda a, s: jnp.concatenate([a[..., s * H:(s + 1) * H], a[..., (s + 2) * H:(s + 3) * H]], axis=-1)
    w_li, w_lf = pick(wg, 0), pick(wg, 1)
    gate_pad = jnp.zeros((w_in.shape[0], 128 - 4 * H), w_in.dtype)
    w_tok = jnp.concatenate([w_in[:, :A_QK], w_li, w_lf, gate_pad], axis=1).astype(BF16)
    w_feat = jnp.concatenate([w_in[:, A_QK:A_QK + 2 * A_V], w_li, w_lf], axis=1).T.astype(BF16)
    gb = gate_b.astype(F32)[None, :]
    b_li, b_lf = pick(gb, 0), pick(gb, 1)
    k, qT, vT, oT, li, lf, liT, lfT = _a_proj(x, g, w_tok, w_feat, b_li, b_lf,
                                              conv_w.astype(F32), conv_b.astype(F32)[None, :], bsz, seq)
    hfT = _mlstm(k, qT, vT, li, lf, liT, lfT, bsz, seq, reverse=False)
    hT = _mlstm(k, qT, vT, li, lf, liT, lfT, bsz, seq, reverse=True, add=hfT)
    hg = jnp.broadcast_to(head_g.astype(F32)[:, None], (A_V, ROW_TILE))
    return _a_out(x, hT, oT, hg, w_out.astype(BF16))


def _dilated_mixer(x, g, w_in, w_out, bsz, seq):
    slabs = _b_proj(x, g, w_in.astype(BF16))
    outs, stats = [], []
    for gi, (_, dil) in enumerate(B_GROUPS):
        o, l = _b_attn(slabs[3 * gi], slabs[3 * gi + 1], slabs[3 * gi + 2], _attn_bias(dil),
                       bsz, seq, dil)
        outs.append(o)
        stats.append(l)
    return _b_out(x, outs, stats, w_out.astype(BF16))


def kernel(x, norm_ffn1, ffn1_gate, ffn1_up, ffn1_down, norm_mix, a_w_in, a_conv_w, a_conv_b,
           a_gate_b, a_head_g, a_w_out, b_w_in, b_w_out, norm_ffn2, ffn2_gate, ffn2_up,
           ffn2_down, norm_final):
    bsz, seq, d = x.shape
    h = x.reshape(bsz * seq, d)
    row = lambda a: a.astype(F32)[None, :]
    g_final = row(norm_final)
    for i in range(DEPTH):
        h = _ffn(h, row(norm_ffn1[i]), ffn1_gate, ffn1_up, ffn1_down, i, g_final, False)
        j = i // 2
        if i % 2 == 0:
            h = _mlstm_mixer(h, row(norm_mix[i]), a_w_in[j], a_conv_w[j], a_conv_b[j],
                             a_gate_b[j], a_head_g[j], a_w_out[j], bsz, seq)
        else:
            h = _dilated_mixer(h, row(norm_mix[i]), b_w_in[j], b_w_out[j], bsz, seq)
        h = _ffn(h, row(norm_ffn2[i]), ffn2_gate, ffn2_up, ffn2_down, i, g_final, i == DEPTH - 1)
    return h.reshape(bsz, seq, d)
```
